```python
import jax, jax.numpy as jnp
from jax import lax
import numpy as np

D_MODEL = 1024
BATCH = 1
SEQ = 16384
DEPTH = 1

CHUNK = 64
C_CONV = 512
CONV_WIDTH = 31
N_HEADS = 8
HEAD_DIM = 64
D_ATTN = N_HEADS * HEAD_DIM
IDX_HEADS = 8
IDX_DIM = 64
TOPK_KEYS_MAX = 256
Q_BLOCK = 128
ROPE_THETA = 10000.0
D_MIX = C_CONV + D_ATTN
N_IN = 2 * C_CONV + 3 * D_ATTN + IDX_HEADS * IDX_DIM + IDX_DIM + IDX_HEADS
N_EXPERTS = 32
TOP_K_EXPERTS = 4
D_EXPERT = 1024
SWIGLU_LIMIT = 7.0
SWIGLU_ALPHA = 1.702
EXPERT_BLOCK = 128
NORM_EPS = 1e-5

kernel_name = 'hymba_conformer_dsa_moe_block'


def rmsnorm(x, g):
    xf = x.astype(jnp.float32)
    y = xf * lax.rsqrt(jnp.mean(xf * xf, axis=-1, keepdims=True) + NORM_EPS)
    return (y * g.astype(jnp.float32)).astype(x.dtype)


def layernorm(x, g, b):
    xf = x.astype(jnp.float32)
    mu = jnp.mean(xf, axis=-1, keepdims=True)
    var = jnp.mean(jnp.square(xf - mu), axis=-1, keepdims=True)
    y = (xf - mu) * lax.rsqrt(var + NORM_EPS)
    return (y * g.astype(jnp.float32) + b.astype(jnp.float32)).astype(x.dtype)


def rope(x, pos):
    d = x.shape[-1]
    half = d // 2
    inv = ROPE_THETA ** (-jnp.arange(half, dtype=jnp.float32) / half)
    ang = pos.astype(jnp.float32)[..., None] * inv
    ang = ang.reshape(ang.shape[:2] + (1,) * (x.ndim - 3) + (half,))
    cos, sin = jnp.cos(ang), jnp.sin(ang)
    xf = x.astype(jnp.float32)
    x1, x2 = xf[..., :half], xf[..., half:]
    return jnp.concatenate([x1 * cos - x2 * sin, x2 * cos + x1 * sin], axis=-1).astype(x.dtype)


def causal_depthwise_conv(u, w, b):
    out = lax.conv_general_dilated(
        u, w[:, None, :].astype(u.dtype), window_strides=(1,),
        padding=[(CONV_WIDTH - 1, 0)], dimension_numbers=('NWC', 'WIO', 'NWC'),
        feature_group_count=u.shape[-1])
    return out + b.astype(u.dtype)


def dsa_attention(q, k, v, qi, ki, wi, pos):
    B, S = pos.shape
    nb = S // Q_BLOCK
    k_sel = min(TOPK_KEYS_MAX, S // 4)
    chunk = pos // CHUNK
    gather = jax.vmap(lambda a, i: a[i])

    def to_blocks(a):
        return jnp.moveaxis(a.reshape((B, nb, Q_BLOCK) + a.shape[2:]), 1, 0)

    def one_block(args):
        qb, qib, wib, cqb = args
        s_h = jax.nn.relu(jnp.einsum('bqhd,bsd->bqhs', qib, ki).astype(jnp.float32))
        s_idx = jnp.einsum('bqhs,bqh->bqs', s_h, wib.astype(jnp.float32))
        adm = chunk[:, None, :] <= cqb[:, :, None]
        s_idx = jnp.where(adm, s_idx, -jnp.inf)
        _, sel = lax.top_k(s_idx, k_sel)
        kg = gather(k, sel)
        vg = gather(v, sel)
        valid = gather(chunk, sel) <= cqb[:, :, None]
        logits = jnp.einsum('bqhd,bqkhd->bqhk', qb, kg).astype(jnp.float32) * (HEAD_DIM ** -0.5)
        logits = jnp.where(valid[:, :, None, :], logits, -jnp.inf)
        p = jax.nn.softmax(logits, axis=-1).astype(vg.dtype)
        return jnp.einsum('bqhk,bqkhd->bqhd', p, vg)

    out = lax.map(one_block, (to_blocks(q), to_blocks(qi), to_blocks(wi), to_blocks(chunk)))
    return jnp.moveaxis(out, 0, 1).reshape(B, S, -1)


def moe_ffn(xn, w_router, b_router, w_gate_up, b_gate_up, w_down, b_down):
    B, S, D = xn.shape
    T = B * S
    xf = xn.reshape(T, D)
    logits = (xf @ w_router + b_router).astype(jnp.float32)
    top_vals, top_idx = lax.top_k(logits, TOP_K_EXPERTS)
    gates = jax.nn.softmax(top_vals, axis=-1)
    n_pairs = T * TOP_K_EXPERTS
    expert_flat = top_idx.reshape(-1).astype(jnp.int32)
    token_flat = jnp.arange(n_pairs, dtype=jnp.int32) // TOP_K_EXPERTS
    gate_flat = gates.reshape(-1)
    order = jnp.argsort(expert_flat, stable=True)
    sorted_expert = expert_flat[order]
    counts = jax.ops.segment_sum(jnp.ones_like(expert_flat), expert_flat, num_segments=N_EXPERTS)
    padded = (counts + EXPERT_BLOCK - 1) // EXPERT_BLOCK * EXPERT_BLOCK
    start = jnp.cumsum(counts) - counts
    padded_end = jnp.cumsum(padded)
    padded_start = padded_end - padded
    dest = padded_start[sorted_expert] + jnp.arange(n_pairs, dtype=jnp.int32) - start[sorted_expert]
    n_blocks = -(-n_pairs // EXPERT_BLOCK) + N_EXPERTS
    n_rows = n_blocks * EXPERT_BLOCK
    row_token = jnp.zeros((n_rows,), jnp.int32).at[dest].set(token_flat[order])
    row_gate = jnp.zeros((n_rows,), jnp.float32).at[dest].set(gate_flat[order])
    block_start = jnp.arange(n_blocks, dtype=jnp.int32) * EXPERT_BLOCK
    block_expert = jnp.minimum(jnp.searchsorted(padded_end, block_start, side='right'), N_EXPERTS - 1)

    def expert_block(args):
        xb, e = args
        gu = xb @ w_gate_up[e] + b_gate_up[e]
        g, u = jnp.split(gu, 2, axis=-1)
        g = jnp.minimum(g, SWIGLU_LIMIT)
        u = jnp.clip(u, -SWIGLU_LIMIT, SWIGLU_LIMIT)
        hdn = g * jax.nn.sigmoid(SWIGLU_ALPHA * g) * (u + 1)
        return hdn @ w_down[e] + b_down[e]

    xs = xf[row_token].reshape(n_blocks, EXPERT_BLOCK, D)
    ys = lax.map(expert_block, (xs, block_expert)).reshape(n_rows, D)
    ys = ys * row_gate[:, None].astype(ys.dtype)
    out = jnp.zeros_like(xf).at[row_token].add(ys)
    return out.reshape(B, S, D)


def setup_inputs(seed: int = 0) -> dict:
    key = jax.random.key(seed)
    ks = jax.random.split(key, 20)
    f32 = jnp.float32

    def nrm(k, shape, scale):
        return jax.random.normal(k, shape, f32) * scale

    return {
        'x': nrm(ks[0], (BATCH, SEQ, D_MODEL), 1.0),
        'positions': jnp.broadcast_to(jnp.arange(SEQ, dtype=jnp.int32), (BATCH, SEQ)),
        'norm_mix_g': 1.0 + nrm(ks[1], (DEPTH, D_MODEL), 0.02),
        'w_in': nrm(ks[2], (DEPTH, D_MODEL, N_IN), D_MODEL ** -0.5),
        'idx_k_norm_g': 1.0 + nrm(ks[3], (DEPTH, IDX_DIM), 0.02),
        'idx_k_norm_b': nrm(ks[4], (DEPTH, IDX_DIM), 0.02),
        'conv_w': nrm(ks[5], (DEPTH, CONV_WIDTH, C_CONV), CONV_WIDTH ** -0.5),
        'conv_b': nrm(ks[6], (DEPTH, C_CONV), 0.02),
        'conv_norm_g': 1.0 + nrm(ks[7], (DEPTH, C_CONV), 0.02),
        'conv_norm_b': nrm(ks[8], (DEPTH, C_CONV), 0.02),
        'w_out': nrm(ks[9], (DEPTH, D_MIX, D_MODEL), D_MIX ** -0.5),
        'norm_ffn_g': 1.0 + nrm(ks[10], (DEPTH, D_MODEL), 0.02),
        'w_router': nrm(ks[11], (DEPTH, D_MODEL, N_EXPERTS), D_MODEL ** -0.5),
        'b_router': nrm(ks[12], (DEPTH, N_EXPERTS), 0.01),
        'w_gate_up': nrm(ks[13], (DEPTH, N_EXPERTS, D_MODEL, 2 * D_EXPERT), D_MODEL ** -0.5),
        'b_gate_up': nrm(ks[14], (DEPTH, N_EXPERTS, 2 * D_EXPERT), 0.01),
        'w_down': nrm(ks[15], (DEPTH, N_EXPERTS, D_EXPERT, D_MODEL), D_EXPERT ** -0.5),
        'b_down': nrm(ks[16], (DEPTH, N_EXPERTS, D_MODEL), 0.01),
        'norm_final_g': 1.0 + nrm(ks[17], (D_MODEL,), 0.02),
    }


def reference(x, positions, norm_mix_g, w_in, idx_k_norm_g, idx_k_norm_b, conv_w, conv_b,
              conv_norm_g, conv_norm_b, w_out, norm_ffn_g, w_router, b_router, w_gate_up,
              b_gate_up, w_down, b_down, norm_final_g):
    B, S, _ = x.shape
    sizes = (C_CONV, C_CONV, D_ATTN, D_ATTN, D_ATTN, IDX_HEADS * IDX_DIM, IDX_DIM, IDX_HEADS)
    cuts = [sum(sizes[:i + 1]) for i in range(len(sizes) - 1)]
    h = x
    for l in range(DEPTH):
        hn = rmsnorm(h, norm_mix_g[l])
        proj = hn @ w_in[l]
        glu_val, glu_gate, q, k, v, qi, ki, wi = jnp.split(proj, cuts, axis=-1)
        u = glu_val * jax.nn.sigmoid(glu_gate)
        u = causal_depthwise_conv(u, conv_w[l], conv_b[l])
        u = jax.nn.silu(layernorm(u, conv_norm_g[l], conv_norm_b[l]))
        q = rope(q.reshape(B, S, N_HEADS, HEAD_DIM), positions)
        k = rope(k.reshape(B, S, N_HEADS, HEAD_DIM), positions)
        v = v.reshape(B, S, N_HEADS, HEAD_DIM)
        qi = rope(qi.reshape(B, S, IDX_HEADS, IDX_DIM), positions)
        ki = rope(layernorm(ki, idx_k_norm_g[l], idx_k_norm_b[l]), positions)
        wi = wi * (IDX_HEADS ** -0.5 * IDX_DIM ** -0.5)
        a = dsa_attention(q, k, v, qi, ki, wi, positions)
        h = h + jnp.concatenate([u, a], axis=-1) @ w_out[l]
        h = h + moe_ffn(rmsnorm(h, norm_ffn_g[l]), w_router[l], b_router[l], w_gate_up[l],
                        b_gate_up[l], w_down[l], b_down[l])
    return rmsnorm(h, norm_final_g)
```

```python
import functools

import jax
import jax.numpy as jnp
from jax import lax
from jax.experimental import pallas as pl
from jax.experimental.pallas import tpu as pltpu

D_MODEL = 1024
CHUNK = 64
C_CONV = 512
CONV_WIDTH = 31
N_HEADS = 8
HEAD_DIM = 64
D_ATTN = N_HEADS * HEAD_DIM
IDX_HEADS = 8
IDX_DIM = 64
TOPK_KEYS_MAX = 256
ROPE_THETA = 10000.0
N_EXPERTS = 32
TOP_K_EXPERTS = 4
D_EXPERT = 1024
SWIGLU_LIMIT = 7.0
SWIGLU_ALPHA = 1.702
NORM_EPS = 1e-5

LANES = 128
CONV_HALO = 32
ROW_TILE = 256
Q_BLOCK = 128
KEY_CHUNK = 512
ATT_TQ = 256
ATT_TK = 512
EXPERT_BLOCK = 256
ROUTE_TILE = 128
VMEM_LIMIT = 56 * 1024 * 1024

_NT = (((1,), (1,)), ((), ()))


def _cparams(sem):
    return pltpu.CompilerParams(dimension_semantics=sem, vmem_limit_bytes=VMEM_LIMIT)


def _rope128(xp, cos_t, sin_lo, sin_hi):
    return xp * cos_t + pltpu.roll(xp, 96, 1) * sin_lo + pltpu.roll(xp, 32, 1) * sin_hi


def _inproj_kernel(x_ref, g_ref, wm_ref, wt_ref, cos_ref, slo_ref, shi_ref, cw_ref, cb_ref,
                   cng_ref, cnb_ref, kng_ref, knb_ref,
                   u_ref, q_ref, k_ref, v_ref, qi_ref, tail_ref, ubuf_ref):
    ts = x_ref.shape[0]
    x = x_ref[...]
    ms = jnp.mean(x * x, axis=-1, keepdims=True)
    hn = (x * lax.rsqrt(ms + NORM_EPS) * g_ref[...]).astype(jnp.bfloat16)
    proj = jnp.dot(hn, wm_ref[...], preferred_element_type=jnp.float32)
    tail = jnp.dot(hn, wt_ref[...], preferred_element_type=jnp.float32)

    u = proj[:, 0:C_CONV] * jax.nn.sigmoid(proj[:, C_CONV:2 * C_CONV])

    @pl.when(pl.program_id(0) == 0)
    def _():
        ubuf_ref[0:CONV_HALO, :] = jnp.zeros((CONV_HALO, C_CONV), jnp.float32)

    ubuf_ref[CONV_HALO:CONV_HALO + ts, :] = u
    acc = jnp.zeros((ts, C_CONV), jnp.float32) + cb_ref[...]
    for kk in range(CONV_WIDTH):
        off = CONV_HALO - (CONV_WIDTH - 1) + kk
        acc = acc + ubuf_ref[off:off + ts, :] * cw_ref[kk:kk + 1, :]
    ubuf_ref[0:CONV_HALO, :] = ubuf_ref[ts:ts + CONV_HALO, :]
    mu = jnp.mean(acc, axis=-1, keepdims=True)
    d = acc - mu
    var = jnp.mean(d * d, axis=-1, keepdims=True)
    yn = d * lax.rsqrt(var + NORM_EPS) * cng_ref[...] + cnb_ref[...]
    u_ref[...] = (yn * jax.nn.sigmoid(yn)).astype(jnp.bfloat16)

    cos_t, sin_lo, sin_hi = cos_ref[...], slo_ref[...], shi_ref[...]
    base = 2 * C_CONV
    for p in range(D_ATTN // LANES):
        lo = p * LANES
        qp = proj[:, base + lo:base + lo + LANES]
        q_ref[:, lo:lo + LANES] = (_rope128(qp, cos_t, sin_lo, sin_hi) * (HEAD_DIM ** -0.5)).astype(jnp.bfloat16)
        kp = proj[:, base + D_ATTN + lo:base + D_ATTN + lo + LANES]
        k_ref[:, lo:lo + LANES] = _rope128(kp, cos_t, sin_lo, sin_hi).astype(jnp.bfloat16)
        qip = proj[:, base + 3 * D_ATTN + lo:base + 3 * D_ATTN + lo + LANES]
        qi_ref[:, lo:lo + LANES] = _rope128(qip, cos_t, sin_lo, sin_hi).astype(jnp.bfloat16)
    v_ref[...] = proj[:, base + 2 * D_ATTN:base + 3 * D_ATTN].astype(jnp.bfloat16)

    lane = lax.broadcasted_iota(jnp.int32, tail.shape, 1)
    is_k = lane < IDX_DIM
    kmu = jnp.sum(jnp.where(is_k, tail, 0.0), axis=-1, keepdims=True) * (1.0 / IDX_DIM)
    kd = jnp.where(is_k, tail - kmu, 0.0)
    kvar = jnp.sum(kd * kd, axis=-1, keepdims=True) * (1.0 / IDX_DIM)
    kn = kd * lax.rsqrt(kvar + NORM_EPS) * kng_ref[...] + knb_ref[...]
    kr = _rope128(kn, cos_t, sin_lo, sin_hi)
    wi = tail * (IDX_HEADS ** -0.5 * IDX_DIM ** -0.5)
    tail_ref[...] = jnp.where(is_k, kr, jnp.where(lane < IDX_DIM + IDX_HEADS, wi, 0.0))


def _inproj(x, g, w_main, w_tail, cos_t, sin_lo, sin_hi, cw, cb, cng, cnb, kng, knb):
    s = x.shape[0]
    ts = min(ROW_TILE, s)
    row = lambda w: pl.BlockSpec((ts, w), lambda i: (i, 0))
    full = lambda a: pl.BlockSpec(a.shape, lambda i: (0,) * a.ndim)
    return pl.pallas_call(
        _inproj_kernel,
        grid=(s // ts,),
        in_specs=[row(D_MODEL), full(g), full(w_main), full(w_tail), row(LANES), row(LANES), row(LANES),
                  full(cw), full(cb), full(cng), full(cnb), full(kng), full(knb)],
        out_specs=[row(C_CONV), row(D_ATTN), row(D_ATTN), row(D_ATTN), row(D_ATTN), row(LANES)],
        out_shape=[jax.ShapeDtypeStruct((s, C_CONV), jnp.bfloat16),
                   jax.ShapeDtypeStruct((s, D_ATTN), jnp.bfloat16),
                   jax.ShapeDtypeStruct((s, D_ATTN), jnp.bfloat16),
                   jax.ShapeDtypeStruct((s, D_ATTN), jnp.bfloat16),
                   jax.ShapeDtypeStruct((s, D_ATTN), jnp.bfloat16),
                   jax.ShapeDtypeStruct((s, LANES), jnp.float32)],
        scratch_shapes=[pltpu.VMEM((ts + CONV_HALO, C_CONV), jnp.float32)],
        compiler_params=_cparams(("arbitrary",)),
        name="inproj",
    )(x, g, w_main, w_tail, cos_t, sin_lo, sin_hi, cw, cb, cng, cnb, kng, knb)


def _float_to_key(x):
    b = lax.bitcast_convert_type(x, jnp.int32)
    return jnp.where(b < 0, b ^ jnp.int32(0x7FFFFFFF), b)


def _key_to_float(k):
    b = jnp.where(k < 0, k ^ jnp.int32(0x7FFFFFFF), k)
    return lax.bitcast_convert_type(b, jnp.float32)


def _avg_floor(a, b):
    return (a >> 1) + (b >> 1) + (a & b & 1)


def _indexer_kernel(nkb_ref, qi_ref, tq_ref, cq_ref, ki_ref, ck_ref, bias_ref, sc_ref, *, k_sel):
    qb = qi_ref.shape[0]
    s = ki_ref.shape[0]
    kc = min(KEY_CHUNK, s)
    n_total = s // kc
    nch = (nkb_ref[pl.program_id(0)] * Q_BLOCK + kc - 1) // kc
    neg_inf = jnp.float32(-jnp.inf)

    qi = qi_ref[...]
    tq = tq_ref[...]
    cq = cq_ref[...]
    q_heads = [qi[:, h * IDX_DIM:(h + 1) * IDX_DIM] for h in range(IDX_HEADS)]
    w_heads = [tq[:, IDX_DIM + h:IDX_DIM + h + 1] for h in range(IDX_HEADS)]

    def score_chunk(c, carry):
        k0 = pl.multiple_of(c * kc, kc)
        kic = ki_ref[pl.ds(k0, kc), :]
        acc = jnp.zeros((qb, kc), jnp.float32)
        for h in range(IDX_HEADS):
            sh = lax.dot_general(q_heads[h], kic, _NT, preferred_element_type=jnp.float32)
            acc = acc + w_heads[h] * jnp.maximum(sh, 0.0)
        adm = ck_ref[:, pl.ds(k0, kc)] <= cq
        sc_ref[:, pl.ds(k0, kc)] = jnp.where(adm, acc, neg_inf)
        return carry

    lax.fori_loop(0, nch, score_chunk, 0)

    def count_rows(pred):
        def body(c, acc):
            k0 = pl.multiple_of(c * kc, kc)
            m = pred(sc_ref[:, pl.ds(k0, kc)], k0)
            for j in range(kc // LANES):
                acc = acc + jnp.where(m[:, j * LANES:(j + 1) * LANES], 1, 0)
            return acc
        acc = lax.fori_loop(0, nch, body, jnp.zeros((qb, LANES), jnp.int32))
        return jnp.sum(acc, axis=1, keepdims=True)

    def count_ge(cand):
        cb = jnp.broadcast_to(cand, (qb, kc))
        return count_rows(lambda x, k0: x >= cb)

    lo0 = jnp.full((qb, 1), _float_to_key(jnp.float32(-jnp.inf)) + 1, jnp.int32)
    hi0 = jnp.full((qb, 1), _float_to_key(jnp.float32(jnp.inf)), jnp.int32)
    n_adm = count_ge(_key_to_float(lo0))
    k_eff = jnp.minimum(n_adm, k_sel)

    def settled(lo, hi, clo):
        return (clo == k_eff) | (_avg_floor(lo, hi) == lo)

    def cond(st):
        return st[5] > 0

    def step(st):
        lo, hi, clo, chi, done, _ = st
        mid = _avg_floor(lo, hi)
        cnt = count_ge(_key_to_float(mid))
        ge = cnt >= k_eff
        up = jnp.logical_and(done == 0, ge)
        dn = jnp.logical_and(done == 0, jnp.logical_not(ge))
        lo = jnp.where(up, mid, lo)
        clo = jnp.where(up, cnt, clo)
        hi = jnp.where(dn, mid, hi)
        chi = jnp.where(dn, cnt, chi)
        done = jnp.where(settled(lo, hi, clo), 1, done)
        return lo, hi, clo, chi, done, jnp.sum(1 - done)

    done0 = jnp.where(settled(lo0, hi0, n_adm), 1, 0)
    st0 = (lo0, hi0, n_adm, jnp.zeros((qb, 1), jnp.int32), done0, jnp.sum(1 - done0))
    lo, hi, clo, chi, _, _ = lax.while_loop(cond, step, st0)
    thr = _key_to_float(lo)

    tie = clo > k_eff
    need = k_eff - chi
    n_tie = jnp.sum(jnp.where(tie, 1, 0))

    def tie_limit():
        thr_b = jnp.broadcast_to(thr, (qb, kc))

        def cnt_upto(m):
            mb = jnp.broadcast_to(m, (qb, kc))
            def pred(x, k0):
                idx = k0 + lax.broadcasted_iota(jnp.int32, (qb, kc), 1)
                return jnp.logical_and(x == thr_b, idx <= mb)
            return count_rows(pred)

        def body(_, jj):
            jl, jh = jj
            m = (jl + jh) >> 1
            ok = cnt_upto(m) >= need
            return jnp.where(ok, jl, m), jnp.where(ok, m, jh)

        n_it = max(1, (s - 1).bit_length()) + 1
        jl0 = jnp.full((qb, 1), -1, jnp.int32)
        jh0 = jnp.full((qb, 1), s - 1, jnp.int32)
        _, jh = lax.fori_loop(0, n_it, body, (jl0, jh0))
        return jnp.where(tie, jh, s)

    jlim = lax.cond(n_tie > 0, tie_limit, lambda: jnp.full((qb, 1), s, jnp.int32))

    thr_b = jnp.broadcast_to(thr, (qb, kc))
    jlim_b = jnp.broadcast_to(jlim, (qb, kc))

    def emit_chunk(c, carry):
        k0 = pl.multiple_of(c * kc, kc)
        x = sc_ref[:, pl.ds(k0, kc)]
        idx = k0 + lax.broadcasted_iota(jnp.int32, (qb, kc), 1)
        sel = jnp.logical_or(x > thr_b, jnp.logical_and(x == thr_b, idx <= jlim_b))
        bias_ref[:, pl.ds(k0, kc)] = jnp.where(sel, 0.0, neg_inf).astype(jnp.bfloat16)
        return carry

    lax.fori_loop(0, nch, emit_chunk, 0)

    def fill_chunk(c, carry):
        k0 = pl.multiple_of(c * kc, kc)
        bias_ref[:, pl.ds(k0, kc)] = jnp.full((qb, kc), neg_inf, jnp.bfloat16)
        return carry

    lax.fori_loop(nch, n_total, fill_chunk, 0)


def _indexer(nkb, qi, tail, cq, ki, ck, k_sel):
    s = qi.shape[0]
    qb = min(Q_BLOCK, s)
    grid_spec = pltpu.PrefetchScalarGridSpec(
        num_scalar_prefetch=1,
        grid=(s // qb,),
        in_specs=[pl.BlockSpec((qb, D_ATTN), lambda i, n: (i, 0)),
                  pl.BlockSpec((qb, LANES), lambda i, n: (i, 0)),
                  pl.BlockSpec((qb, 1), lambda i, n: (i, 0)),
                  pl.BlockSpec((s, IDX_DIM), lambda i, n: (0, 0)),
                  pl.BlockSpec((1, s), lambda i, n: (0, 0))],
        out_specs=pl.BlockSpec((qb, s), lambda i, n: (i, 0)),
        scratch_shapes=[pltpu.VMEM((qb, s), jnp.float32)],
    )
    return pl.pallas_call(
        functools.partial(_indexer_kernel, k_sel=k_sel),
        grid_spec=grid_spec,
        out_shape=jax.ShapeDtypeStruct((s, s), jnp.bfloat16),
        compiler_params=_cparams(("arbitrary",)),
        name="indexer",
    )(nkb, qi, tail, cq, ki, ck)


def _attention_kernel(nkt_ref, q_ref, k_ref, v_ref, b_ref, o_ref, qm_ref, m_ref, l_ref, acc_ref):
    i, j = pl.program_id(0), pl.program_id(1)
    tq = q_ref.shape[0]
    n_pairs = D_ATTN // LANES
    lane = lax.broadcasted_iota(jnp.int32, (tq, LANES), 1)
    first_half = lane < HEAD_DIM

    @pl.when(j == 0)
    def _():
        q = q_ref[...]
        zero = jnp.zeros((tq, LANES), q.dtype)
        for p in range(n_pairs):
            qp = q[:, p * LANES:(p + 1) * LANES]
            qm_ref[2 * p] = jnp.where(first_half, qp, zero)
            qm_ref[2 * p + 1] = jnp.where(first_half, zero, qp)
        m_ref[...] = jnp.full(m_ref.shape, -1e30, jnp.float32)
        l_ref[...] = jnp.zeros(l_ref.shape, jnp.float32)
        acc_ref[...] = jnp.zeros(acc_ref.shape, jnp.float32)

    @pl.when(j < nkt_ref[i])
    def _():
        bias = b_ref[...].astype(jnp.float32)
        for h in range(N_HEADS):
            p = h // 2
            kp = k_ref[:, p * LANES:(p + 1) * LANES]
            vp = v_ref[:, p * LANES:(p + 1) * LANES]
            sc = lax.dot_general(qm_ref[h], kp, _NT, preferred_element_type=jnp.float32) + bias
            m_prev = m_ref[h]
            m_new = jnp.maximum(m_prev, jnp.max(sc, axis=-1, keepdims=True))
            alpha = jnp.exp(m_prev - m_new)
            pr = jnp.exp(sc - m_new)
            l_ref[h] = alpha * l_ref[h] + jnp.sum(pr, axis=-1, keepdims=True)
            acc_ref[h] = alpha * acc_ref[h] + jnp.dot(pr.astype(jnp.bfloat16), vp,
                                                      preferred_element_type=jnp.float32)
            m_ref[h] = m_new

    @pl.when(j == nkt_ref[i] - 1)
    def _():
        for p in range(n_pairs):
            o0 = acc_ref[2 * p] / l_ref[2 * p]
            o1 = acc_ref[2 * p + 1] / l_ref[2 * p + 1]
            o_ref[:, p * LANES:(p + 1) * LANES] = jnp.where(first_half, o0, o1).astype(o_ref.dtype)


def _attention(nkt, q, k, v, bias):
    s = q.shape[0]
    tq, tk = min(ATT_TQ, s), min(ATT_TK, s)
    kv_map = lambda i, j, n: (jnp.minimum(j, n[i] - 1), 0)
    grid_spec = pltpu.PrefetchScalarGridSpec(
        num_scalar_prefetch=1,
        grid=(s // tq, s // tk),
        in_specs=[pl.BlockSpec((tq, D_ATTN), lambda i, j, n: (i, 0)),
                  pl.BlockSpec((tk, D_ATTN), kv_map),
                  pl.BlockSpec((tk, D_ATTN), kv_map),
                  pl.BlockSpec((tq, tk), lambda i, j, n: (i, jnp.minimum(j, n[i] - 1)))],
        out_specs=pl.BlockSpec((tq, D_ATTN), lambda i, j, n: (i, 0)),
        scratch_shapes=[pltpu.VMEM((N_HEADS, tq, LANES), jnp.bfloat16),
                        pltpu.VMEM((N_HEADS, tq, 1), jnp.float32),
                        pltpu.VMEM((N_HEADS, tq, 1), jnp.float32),
                        pltpu.VMEM((N_HEADS, tq, LANES), jnp.float32)],
    )
    return pl.pallas_call(
        _attention_kernel,
        grid_spec=grid_spec,
        out_shape=jax.ShapeDtypeStruct((s, D_ATTN), jnp.bfloat16),
        compiler_params=_cparams(("arbitrary", "arbitrary")),
        name="attention",
    )(nkt, q, k, v, bias)


def _outproj_kernel(x_ref, u_ref, a_ref, wu_ref, wa_ref, g_ref, wr_ref, br_ref,
                    h_ref, xn_ref, eidx_ref, gate_ref, rank_ref, cnt_ref, carry_ref):
    ts = x_ref.shape[0]

    @pl.when(pl.program_id(0) == 0)
    def _():
        carry_ref[...] = jnp.zeros(carry_ref.shape, jnp.float32)

    h = (x_ref[...]
         + jnp.dot(u_ref[...], wu_ref[...], preferred_element_type=jnp.float32)
         + jnp.dot(a_ref[...], wa_ref[...], preferred_element_type=jnp.float32))
    h_ref[...] = h
    ms = jnp.mean(h * h, axis=-1, keepdims=True)
    xn = h * lax.rsqrt(ms + NORM_EPS) * g_ref[...]
    xn_ref[...] = xn
    logits = jnp.dot(xn, wr_ref[...], preferred_element_type=jnp.float32,
                     precision=lax.Precision.HIGHEST) + br_ref[...]

    lane = lax.broadcasted_iota(jnp.int32, (ts, LANES), 1)
    work = logits
    vals, idxs = [], []
    multi = jnp.zeros((ts, LANES), jnp.float32)
    for _ in range(TOP_K_EXPERTS):
        mx = jnp.max(work, axis=-1, keepdims=True)
        ix = jnp.min(jnp.where(work == mx, lane, LANES), axis=-1, keepdims=True)
        hit = lane == ix
        multi = jnp.where(hit, 1.0, multi)
        work = jnp.where(hit, -jnp.inf, work)
        vals.append(mx)
        idxs.append(ix)
    ex = [jnp.exp(v - vals[0]) for v in vals]
    den = ex[0] + ex[1] + ex[2] + ex[3]

    r = lax.broadcasted_iota(jnp.int32, (ts, ts), 0)
    c = lax.broadcasted_iota(jnp.int32, (ts, ts), 1)
    below = jnp.where(c < r, 1.0, 0.0).astype(jnp.bfloat16)
    prior = jnp.dot(below, multi.astype(jnp.bfloat16), preferred_element_type=jnp.float32) + carry_ref[...]
    eidx = jnp.zeros((ts, LANES), jnp.int32)
    gate = jnp.zeros((ts, LANES), jnp.float32)
    rank = jnp.zeros((ts, LANES), jnp.int32)
    for kk in range(TOP_K_EXPERTS):
        rk = jnp.sum(jnp.where(lane == idxs[kk], prior, 0.0), axis=-1, keepdims=True)
        eidx = jnp.where(lane == kk, idxs[kk], eidx)
        gate = jnp.where(lane == kk, ex[kk] / den, gate)
        rank = jnp.where(lane == kk, rk.astype(jnp.int32), rank)
    eidx_ref[...] = eidx
    gate_ref[...] = gate
    rank_ref[...] = rank
    carry_ref[...] = carry_ref[...] + jnp.sum(multi, axis=0, keepdims=True)
    cnt_ref[...] = carry_ref[...]


def _outproj(x, u, a, w_u, w_a, g, w_r, b_r):
    s = x.shape[0]
    ts = min(ROW_TILE, s)
    row = lambda w: pl.BlockSpec((ts, w), lambda i: (i, 0))
    full = lambda arr: pl.BlockSpec(arr.shape, lambda i: (0,) * arr.ndim)
    return pl.pallas_call(
        _outproj_kernel,
        grid=(s // ts,),
        in_specs=[row(D_MODEL), row(C_CONV), row(D_ATTN), full(w_u), full(w_a), full(g), full(w_r), full(b_r)],
        out_specs=[row(D_MODEL), row(D_MODEL), row(LANES), row(LANES), row(LANES),
                   pl.BlockSpec((1, LANES), lambda i: (0, 0))],
        out_shape=[jax.ShapeDtypeStruct((s, D_MODEL), jnp.float32),
                   jax.ShapeDtypeStruct((s, D_MODEL), jnp.float32),
                   jax.ShapeDtypeStruct((s, LANES), jnp.int32),
                   jax.ShapeDtypeStruct((s, LANES), jnp.float32),
                   jax.ShapeDtypeStruct((s, LANES), jnp.int32),
                   jax.ShapeDtypeStruct((1, LANES), jnp.float32)],
        scratch_shapes=[pltpu.VMEM((1, LANES), jnp.float32)],
        compiler_params=_cparams(("arbitrary",)),
        name="outproj_router",
    )(x, u, a, w_u, w_a, g, w_r, b_r)


def _row_copy(src_ref, dst_ref, sem, src_row, dst_row):
    return pltpu.make_async_copy(src_ref.at[pl.ds(src_row, 1), :], dst_ref.at[pl.ds(dst_row, 1), :], sem)


def _dispatch_kernel(dest_ref, xn_ref, xs_in_ref, xs_ref, sem):
    del xs_in_ref
    tt = xn_ref.shape[0]

    def start(r, carry):
        for kk in range(TOP_K_EXPERTS):
            _row_copy(xn_ref, xs_ref, sem, r, dest_ref[0, 0, r * TOP_K_EXPERTS + kk]).start()
        return carry

    def wait(r, carry):
        for kk in range(TOP_K_EXPERTS):
            _row_copy(xn_ref, xs_ref, sem, r, dest_ref[0, 0, r * TOP_K_EXPERTS + kk]).wait()
        return carry

    lax.fori_loop(0, tt, start, 0)
    lax.fori_loop(0, tt, wait, 0)


def _dispatch(dest3, xn, xs_init):
    s = xn.shape[0]
    tt = min(ROUTE_TILE, s)
    return pl.pallas_call(
        _dispatch_kernel,
        grid=(s // tt,),
        in_specs=[pl.BlockSpec((1, 1, tt * TOP_K_EXPERTS), lambda i: (i, 0, 0), memory_space=pltpu.SMEM),
                  pl.BlockSpec((tt, D_MODEL), lambda i: (i, 0)),
                  pl.BlockSpec(memory_space=pl.ANY)],
        out_specs=pl.BlockSpec(memory_space=pl.ANY),
        out_shape=jax.ShapeDtypeStruct(xs_init.shape, xs_init.dtype),
        scratch_shapes=[pltpu.SemaphoreType.DMA(())],
        input_output_aliases={2: 0},
        compiler_params=_cparams(("arbitrary",)),
        name="dispatch",
    )(dest3, xn, xs_init)


def _experts_kernel(be_ref, nb_ref, xs_ref, wgu_ref, bgu_ref, wdn_ref, bdn_ref, ys_ref, wgu_bf, wdn_bf):
    b = pl.program_id(0)
    prev = be_ref[jnp.maximum(b - 1, 0)]
    fresh = jnp.logical_or(b == 0, be_ref[b] != prev)

    @pl.when(jnp.logical_and(b < nb_ref[0], fresh))
    def _():
        wgu_bf[...] = wgu_ref[0].astype(jnp.bfloat16)
        wdn_bf[...] = wdn_ref[0].astype(jnp.bfloat16)

    @pl.when(b < nb_ref[0])
    def _():
        xb = xs_ref[...].astype(jnp.bfloat16)
        gu = jnp.dot(xb, wgu_bf[...], preferred_element_type=jnp.float32) + bgu_ref[0]
        g = jnp.minimum(gu[:, :D_EXPERT], SWIGLU_LIMIT)
        u = jnp.clip(gu[:, D_EXPERT:], -SWIGLU_LIMIT, SWIGLU_LIMIT)
        hdn = g * jax.nn.sigmoid(SWIGLU_ALPHA * g) * (u + 1.0)
        ys_ref[...] = jnp.dot(hdn.astype(jnp.bfloat16), wdn_bf[...],
                              preferred_element_type=jnp.float32) + bdn_ref[0]

    @pl.when(b >= nb_ref[0])
    def _():
        ys_ref[...] = jnp.zeros(ys_ref.shape, ys_ref.dtype)


def _experts(block_expert, n_used, xs, w_gu, b_gu, w_dn, b_dn):
    n_rows = xs.shape[0]
    blk = EXPERT_BLOCK
    row_map = lambda b, be, nb: (jnp.minimum(b, nb[0] - 1), 0)
    exp_map = lambda b, be, nb: (be[b], 0, 0)
    grid_spec = pltpu.PrefetchScalarGridSpec(
        num_scalar_prefetch=2,
        grid=(n_rows // blk,),
        in_specs=[pl.BlockSpec((blk, D_MODEL), row_map),
                  pl.BlockSpec((1, D_MODEL, 2 * D_EXPERT), exp_map),
                  pl.BlockSpec((1, 1, 2 * D_EXPERT), exp_map),
                  pl.BlockSpec((1, D_EXPERT, D_MODEL), exp_map),
                  pl.BlockSpec((1, 1, D_MODEL), exp_map)],
        out_specs=pl.BlockSpec((blk, D_MODEL), lambda b, be, nb: (b, 0)),
        scratch_shapes=[pltpu.VMEM((D_MODEL, 2 * D_EXPERT), jnp.bfloat16),
                        pltpu.VMEM((D_EXPERT, D_MODEL), jnp.bfloat16)],
    )
    return pl.pallas_call(
        _experts_kernel,
        grid_spec=grid_spec,
        out_shape=jax.ShapeDtypeStruct((n_rows, D_MODEL), jnp.float32),
        compiler_params=_cparams(("arbitrary",)),
        name="experts",
    )(block_expert, n_used, xs, w_gu, b_gu, w_dn, b_dn)


def _combine_kernel(dest_ref, h_ref, gate_ref, g_ref, ys_ref, o_ref, buf_ref, sem):
    tt = h_ref.shape[0]

    def start(r, carry):
        for kk in range(TOP_K_EXPERTS):
            _row_copy(ys_ref, buf_ref.at[kk], sem, dest_ref[0, 0, r * TOP_K_EXPERTS + kk], r).start()
        return carry

    def wait(r, carry):
        for kk in range(TOP_K_EXPERTS):
            _row_copy(ys_ref, buf_ref.at[kk], sem, dest_ref[0, 0, r * TOP_K_EXPERTS + kk], r).wait()
        return carry

    lax.fori_loop(0, tt, start, 0)
    lax.fori_loop(0, tt, wait, 0)
    gate = gate_ref[...]
    h = h_ref[...]
    for kk in range(TOP_K_EXPERTS):
        h = h + gate[:, kk:kk + 1] * buf_ref[kk]
    ms = jnp.mean(h * h, axis=-1, keepdims=True)
    o_ref[...] = h * lax.rsqrt(ms + NORM_EPS) * g_ref[...]


def _combine(dest3, h, gate, g_final, ys):
    s = h.shape[0]
    tt = min(ROUTE_TILE, s)
    return pl.pallas_call(
        _combine_kernel,
        grid=(s // tt,),
        in_specs=[pl.BlockSpec((1, 1, tt * TOP_K_EXPERTS), lambda i: (i, 0, 0), memory_space=pltpu.SMEM),
                  pl.BlockSpec((tt, D_MODEL), lambda i: (i, 0)),
                  pl.BlockSpec((tt, LANES), lambda i: (i, 0)),
                  pl.BlockSpec((1, D_MODEL), lambda i: (0, 0)),
                  pl.BlockSpec(memory_space=pl.ANY)],
        out_specs=pl.BlockSpec((tt, D_MODEL), lambda i: (i, 0)),
        out_shape=jax.ShapeDtypeStruct((s, D_MODEL), jnp.float32),
        scratch_shapes=[pltpu.VMEM((TOP_K_EXPERTS, tt, D_MODEL), jnp.float32),
                        pltpu.SemaphoreType.DMA(())],
        compiler_params=_cparams(("arbitrary",)),
        name="combine",
    )(dest3, h, gate, g_final, ys)


def _rope_tables(pos):
    half = HEAD_DIM // 2
    inv = ROPE_THETA ** (-jnp.arange(half, dtype=jnp.float32) / half)
    ang = pos.astype(jnp.float32)[:, None] * inv
    cos, sin = jnp.cos(ang), jnp.sin(ang)
    zero = jnp.zeros_like(sin)
    reps = LANES // HEAD_DIM
    cos_t = jnp.tile(jnp.concatenate([cos, cos], axis=-1), (1, reps))
    sin_lo = jnp.tile(jnp.concatenate([-sin, zero], axis=-1), (1, reps))
    sin_hi = jnp.tile(jnp.concatenate([zero, sin], axis=-1), (1, reps))
    return cos_t, sin_lo, sin_hi


def _block_bounds(chunk, q_rows, k_rows):
    cq_max = jnp.max(chunk.reshape(-1, q_rows), axis=1)
    ck_min = jnp.min(chunk.reshape(-1, k_rows), axis=1)
    need = ck_min[None, :] <= cq_max[:, None]
    last = jnp.max(jnp.where(need, jnp.arange(ck_min.shape[0], dtype=jnp.int32)[None, :] + 1, 1), axis=1)
    return last.astype(jnp.int32)


def _layer(h, pos, norm_mix_g, w_in, idx_k_norm_g, idx_k_norm_b, conv_w, conv_b, conv_norm_g, conv_norm_b,
           w_out, norm_ffn_g, w_router, b_router, w_gate_up, b_gate_up, w_down, b_down, out_gain):
    s = h.shape[0]
    f32, bf16 = jnp.float32, jnp.bfloat16
    n_main = 2 * C_CONV + 3 * D_ATTN + IDX_HEADS * IDX_DIM
    w_main = w_in[:, :n_main].astype(bf16)
    w_tail = jnp.pad(w_in[:, n_main:], ((0, 0), (0, LANES - (IDX_DIM + IDX_HEADS)))).astype(bf16)
    cos_t, sin_lo, sin_hi = _rope_tables(pos)
    cw = jnp.pad(conv_w, ((0, CONV_HALO - CONV_WIDTH), (0, 0)))
    kng = jnp.pad(idx_k_norm_g, (0, LANES - IDX_DIM))[None, :]
    knb = jnp.pad(idx_k_norm_b, (0, LANES - IDX_DIM))[None, :]
    u, q, k, v, qi, tail = _inproj(h, norm_mix_g[None, :], w_main, w_tail, cos_t, sin_lo, sin_hi, cw,
                                   conv_b[None, :], conv_norm_g[None, :], conv_norm_b[None, :], kng, knb)

    chunk = pos // CHUNK
    k_sel = min(TOPK_KEYS_MAX, s // 4)
    qb = min(Q_BLOCK, s)
    ki = tail[:, :IDX_DIM].astype(bf16)
    bias = _indexer(_block_bounds(chunk, qb, qb), qi, tail, chunk[:, None], ki, chunk[None, :], k_sel)
    a = _attention(_block_bounds(chunk, min(ATT_TQ, s), min(ATT_TK, s)), q, k, v, bias)

    w_r = jnp.pad(w_router, ((0, 0), (0, LANES - N_EXPERTS)))
    b_r = jnp.pad(b_router, (0, LANES - N_EXPERTS), constant_values=-jnp.inf)[None, :]
    h1, xn, eidx, gate, rank, counts = _outproj(h, u, a, w_out[:C_CONV].astype(bf16), w_out[C_CONV:].astype(bf16),
                                                norm_ffn_g[None, :], w_r, b_r)

    blk = EXPERT_BLOCK
    cnt = counts[0, :N_EXPERTS].astype(jnp.int32)
    padded = (cnt + blk - 1) // blk * blk
    end = jnp.cumsum(padded)
    start = end - padded
    n_blocks = s * TOP_K_EXPERTS // blk + N_EXPERTS
    dest = start[eidx[:, :TOP_K_EXPERTS]] + rank[:, :TOP_K_EXPERTS]
    tt = min(ROUTE_TILE, s)
    dest3 = dest.reshape(s // tt, 1, tt * TOP_K_EXPERTS)
    block_expert = jnp.minimum(
        jnp.searchsorted(end, jnp.arange(n_blocks, dtype=jnp.int32) * blk, side='right'),
        N_EXPERTS - 1).astype(jnp.int32)
    n_used = (end[-1:] // blk).astype(jnp.int32)

    xs = _dispatch(dest3, xn, jnp.zeros((n_blocks * blk, D_MODEL), f32))
    ys = _experts(block_expert, n_used, xs, w_gate_up, b_gate_up[:, None, :], w_down, b_down[:, None, :])
    return _combine(dest3, h1, gate, out_gain[None, :], ys)


def kernel(x, positions, norm_mix_g, w_in, idx_k_norm_g, idx_k_norm_b, conv_w, conv_b, conv_norm_g, conv_norm_b,
           w_out, norm_ffn_g, w_router, b_router, w_gate_up, b_gate_up, w_down, b_down, norm_final_g):
    assert x.shape[0] == 1 and norm_mix_g.shape[0] == 1, "single sequence, single layer"
    y = _layer(x[0], positions[0], norm_mix_g[0], w_in[0], idx_k_norm_g[0], idx_k_norm_b[0], conv_w[0],
               conv_b[0], conv_norm_g[0], conv_norm_b[0], w_out[0], norm_ffn_g[0], w_router[0], b_router[0],
               w_gate_up[0], b_gate_up[0], w_down[0], b_down[0], norm_final_g)
    return y[None]
```

```python
import functools

import jax
import jax.numpy as jnp
from jax import lax
from jax.experimental import pallas as pl
from jax.experimental.pallas import tpu as pltpu

D_MODEL = 1024
CHUNK = 64
C_CONV = 512
CONV_WIDTH = 31
N_HEADS = 8
HEAD_DIM = 64
D_ATTN = N_HEADS * HEAD_DIM
IDX_HEADS = 8
IDX_DIM = 64
TOPK_KEYS_MAX = 256
ROPE_THETA = 10000.0
N_EXPERTS = 32
TOP_K_EXPERTS = 4
D_EXPERT = 1024
SWIGLU_LIMIT = 7.0
SWIGLU_ALPHA = 1.702
NORM_EPS = 1e-5

LANES = 128
SUBLANES = 8
CONV_HALO = 32
ROW_TILE = 256
Q_BLOCK = 128
KEY_CHUNK = 512
ATT_TQ = 256
ATT_TK = 512
EXPERT_BLOCK = 256
ROUTE_TILE = 128
VMEM_LIMIT = 56 * 1024 * 1024

_NT = (((1,), (1,)), ((), ()))


def _cparams(sem):
    return pltpu.CompilerParams(dimension_semantics=sem, vmem_limit_bytes=VMEM_LIMIT)


def _rope128(xp, cos_t, sin_lo, sin_hi):
    return xp * cos_t + pltpu.roll(xp, 96, 1) * sin_lo + pltpu.roll(xp, 32, 1) * sin_hi


def _inproj_kernel(x_ref, g_ref, wm_ref, wt_ref, cos_ref, slo_ref, shi_ref, cw_ref, cb_ref,
                   cng_ref, cnb_ref, kng_ref, knb_ref,
                   u_ref, q_ref, k_ref, v_ref, qi_ref, tail_ref, ubuf_ref):
    ts = x_ref.shape[0]
    x = x_ref[...]
    ms = jnp.mean(x * x, axis=-1, keepdims=True)
    hn = (x * lax.rsqrt(ms + NORM_EPS) * g_ref[...]).astype(jnp.bfloat16)
    proj = jnp.dot(hn, wm_ref[...], preferred_element_type=jnp.float32)
    tail = jnp.dot(hn, wt_ref[...], preferred_element_type=jnp.float32)

    u = proj[:, 0:C_CONV] * jax.nn.sigmoid(proj[:, C_CONV:2 * C_CONV])

    @pl.when(pl.program_id(0) == 0)
    def _():
        ubuf_ref[0:CONV_HALO, :] = jnp.zeros((CONV_HALO, C_CONV), jnp.float32)

    ubuf_ref[CONV_HALO:CONV_HALO + ts, :] = u
    acc = jnp.zeros((ts, C_CONV), jnp.float32) + cb_ref[...]
    for kk in range(CONV_WIDTH):
        off = CONV_HALO - (CONV_WIDTH - 1) + kk
        acc = acc + ubuf_ref[off:off + ts, :] * cw_ref[kk:kk + 1, :]
    ubuf_ref[0:CONV_HALO, :] = ubuf_ref[ts:ts + CONV_HALO, :]
    mu = jnp.mean(acc, axis=-1, keepdims=True)
    d = acc - mu
    var = jnp.mean(d * d, axis=-1, keepdims=True)
    yn = d * lax.rsqrt(var + NORM_EPS) * cng_ref[...] + cnb_ref[...]
    u_ref[...] = (yn * jax.nn.sigmoid(yn)).astype(jnp.bfloat16)

    cos_t, sin_lo, sin_hi = cos_ref[...], slo_ref[...], shi_ref[...]
    base = 2 * C_CONV
    for p in range(D_ATTN // LANES):
        lo = p * LANES
        qp = proj[:, base + lo:base + lo + LANES]
        q_ref[:, lo:lo + LANES] = (_rope128(qp, cos_t, sin_lo, sin_hi) * (HEAD_DIM ** -0.5)).astype(jnp.bfloat16)
        kp = proj[:, base + D_ATTN + lo:base + D_ATTN + lo + LANES]
        k_ref[:, lo:lo + LANES] = _rope128(kp, cos_t, sin_lo, sin_hi).astype(jnp.bfloat16)
        qip = proj[:, base + 3 * D_ATTN + lo:base + 3 * D_ATTN + lo + LANES]
        qi_ref[:, lo:lo + LANES] = _rope128(qip, cos_t, sin_lo, sin_hi).astype(jnp.bfloat16)
    v_ref[...] = proj[:, base + 2 * D_ATTN:base + 3 * D_ATTN].astype(jnp.bfloat16)

    lane = lax.broadcasted_iota(jnp.int32, tail.shape, 1)
    is_k = lane < IDX_DIM
    kmu = jnp.sum(jnp.where(is_k, tail, 0.0), axis=-1, keepdims=True) * (1.0 / IDX_DIM)
    kd = jnp.where(is_k, tail - kmu, 0.0)
    kvar = jnp.sum(kd * kd, axis=-1, keepdims=True) * (1.0 / IDX_DIM)
    kn = kd * lax.rsqrt(kvar + NORM_EPS) * kng_ref[...] + knb_ref[...]
    kr = _rope128(kn, cos_t, sin_lo, sin_hi)
    wi = tail * (IDX_HEADS ** -0.5 * IDX_DIM ** -0.5)
    tail_ref[...] = jnp.where(is_k, kr, jnp.where(lane < IDX_DIM + IDX_HEADS, wi, 0.0))


def _inproj(x, g, w_main, w_tail, cos_t, sin_lo, sin_hi, cw, cb, cng, cnb, kng, knb):
    s = x.shape[0]
    ts = min(ROW_TILE, s)
    row = lambda w: pl.BlockSpec((ts, w), lambda i: (i, 0))
    full = lambda a: pl.BlockSpec(a.shape, lambda i: (0,) * a.ndim)
    return pl.pallas_call(
        _inproj_kernel,
        grid=(s // ts,),
        in_specs=[row(D_MODEL), full(g), full(w_main), full(w_tail), row(LANES), row(LANES), row(LANES),
                  full(cw), full(cb), full(cng), full(cnb), full(kng), full(knb)],
        out_specs=[row(C_CONV), row(D_ATTN), row(D_ATTN), row(D_ATTN), row(D_ATTN), row(LANES)],
        out_shape=[jax.ShapeDtypeStruct((s, C_CONV), jnp.bfloat16),
                   jax.ShapeDtypeStruct((s, D_ATTN), jnp.bfloat16),
                   jax.ShapeDtypeStruct((s, D_ATTN), jnp.bfloat16),
                   jax.ShapeDtypeStruct((s, D_ATTN), jnp.bfloat16),
                   jax.ShapeDtypeStruct((s, D_ATTN), jnp.bfloat16),
                   jax.ShapeDtypeStruct((s, LANES), jnp.float32)],
        scratch_shapes=[pltpu.VMEM((ts + CONV_HALO, C_CONV), jnp.float32)],
        compiler_params=_cparams(("arbitrary",)),
        name="inproj",
    )(x, g, w_main, w_tail, cos_t, sin_lo, sin_hi, cw, cb, cng, cnb, kng, knb)


def _float_to_key(x):
    b = lax.bitcast_convert_type(x, jnp.int32)
    return jnp.where(b < 0, b ^ jnp.int32(0x7FFFFFFF), b)


def _key_to_float(k):
    b = jnp.where(k < 0, k ^ jnp.int32(0x7FFFFFFF), k)
    return lax.bitcast_convert_type(b, jnp.float32)


def _avg_floor(a, b):
    return (a >> 1) + (b >> 1) + (a & b & 1)


def _probit_upper(q):
    p = jnp.minimum(q, 1.0 - q)
    t = jnp.sqrt(-2.0 * jnp.log(p))
    z = t - (2.515517 + t * (0.802853 + t * 0.010328)) / (1.0 + t * (1.432788 + t * (0.189269 + t * 0.001308)))
    return jnp.where(q <= 0.5, z, -z)


def _indexer_kernel(nkb_ref, qi_ref, wi_ref, cq_ref, ki_ref, ck_ref, bias_ref, sc_ref, *, k_sel):
    qb = qi_ref.shape[0]
    s = ki_ref.shape[0]
    kc = min(KEY_CHUNK, s)
    grp = kc // SUBLANES
    n_total = s // kc
    nch = (nkb_ref[pl.program_id(0)] * Q_BLOCK + kc - 1) // kc
    f32, i32 = jnp.float32, jnp.int32
    neg_inf, pos_inf = f32(-jnp.inf), f32(jnp.inf)

    def as3(a):
        return a.reshape(grp, SUBLANES, qb)

    def rows8(v):
        return jnp.broadcast_to(v, (SUBLANES, qb))

    qi = qi_ref[...]
    q_pairs = [jnp.concatenate([qi[:, (2 * p) * IDX_DIM:(2 * p + 1) * IDX_DIM],
                                qi[:, (2 * p + 1) * IDX_DIM:(2 * p + 2) * IDX_DIM]], axis=0)
               for p in range(IDX_HEADS // 2)]
    w8 = [rows8(wi_ref[h:h + 1, :]) for h in range(IDX_HEADS)]
    cq8 = rows8(cq_ref[...])

    def score_chunk(c, carry):
        rmax, rmin = carry
        k0 = pl.multiple_of(c * kc, kc)
        kic = ki_ref[pl.ds(k0, kc), :]
        acc = jnp.zeros((grp, SUBLANES, qb), f32)
        for p in range(IDX_HEADS // 2):
            sp = lax.dot_general(kic, q_pairs[p], _NT, preferred_element_type=f32)
            acc = acc + w8[2 * p][None] * jnp.maximum(as3(sp[:, :qb]), 0.0)
            acc = acc + w8[2 * p + 1][None] * jnp.maximum(as3(sp[:, qb:]), 0.0)
        adm = as3(ck_ref[pl.ds(k0, kc), :]) <= cq8[None]
        val = jnp.where(adm, acc, neg_inf)
        sc_ref[pl.ds(k0, kc), :] = val.reshape(kc, qb)
        rmax = jnp.maximum(rmax, jnp.max(val, axis=0))
        rmin = jnp.minimum(rmin, jnp.min(jnp.where(adm, acc, pos_inf), axis=0))
        return rmax, rmin

    rmax8, rmin8 = lax.fori_loop(0, nch, score_chunk,
                                 (jnp.full((SUBLANES, qb), neg_inf, f32), jnp.full((SUBLANES, qb), pos_inf, f32)))
    row_max = jnp.max(rmax8, axis=0, keepdims=True)
    row_min = jnp.min(rmin8, axis=0, keepdims=True)

    def count_rows(pred):
        def body(c, acc):
            k0 = pl.multiple_of(c * kc, kc)
            m = pred(as3(sc_ref[pl.ds(k0, kc), :]), k0)
            return acc + jnp.sum(jnp.where(m, 1, 0), axis=0)
        acc = lax.fori_loop(0, nch, body, jnp.zeros((SUBLANES, qb), i32))
        return jnp.sum(acc, axis=0, keepdims=True)

    def count_ge(cand):
        c8 = rows8(cand)[None]
        return count_rows(lambda x3, k0: x3 >= c8)

    lo0 = _float_to_key(row_min)
    hi0 = _float_to_key(row_max) + 1
    n_adm = count_ge(row_min)
    k_eff = jnp.minimum(n_adm, k_sel)
    n_f = n_adm.astype(f32)
    z_target = _probit_upper((k_eff.astype(f32) + 0.5) / (n_f + 1.0))

    def settled(lo, hi, clo):
        return jnp.logical_or(clo == k_eff, _avg_floor(lo, hi) == lo)

    def cond(st):
        return st[5] > 0

    def step(st):
        lo, hi, clo, chi, done, _, it = st
        lo_f, hi_f = _key_to_float(lo), _key_to_float(hi)
        z_lo = _probit_upper((clo.astype(f32) + 0.5) / (n_f + 1.0))
        z_hi = _probit_upper((chi.astype(f32) + 0.5) / (n_f + 1.0))
        frac = jnp.clip((z_target - z_lo) / (z_hi - z_lo), 0.02, 0.98)
        guess = lo_f + (hi_f - lo_f) * frac
        guess_ok = jnp.logical_and(guess == guess, jnp.abs(guess) < pos_inf)
        mid = jnp.maximum(_avg_floor(lo, hi), lo + 1)
        cand = jnp.clip(_float_to_key(jnp.where(guess_ok, guess, lo_f)), lo + 1, hi - 1)
        cand = jnp.where(jnp.logical_or(it % 3 == 2, jnp.logical_not(guess_ok)), mid, cand)
        cnt = count_ge(_key_to_float(cand))
        ge = cnt >= k_eff
        up = jnp.logical_and(done == 0, ge)
        dn = jnp.logical_and(done == 0, jnp.logical_not(ge))
        lo = jnp.where(up, cand, lo)
        clo = jnp.where(up, cnt, clo)
        hi = jnp.where(dn, cand, hi)
        chi = jnp.where(dn, cnt, chi)
        done = jnp.where(settled(lo, hi, clo), 1, done)
        return lo, hi, clo, chi, done, jnp.sum(1 - done), it + 1

    done0 = jnp.where(settled(lo0, hi0, n_adm), 1, 0)
    st0 = (lo0, hi0, n_adm, jnp.zeros((1, qb), i32), done0, jnp.sum(1 - done0), i32(0))
    lo, hi, clo, chi, _, _, _ = lax.while_loop(cond, step, st0)
    thr = _key_to_float(lo)
    thr8 = rows8(thr)[None]

    tie = clo > k_eff
    need = k_eff - chi
    n_tie = jnp.sum(jnp.where(tie, 1, 0))
    key_iota = (lax.broadcasted_iota(i32, (grp, SUBLANES, qb), 0) * SUBLANES
                + lax.broadcasted_iota(i32, (grp, SUBLANES, qb), 1))

    def store_bias(k0, sel):
        bias_ref[0, pl.ds(k0, kc), :] = jnp.where(sel, 0.0, neg_inf).reshape(kc, qb).astype(jnp.bfloat16)

    @pl.when(n_tie == 0)
    def _():
        def emit_chunk(c, carry):
            k0 = pl.multiple_of(c * kc, kc)
            store_bias(k0, as3(sc_ref[pl.ds(k0, kc), :]) >= thr8)
            return carry
        lax.fori_loop(0, nch, emit_chunk, 0)

    @pl.when(n_tie > 0)
    def _():
        def cnt_upto(m):
            m8 = rows8(m)[None]
            return count_rows(lambda x3, k0: jnp.logical_and(x3 == thr8, key_iota + k0 <= m8))

        def body(_, jj):
            jl, jh = jj
            m = (jl + jh) >> 1
            ok = cnt_upto(m) >= need
            return jnp.where(ok, jl, m), jnp.where(ok, m, jh)

        n_it = max(1, (s - 1).bit_length()) + 1
        _, jh = lax.fori_loop(0, n_it, body, (jnp.full((1, qb), -1, i32), jnp.full((1, qb), s - 1, i32)))
        jlim8 = rows8(jnp.where(tie, jh, s))[None]

        def emit_chunk(c, carry):
            k0 = pl.multiple_of(c * kc, kc)
            x3 = as3(sc_ref[pl.ds(k0, kc), :])
            store_bias(k0, jnp.logical_or(x3 > thr8, jnp.logical_and(x3 == thr8, key_iota + k0 <= jlim8)))
            return carry
        lax.fori_loop(0, nch, emit_chunk, 0)

    def fill_chunk(c, carry):
        k0 = pl.multiple_of(c * kc, kc)
        bias_ref[0, pl.ds(k0, kc), :] = jnp.full((kc, qb), neg_inf, jnp.bfloat16)
        return carry

    lax.fori_loop(nch, n_total, fill_chunk, 0)


def _indexer(nkb, qi, wi_t, cq_row, ki, ck_lanes, k_sel):
    s = qi.shape[0]
    qb = min(Q_BLOCK, s)
    grid_spec = pltpu.PrefetchScalarGridSpec(
        num_scalar_prefetch=1,
        grid=(s // qb,),
        in_specs=[pl.BlockSpec((qb, D_ATTN), lambda i, n: (i, 0)),
                  pl.BlockSpec((IDX_HEADS, qb), lambda i, n: (0, i)),
                  pl.BlockSpec((1, qb), lambda i, n: (0, i)),
                  pl.BlockSpec((s, IDX_DIM), lambda i, n: (0, 0)),
                  pl.BlockSpec((s, qb), lambda i, n: (0, 0))],
        out_specs=pl.BlockSpec((1, s, qb), lambda i, n: (i, 0, 0)),
        scratch_shapes=[pltpu.VMEM((s, qb), jnp.float32)],
    )
    return pl.pallas_call(
        functools.partial(_indexer_kernel, k_sel=k_sel),
        grid_spec=grid_spec,
        out_shape=jax.ShapeDtypeStruct((s // qb, s, qb), jnp.bfloat16),
        compiler_params=_cparams(("arbitrary",)),
        name="indexer",
    )(nkb, qi, wi_t, cq_row, ki, ck_lanes)


def _attention_kernel(nkt_ref, q_ref, k_ref, vt_ref, b_ref, o_ref, qm_ref, m_ref, l_ref, acc_ref):
    i, j = pl.program_id(0), pl.program_id(1)
    tq = q_ref.shape[0]
    n_pairs = D_ATTN // LANES

    @pl.when(j == 0)
    def _():
        q = q_ref[...]
        lane = lax.broadcasted_iota(jnp.int32, (tq, LANES), 1)
        zero = jnp.zeros((tq, LANES), q.dtype)
        for p in range(n_pairs):
            qp = q[:, p * LANES:(p + 1) * LANES]
            qm_ref[2 * p] = jnp.where(lane < HEAD_DIM, qp, zero)
            qm_ref[2 * p + 1] = jnp.where(lane < HEAD_DIM, zero, qp)
        m_ref[...] = jnp.full(m_ref.shape, -1e30, jnp.float32)
        l_ref[...] = jnp.zeros(l_ref.shape, jnp.float32)
        acc_ref[...] = jnp.zeros(acc_ref.shape, jnp.float32)

    @pl.when(j < nkt_ref[i])
    def _():
        bias = jnp.concatenate([b_ref[r] for r in range(b_ref.shape[0])], axis=1).astype(jnp.float32)
        scores = []
        for h in range(N_HEADS):
            kp = k_ref[:, (h // 2) * LANES:(h // 2 + 1) * LANES]
            scores.append(lax.dot_general(kp, qm_ref[h], _NT, preferred_element_type=jnp.float32) + bias)
        for h in range(N_HEADS):
            p = h // 2
            vtp = vt_ref[p * LANES:(p + 1) * LANES, :]
            st = scores[h]
            m_prev = m_ref[h]
            m_new = jnp.maximum(m_prev, jnp.max(st, axis=0, keepdims=True))
            alpha = jnp.exp(m_prev - m_new)
            pt = jnp.exp(st - m_new)
            l_ref[h] = alpha * l_ref[h] + jnp.sum(pt, axis=0, keepdims=True)
            acc_ref[h] = alpha * acc_ref[h] + jnp.dot(vtp, pt.astype(jnp.bfloat16),
                                                      preferred_element_type=jnp.float32)
            m_ref[h] = m_new

    @pl.when(j == nkt_ref[i] - 1)
    def _():
        row = lax.broadcasted_iota(jnp.int32, (LANES, tq), 0)
        for p in range(n_pairs):
            o0 = acc_ref[2 * p] / l_ref[2 * p]
            o1 = acc_ref[2 * p + 1] / l_ref[2 * p + 1]
            o_ref[:, p * LANES:(p + 1) * LANES] = jnp.where(row < HEAD_DIM, o0, o1).T.astype(o_ref.dtype)


def _attention(nkt, q, k, v_t, bias3):
    s = q.shape[0]
    tq, tk = min(ATT_TQ, s), min(ATT_TK, s)
    qb = bias3.shape[2]
    last = lambda i, j, n: jnp.minimum(j, n[i] - 1)
    grid_spec = pltpu.PrefetchScalarGridSpec(
        num_scalar_prefetch=1,
        grid=(s // tq, s // tk),
        in_specs=[pl.BlockSpec((tq, D_ATTN), lambda i, j, n: (i, 0)),
                  pl.BlockSpec((tk, D_ATTN), lambda i, j, n: (last(i, j, n), 0)),
                  pl.BlockSpec((D_ATTN, tk), lambda i, j, n: (0, last(i, j, n))),
                  pl.BlockSpec((tq // qb, tk, qb), lambda i, j, n: (i, last(i, j, n), 0))],
        out_specs=pl.BlockSpec((tq, D_ATTN), lambda i, j, n: (i, 0)),
        scratch_shapes=[pltpu.VMEM((N_HEADS, tq, LANES), jnp.bfloat16),
                        pltpu.VMEM((N_HEADS, 1, tq), jnp.float32),
                        pltpu.VMEM((N_HEADS, 1, tq), jnp.float32),
                        pltpu.VMEM((N_HEADS, LANES, tq), jnp.float32)],
    )
    return pl.pallas_call(
        _attention_kernel,
        grid_spec=grid_spec,
        out_shape=jax.ShapeDtypeStruct((s, D_ATTN), jnp.bfloat16),
        compiler_params=_cparams(("arbitrary", "arbitrary")),
        name="attention",
    )(nkt, q, k, v_t, bias3)


def _outproj_kernel(x_ref, u_ref, a_ref, wu_ref, wa_ref, g_ref, wr_ref, br_ref,
                    h_ref, xn_ref, eidx_ref, gate_ref, rank_ref, cnt_ref, carry_ref):
    ts = x_ref.shape[0]

    @pl.when(pl.program_id(0) == 0)
    def _():
        carry_ref[...] = jnp.zeros(carry_ref.shape, jnp.float32)

    h = (x_ref[...]
         + jnp.dot(u_ref[...], wu_ref[...], preferred_element_type=jnp.float32)
         + jnp.dot(a_ref[...], wa_ref[...], preferred_element_type=jnp.float32))
    h_ref[...] = h
    ms = jnp.mean(h * h, axis=-1, keepdims=True)
    xn = h * lax.rsqrt(ms + NORM_EPS) * g_ref[...]
    xn_ref[...] = xn
    logits = jnp.dot(xn, wr_ref[...], preferred_element_type=jnp.float32,
                     precision=lax.Precision.HIGHEST) + br_ref[...]

    lane = lax.broadcasted_iota(jnp.int32, (ts, LANES), 1)
    work = logits
    vals, idxs = [], []
    multi = jnp.zeros((ts, LANES), jnp.float32)
    for _ in range(TOP_K_EXPERTS):
        mx = jnp.max(work, axis=-1, keepdims=True)
        ix = jnp.min(jnp.where(work == mx, lane, LANES), axis=-1, keepdims=True)
        hit = lane == ix
        multi = jnp.where(hit, 1.0, multi)
        work = jnp.where(hit, -jnp.inf, work)
        vals.append(mx)
        idxs.append(ix)
    ex = [jnp.exp(v - vals[0]) for v in vals]
    den = ex[0] + ex[1] + ex[2] + ex[3]

    r = lax.broadcasted_iota(jnp.int32, (ts, ts), 0)
    c = lax.broadcasted_iota(jnp.int32, (ts, ts), 1)
    below = jnp.where(c < r, 1.0, 0.0).astype(jnp.bfloat16)
    prior = jnp.dot(below, multi.astype(jnp.bfloat16), preferred_element_type=jnp.float32) + carry_ref[...]
    eidx = jnp.zeros((ts, LANES), jnp.int32)
    gate = jnp.zeros((ts, LANES), jnp.float32)
    rank = jnp.zeros((ts, LANES), jnp.int32)
    for kk in range(TOP_K_EXPERTS):
        rk = jnp.sum(jnp.where(lane == idxs[kk], prior, 0.0), axis=-1, keepdims=True)
        eidx = jnp.where(lane == kk, idxs[kk], eidx)
        gate = jnp.where(lane == kk, ex[kk] / den, gate)
        rank = jnp.where(lane == kk, rk.astype(jnp.int32), rank)
    eidx_ref[...] = eidx
    gate_ref[...] = gate
    rank_ref[...] = rank
    carry_ref[...] = carry_ref[...] + jnp.sum(multi, axis=0, keepdims=True)
    cnt_ref[...] = carry_ref[...]


def _outproj(x, u, a, w_u, w_a, g, w_r, b_r):
    s = x.shape[0]
    ts = min(ROW_TILE, s)
    row = lambda w: pl.BlockSpec((ts, w), lambda i: (i, 0))
    full = lambda arr: pl.BlockSpec(arr.shape, lambda i: (0,) * arr.ndim)
    return pl.pallas_call(
        _outproj_kernel,
        grid=(s // ts,),
        in_specs=[row(D_MODEL), row(C_CONV), row(D_ATTN), full(w_u), full(w_a), full(g), full(w_r), full(b_r)],
        out_specs=[row(D_MODEL), row(D_MODEL), row(LANES), row(LANES), row(LANES),
                   pl.BlockSpec((1, LANES), lambda i: (0, 0))],
        out_shape=[jax.ShapeDtypeStruct((s, D_MODEL), jnp.float32),
                   jax.ShapeDtypeStruct((s, D_MODEL), jnp.float32),
                   jax.ShapeDtypeStruct((s, LANES), jnp.int32),
                   jax.ShapeDtypeStruct((s, LANES), jnp.float32),
                   jax.ShapeDtypeStruct((s, LANES), jnp.int32),
                   jax.ShapeDtypeStruct((1, LANES), jnp.float32)],
        scratch_shapes=[pltpu.VMEM((1, LANES), jnp.float32)],
        compiler_params=_cparams(("arbitrary",)),
        name="outproj_router",
    )(x, u, a, w_u, w_a, g, w_r, b_r)


def _row_copy(src_ref, dst_ref, sem, src_row, dst_row):
    return pltpu.make_async_copy(src_ref.at[pl.ds(src_row, 1), :], dst_ref.at[pl.ds(dst_row, 1), :], sem)


def _dispatch_kernel(dest_ref, xn_ref, xs_in_ref, xs_ref, sem):
    del xs_in_ref
    tt = xn_ref.shape[0]

    def start(r, carry):
        for kk in range(TOP_K_EXPERTS):
            _row_copy(xn_ref, xs_ref, sem, r, dest_ref[0, 0, r * TOP_K_EXPERTS + kk]).start()
        return carry

    def wait(r, carry):
        for kk in range(TOP_K_EXPERTS):
            _row_copy(xn_ref, xs_ref, sem, r, dest_ref[0, 0, r * TOP_K_EXPERTS + kk]).wait()
        return carry

    lax.fori_loop(0, tt, start, 0)
    lax.fori_loop(0, tt, wait, 0)


def _dispatch(dest3, xn, xs_init):
    s = xn.shape[0]
    tt = min(ROUTE_TILE, s)
    return pl.pallas_call(
        _dispatch_kernel,
        grid=(s // tt,),
        in_specs=[pl.BlockSpec((1, 1, tt * TOP_K_EXPERTS), lambda i: (i, 0, 0), memory_space=pltpu.SMEM),
                  pl.BlockSpec((tt, D_MODEL), lambda i: (i, 0)),
                  pl.BlockSpec(memory_space=pl.ANY)],
        out_specs=pl.BlockSpec(memory_space=pl.ANY),
        out_shape=jax.ShapeDtypeStruct(xs_init.shape, xs_init.dtype),
        scratch_shapes=[pltpu.SemaphoreType.DMA(())],
        input_output_aliases={2: 0},
        compiler_params=_cparams(("arbitrary",)),
        name="dispatch",
    )(dest3, xn, xs_init)


def _experts_kernel(be_ref, nb_ref, xs_ref, wgu_ref, bgu_ref, wdn_ref, bdn_ref, ys_ref, wgu_bf, wdn_bf):
    b = pl.program_id(0)
    prev = be_ref[jnp.maximum(b - 1, 0)]
    fresh = jnp.logical_or(b == 0, be_ref[b] != prev)

    @pl.when(jnp.logical_and(b < nb_ref[0], fresh))
    def _():
        wgu_bf[...] = wgu_ref[0].astype(jnp.bfloat16)
        wdn_bf[...] = wdn_ref[0].astype(jnp.bfloat16)

    @pl.when(b < nb_ref[0])
    def _():
        xb = xs_ref[...].astype(jnp.bfloat16)
        gu = jnp.dot(xb, wgu_bf[...], preferred_element_type=jnp.float32) + bgu_ref[0]
        g = jnp.minimum(gu[:, :D_EXPERT], SWIGLU_LIMIT)
        u = jnp.clip(gu[:, D_EXPERT:], -SWIGLU_LIMIT, SWIGLU_LIMIT)
        hdn = g * jax.nn.sigmoid(SWIGLU_ALPHA * g) * (u + 1.0)
        ys_ref[...] = jnp.dot(hdn.astype(jnp.bfloat16), wdn_bf[...],
                              preferred_element_type=jnp.float32) + bdn_ref[0]

    @pl.when(b >= nb_ref[0])
    def _():
        ys_ref[...] = jnp.zeros(ys_ref.shape, ys_ref.dtype)


def _experts(block_expert, n_used, xs, w_gu, b_gu, w_dn, b_dn):
    n_rows = xs.shape[0]
    blk = EXPERT_BLOCK
    row_map = lambda b, be, nb: (jnp.minimum(b, nb[0] - 1), 0)
    exp_map = lambda b, be, nb: (be[b], 0, 0)
    grid_spec = pltpu.PrefetchScalarGridSpec(
        num_scalar_prefetch=2,
        grid=(n_rows // blk,),
        in_specs=[pl.BlockSpec((blk, D_MODEL), row_map),
                  pl.BlockSpec((1, D_MODEL, 2 * D_EXPERT), exp_map),
                  pl.BlockSpec((1, 1, 2 * D_EXPERT), exp_map),
                  pl.BlockSpec((1, D_EXPERT, D_MODEL), exp_map),
                  pl.BlockSpec((1, 1, D_MODEL), exp_map)],
        out_specs=pl.BlockSpec((blk, D_MODEL), lambda b, be, nb: (b, 0)),
        scratch_shapes=[pltpu.VMEM((D_MODEL, 2 * D_EXPERT), jnp.bfloat16),
                        pltpu.VMEM((D_EXPERT, D_MODEL), jnp.bfloat16)],
    )
    return pl.pallas_call(
        _experts_kernel,
        grid_spec=grid_spec,
        out_shape=jax.ShapeDtypeStruct((n_rows, D_MODEL), jnp.float32),
        compiler_params=_cparams(("arbitrary",)),
        name="experts",
    )(block_expert, n_used, xs, w_gu, b_gu, w_dn, b_dn)


def _combine_kernel(dest_ref, h_ref, gate_ref, g_ref, ys_ref, o_ref, buf_ref, sem):
    tt = h_ref.shape[0]

    def start(r, carry):
        for kk in range(TOP_K_EXPERTS):
            _row_copy(ys_ref, buf_ref.at[kk], sem, dest_ref[0, 0, r * TOP_K_EXPERTS + kk], r).start()
        return carry

    def wait(r, carry):
        for kk in range(TOP_K_EXPERTS):
            _row_copy(ys_ref, buf_ref.at[kk], sem, dest_ref[0, 0, r * TOP_K_EXPERTS + kk], r).wait()
        return carry

    lax.fori_loop(0, tt, start, 0)
    lax.fori_loop(0, tt, wait, 0)
    gate = gate_ref[...]
    h = h_ref[...]
    for kk in range(TOP_K_EXPERTS):
        h = h + gate[:, kk:kk + 1] * buf_ref[kk]
    ms = jnp.mean(h * h, axis=-1, keepdims=True)
    o_ref[...] = h * lax.rsqrt(ms + NORM_EPS) * g_ref[...]


def _combine(dest3, h, gate, g_final, ys):
    s = h.shape[0]
    tt = min(ROUTE_TILE, s)
    return pl.pallas_call(
        _combine_kernel,
        grid=(s // tt,),
        in_specs=[pl.BlockSpec((1, 1, tt * TOP_K_EXPERTS), lambda i: (i, 0, 0), memory_space=pltpu.SMEM),
                  pl.BlockSpec((tt, D_MODEL), lambda i: (i, 0)),
                  pl.BlockSpec((tt, LANES), lambda i: (i, 0)),
                  pl.BlockSpec((1, D_MODEL), lambda i: (0, 0)),
                  pl.BlockSpec(memory_space=pl.ANY)],
        out_specs=pl.BlockSpec((tt, D_MODEL), lambda i: (i, 0)),
        out_shape=jax.ShapeDtypeStruct((s, D_MODEL), jnp.float32),
        scratch_shapes=[pltpu.VMEM((TOP_K_EXPERTS, tt, D_MODEL), jnp.float32),
                        pltpu.SemaphoreType.DMA(())],
        compiler_params=_cparams(("arbitrary",)),
        name="combine",
    )(dest3, h, gate, g_final, ys)


def _rope_tables(pos):
    half = HEAD_DIM // 2
    inv = ROPE_THETA ** (-jnp.arange(half, dtype=jnp.float32) / half)
    ang = pos.astype(jnp.float32)[:, None] * inv
    cos, sin = jnp.cos(ang), jnp.sin(ang)
    zero = jnp.zeros_like(sin)
    reps = LANES // HEAD_DIM
    cos_t = jnp.tile(jnp.concatenate([cos, cos], axis=-1), (1, reps))
    sin_lo = jnp.tile(jnp.concatenate([-sin, zero], axis=-1), (1, reps))
    sin_hi = jnp.tile(jnp.concatenate([zero, sin], axis=-1), (1, reps))
    return cos_t, sin_lo, sin_hi


def _block_bounds(chunk, q_rows, k_rows):
    cq_max = jnp.max(chunk.reshape(-1, q_rows), axis=1)
    ck_min = jnp.min(chunk.reshape(-1, k_rows), axis=1)
    need = ck_min[None, :] <= cq_max[:, None]
    last = jnp.max(jnp.where(need, jnp.arange(ck_min.shape[0], dtype=jnp.int32)[None, :] + 1, 1), axis=1)
    return last.astype(jnp.int32)


def _layer(h, pos, norm_mix_g, w_in, idx_k_norm_g, idx_k_norm_b, conv_w, conv_b, conv_norm_g, conv_norm_b,
           w_out, norm_ffn_g, w_router, b_router, w_gate_up, b_gate_up, w_down, b_down, out_gain):
    s = h.shape[0]
    f32, bf16 = jnp.float32, jnp.bfloat16
    n_main = 2 * C_CONV + 3 * D_ATTN + IDX_HEADS * IDX_DIM
    w_main = w_in[:, :n_main].astype(bf16)
    w_tail = jnp.pad(w_in[:, n_main:], ((0, 0), (0, LANES - (IDX_DIM + IDX_HEADS)))).astype(bf16)
    cos_t, sin_lo, sin_hi = _rope_tables(pos)
    cw = jnp.pad(conv_w, ((0, CONV_HALO - CONV_WIDTH), (0, 0)))
    kng = jnp.pad(idx_k_norm_g, (0, LANES - IDX_DIM))[None, :]
    knb = jnp.pad(idx_k_norm_b, (0, LANES - IDX_DIM))[None, :]
    u, q, k, v, qi, tail = _inproj(h, norm_mix_g[None, :], w_main, w_tail, cos_t, sin_lo, sin_hi, cw,
                                   conv_b[None, :], conv_norm_g[None, :], conv_norm_b[None, :], kng, knb)

    chunk = pos // CHUNK
    k_sel = min(TOPK_KEYS_MAX, s // 4)
    qb = min(Q_BLOCK, s)
    ki = tail[:, :IDX_DIM].astype(bf16)
    wi_t = tail[:, IDX_DIM:IDX_DIM + IDX_HEADS].T
    bias3 = _indexer(_block_bounds(chunk, qb, qb), qi, wi_t, chunk[None, :], ki,
                     jnp.broadcast_to(chunk[:, None], (s, qb)), k_sel)
    a = _attention(_block_bounds(chunk, min(ATT_TQ, s), min(ATT_TK, s)), q, k, v.T, bias3)

    w_r = jnp.pad(w_router, ((0, 0), (0, LANES - N_EXPERTS)))
    b_r = jnp.pad(b_router, (0, LANES - N_EXPERTS), constant_values=-jnp.inf)[None, :]
    h1, xn, eidx, gate, rank, counts = _outproj(h, u, a, w_out[:C_CONV].astype(bf16), w_out[C_CONV:].astype(bf16),
                                                norm_ffn_g[None, :], w_r, b_r)

    blk = EXPERT_BLOCK
    cnt = counts[0, :N_EXPERTS].astype(jnp.int32)
    padded = (cnt + blk - 1) // blk * blk
    end = jnp.cumsum(padded)
    start = end - padded
    n_blocks = s * TOP_K_EXPERTS // blk + N_EXPERTS
    dest = start[eidx[:, :TOP_K_EXPERTS]] + rank[:, :TOP_K_EXPERTS]
    tt = min(ROUTE_TILE, s)
    dest3 = dest.reshape(s // tt, 1, tt * TOP_K_EXPERTS)
    block_expert = jnp.minimum(
        jnp.searchsorted(end, jnp.arange(n_blocks, dtype=jnp.int32) * blk, side='right'),
        N_EXPERTS - 1).astype(jnp.int32)
    n_used = (end[-1:] // blk).astype(jnp.int32)

    xs = _dispatch(dest3, xn, jnp.zeros((n_blocks * blk, D_MODEL), f32))
    ys = _experts(block_expert, n_used, xs, w_gate_up, b_gate_up[:, None, :], w_down, b_down[:, None, :])
    return _combine(dest3, h1, gate, out_gain[None, :], ys)


def kernel(x, positions, norm_mix_g, w_in, idx_k_norm_g, idx_k_norm_b, conv_w, conv_b, conv_norm_g, conv_norm_b,
           w_out, norm_ffn_g, w_router, b_router, w_gate_up, b_gate_up, w_down, b_down, norm_final_g):
    assert x.shape[0] == 1 and norm_mix_g.shape[0] == 1, "single sequence, single layer"
    y = _layer(x[0], positions[0], norm_mix_g[0], w_in[0], idx_k_norm_g[0], idx_k_norm_b[0], conv_w[0],
               conv_b[0], conv_norm_g[0], conv_norm_b[0], w_out[0], norm_ffn_g[0], w_router[0], b_router[0],
               w_gate_up[0], b_gate_up[0], w_down[0], b_down[0], norm_final_g)
    return y[None]
```

```python
import functools
import math

import jax
import jax.numpy as jnp
from jax import lax
from jax.experimental import pallas as pl
from jax.experimental.pallas import tpu as pltpu

D_MODEL = 1024
CHUNK = 64
C_CONV = 512
CONV_WIDTH = 31
N_HEADS = 8
HEAD_DIM = 64
D_ATTN = N_HEADS * HEAD_DIM
IDX_HEADS = 8
IDX_DIM = 64
TOPK_KEYS_MAX = 256
ROPE_THETA = 10000.0
N_EXPERTS = 32
TOP_K_EXPERTS = 4
D_EXPERT = 1024
SWIGLU_LIMIT = 7.0
SWIGLU_ALPHA = 1.702
NORM_EPS = 1e-5

LANES = 128
SUBLANES = 8
CONV_HALO = 32
ROW_TILE = 256
Q_BLOCK = 128
KEY_CHUNK = 512
ATT_TQ = 256
ATT_TK = 512
ATT_V_ROWS = LANES + 16
Q_SCALE = HEAD_DIM ** -0.5 * math.log2(math.e)
EXPERT_BLOCK = 256
ROUTE_TILE = 256
VMEM_LIMIT = 56 * 1024 * 1024

_NT = (((1,), (1,)), ((), ()))


def _cparams(sem):
    return pltpu.CompilerParams(dimension_semantics=sem, vmem_limit_bytes=VMEM_LIMIT)


def _rope128(xp, cos_t, sin_lo, sin_hi):
    return xp * cos_t + pltpu.roll(xp, 96, 1) * sin_lo + pltpu.roll(xp, 32, 1) * sin_hi


def _inproj_kernel(x_ref, g_ref, wm_ref, wt_ref, cos_ref, slo_ref, shi_ref, cw_ref, cb_ref,
                   cng_ref, cnb_ref, kng_ref, knb_ref,
                   u_ref, q_ref, k_ref, v_ref, qi_ref, tail_ref, ubuf_ref):
    ts = x_ref.shape[0]
    x = x_ref[...]
    ms = jnp.mean(x * x, axis=-1, keepdims=True)
    hn = (x * lax.rsqrt(ms + NORM_EPS) * g_ref[...]).astype(jnp.bfloat16)
    proj = jnp.dot(hn, wm_ref[...], preferred_element_type=jnp.float32)
    tail = jnp.dot(hn, wt_ref[...], preferred_element_type=jnp.float32)

    u = proj[:, 0:C_CONV] * jax.nn.sigmoid(proj[:, C_CONV:2 * C_CONV])

    @pl.when(pl.program_id(0) == 0)
    def _():
        ubuf_ref[0:CONV_HALO, :] = jnp.zeros((CONV_HALO, C_CONV), jnp.float32)

    ubuf_ref[CONV_HALO:CONV_HALO + ts, :] = u
    acc = jnp.zeros((ts, C_CONV), jnp.float32) + cb_ref[...]
    for kk in range(CONV_WIDTH):
        off = CONV_HALO - (CONV_WIDTH - 1) + kk
        acc = acc + ubuf_ref[off:off + ts, :] * cw_ref[kk:kk + 1, :]
    ubuf_ref[0:CONV_HALO, :] = ubuf_ref[ts:ts + CONV_HALO, :]
    mu = jnp.mean(acc, axis=-1, keepdims=True)
    d = acc - mu
    var = jnp.mean(d * d, axis=-1, keepdims=True)
    yn = d * lax.rsqrt(var + NORM_EPS) * cng_ref[...] + cnb_ref[...]
    u_ref[...] = (yn * jax.nn.sigmoid(yn)).astype(jnp.bfloat16)

    cos_t, sin_lo, sin_hi = cos_ref[...], slo_ref[...], shi_ref[...]
    base = 2 * C_CONV
    for p in range(D_ATTN // LANES):
        lo = p * LANES
        qp = proj[:, base + lo:base + lo + LANES]
        q_ref[:, lo:lo + LANES] = (_rope128(qp, cos_t, sin_lo, sin_hi) * Q_SCALE).astype(jnp.bfloat16)
        kp = proj[:, base + D_ATTN + lo:base + D_ATTN + lo + LANES]
        k_ref[:, lo:lo + LANES] = _rope128(kp, cos_t, sin_lo, sin_hi).astype(jnp.bfloat16)
        qip = proj[:, base + 3 * D_ATTN + lo:base + 3 * D_ATTN + lo + LANES]
        qi_ref[:, lo:lo + LANES] = _rope128(qip, cos_t, sin_lo, sin_hi).astype(jnp.bfloat16)
    v_ref[...] = proj[:, base + 2 * D_ATTN:base + 3 * D_ATTN].astype(jnp.bfloat16)

    lane = lax.broadcasted_iota(jnp.int32, tail.shape, 1)
    is_k = lane < IDX_DIM
    kmu = jnp.sum(jnp.where(is_k, tail, 0.0), axis=-1, keepdims=True) * (1.0 / IDX_DIM)
    kd = jnp.where(is_k, tail - kmu, 0.0)
    kvar = jnp.sum(kd * kd, axis=-1, keepdims=True) * (1.0 / IDX_DIM)
    kn = kd * lax.rsqrt(kvar + NORM_EPS) * kng_ref[...] + knb_ref[...]
    kr = _rope128(kn, cos_t, sin_lo, sin_hi)
    wi = tail * (IDX_HEADS ** -0.5 * IDX_DIM ** -0.5)
    tail_ref[...] = jnp.where(is_k, kr, jnp.where(lane < IDX_DIM + IDX_HEADS, wi, 0.0))


def _inproj(x, g, w_main, w_tail, cos_t, sin_lo, sin_hi, cw, cb, cng, cnb, kng, knb):
    s = x.shape[0]
    ts = min(ROW_TILE, s)
    row = lambda w: pl.BlockSpec((ts, w), lambda i: (i, 0))
    full = lambda a: pl.BlockSpec(a.shape, lambda i: (0,) * a.ndim)
    return pl.pallas_call(
        _inproj_kernel,
        grid=(s // ts,),
        in_specs=[row(D_MODEL), full(g), full(w_main), full(w_tail), row(LANES), row(LANES), row(LANES),
                  full(cw), full(cb), full(cng), full(cnb), full(kng), full(knb)],
        out_specs=[row(C_CONV), row(D_ATTN), row(D_ATTN), row(D_ATTN), row(D_ATTN), row(LANES)],
        out_shape=[jax.ShapeDtypeStruct((s, C_CONV), jnp.bfloat16),
                   jax.ShapeDtypeStruct((s, D_ATTN), jnp.bfloat16),
                   jax.ShapeDtypeStruct((s, D_ATTN), jnp.bfloat16),
                   jax.ShapeDtypeStruct((s, D_ATTN), jnp.bfloat16),
                   jax.ShapeDtypeStruct((s, D_ATTN), jnp.bfloat16),
                   jax.ShapeDtypeStruct((s, LANES), jnp.float32)],
        scratch_shapes=[pltpu.VMEM((ts + CONV_HALO, C_CONV), jnp.float32)],
        compiler_params=_cparams(("arbitrary",)),
        name="inproj",
    )(x, g, w_main, w_tail, cos_t, sin_lo, sin_hi, cw, cb, cng, cnb, kng, knb)


def _float_to_key(x):
    b = lax.bitcast_convert_type(x, jnp.int32)
    return jnp.where(b < 0, b ^ jnp.int32(0x7FFFFFFF), b)


def _key_to_float(k):
    b = jnp.where(k < 0, k ^ jnp.int32(0x7FFFFFFF), k)
    return lax.bitcast_convert_type(b, jnp.float32)


def _avg_floor(a, b):
    return (a >> 1) + (b >> 1) + (a & b & 1)


def _probit_upper(q):
    p = jnp.minimum(q, 1.0 - q)
    t = jnp.sqrt(-2.0 * jnp.log(p))
    z = t - (2.515517 + t * (0.802853 + t * 0.010328)) / (1.0 + t * (1.432788 + t * (0.189269 + t * 0.001308)))
    return jnp.where(q <= 0.5, z, -z)


def _indexer_kernel(nkb_ref, qi_ref, wi_ref, cq_ref, ki_ref, ck_ref, bias_ref, sc_ref, *, k_sel):
    qb = qi_ref.shape[0]
    s = ki_ref.shape[0]
    kc = min(KEY_CHUNK, s)
    grp = kc // SUBLANES
    n_total = s // kc
    nch = (nkb_ref[pl.program_id(0)] * Q_BLOCK + kc - 1) // kc
    f32, i32 = jnp.float32, jnp.int32
    neg_inf, pos_inf = f32(-jnp.inf), f32(jnp.inf)

    def as3(a):
        return a.reshape(grp, SUBLANES, qb)

    def rows8(v):
        return jnp.broadcast_to(v, (SUBLANES, qb))

    qi = qi_ref[...]
    q_pairs = [jnp.concatenate([qi[:, (2 * p) * IDX_DIM:(2 * p + 1) * IDX_DIM],
                                qi[:, (2 * p + 1) * IDX_DIM:(2 * p + 2) * IDX_DIM]], axis=0)
               for p in range(IDX_HEADS // 2)]
    w8 = [rows8(wi_ref[h:h + 1, :]) for h in range(IDX_HEADS)]
    cq8 = rows8(cq_ref[...])

    def score_chunk(c, carry):
        rmax, rmin, nadm = carry
        k0 = pl.multiple_of(c * kc, kc)
        kic = ki_ref[pl.ds(k0, kc), :]
        acc = jnp.zeros((grp, SUBLANES, qb), f32)
        for p in range(IDX_HEADS // 2):
            sp = lax.dot_general(kic, q_pairs[p], _NT, preferred_element_type=f32)
            acc = acc + w8[2 * p][None] * jnp.maximum(as3(sp[:, :qb]), 0.0)
            acc = acc + w8[2 * p + 1][None] * jnp.maximum(as3(sp[:, qb:]), 0.0)
        adm = as3(ck_ref[pl.ds(k0, kc), :]) <= cq8[None]
        val = jnp.where(adm, acc, neg_inf)
        sc_ref[pl.ds(k0, kc), :] = val.reshape(kc, qb)
        rmax = jnp.maximum(rmax, jnp.max(val, axis=0))
        rmin = jnp.minimum(rmin, jnp.min(jnp.where(adm, acc, pos_inf), axis=0))
        nadm = nadm + jnp.sum(jnp.where(adm, 1, 0), axis=0)
        return rmax, rmin, nadm

    rmax8, rmin8, nadm8 = lax.fori_loop(
        0, nch, score_chunk,
        (jnp.full((SUBLANES, qb), neg_inf, f32), jnp.full((SUBLANES, qb), pos_inf, f32),
         jnp.zeros((SUBLANES, qb), i32)))
    row_max = jnp.max(rmax8, axis=0, keepdims=True)
    row_min = jnp.min(rmin8, axis=0, keepdims=True)
    n_adm = jnp.sum(nadm8, axis=0, keepdims=True)

    def count_rows(pred):
        def body(c, acc):
            k0 = pl.multiple_of(c * kc, kc)
            m = pred(as3(sc_ref[pl.ds(k0, kc), :]), k0)
            return acc + jnp.sum(jnp.where(m, 1, 0), axis=0)
        acc = lax.fori_loop(0, nch, body, jnp.zeros((SUBLANES, qb), i32))
        return jnp.sum(acc, axis=0, keepdims=True)

    def count_ge(cand):
        c8 = rows8(cand)[None]
        return count_rows(lambda x3, k0: x3 >= c8)

    k_eff = jnp.minimum(n_adm, k_sel)
    inv_n = 1.0 / (n_adm.astype(f32) + 1.0)

    def zscore(cnt):
        return _probit_upper((cnt.astype(f32) + 0.5) * inv_n)

    z_target = zscore(k_eff)

    def zero_chunk(c, acc):
        k0 = pl.multiple_of(c * kc, kc)
        x3 = as3(sc_ref[pl.ds(k0, kc), :])
        return (acc[0] + jnp.sum(jnp.where(x3 >= 0.0, 1, 0), axis=0),
                acc[1] + jnp.sum(jnp.where(x3 > 0.0, 1, 0), axis=0))

    ge8, gt8 = lax.fori_loop(0, nch, zero_chunk,
                             (jnp.zeros((SUBLANES, qb), i32), jnp.zeros((SUBLANES, qb), i32)))
    c_ge0 = jnp.sum(ge8, axis=0, keepdims=True)
    c_gt0 = jnp.sum(gt8, axis=0, keepdims=True)
    above = c_gt0 >= k_eff
    below = c_ge0 < k_eff
    zero_key = jnp.zeros((1, qb), i32)
    lo0 = jnp.where(below, _float_to_key(row_min), zero_key)
    clo0 = jnp.where(below, n_adm, c_ge0)
    hi0 = jnp.where(above, _float_to_key(row_max) + 1, jnp.where(below, zero_key - 1, zero_key + 1))
    chi0 = jnp.where(above, 0, jnp.where(below, c_ge0, c_gt0))

    def settled(lo, hi, clo):
        return jnp.logical_or(clo == k_eff, _avg_floor(lo, hi) == lo)

    def cond(st):
        return st[7] > 0

    def step(st):
        lo, hi, clo, chi, z_lo, z_hi, done, _, phase = st
        lo_f, hi_f = _key_to_float(lo), _key_to_float(hi)
        frac = jnp.clip((z_target - z_lo) / (z_hi - z_lo), 0.02, 0.98)
        guess = lo_f + (hi_f - lo_f) * frac
        guess_ok = jnp.logical_and(guess == guess, jnp.abs(guess) < pos_inf)
        mid = jnp.maximum(_avg_floor(lo, hi), lo + 1)
        cand = jnp.clip(_float_to_key(jnp.where(guess_ok, guess, lo_f)), lo + 1, hi - 1)
        cand = jnp.where(jnp.logical_or(phase == 2, jnp.logical_not(guess_ok)), mid, cand)
        cnt = count_ge(_key_to_float(cand))
        z_c = zscore(cnt)
        ge = cnt >= k_eff
        up = jnp.logical_and(done == 0, ge)
        dn = jnp.logical_and(done == 0, jnp.logical_not(ge))
        lo = jnp.where(up, cand, lo)
        clo = jnp.where(up, cnt, clo)
        z_lo = jnp.where(up, z_c, z_lo)
        hi = jnp.where(dn, cand, hi)
        chi = jnp.where(dn, cnt, chi)
        z_hi = jnp.where(dn, z_c, z_hi)
        done = jnp.where(settled(lo, hi, clo), 1, done)
        return lo, hi, clo, chi, z_lo, z_hi, done, jnp.sum(1 - done), jnp.where(phase == 2, 0, phase + 1)

    done0 = jnp.where(settled(lo0, hi0, clo0), 1, 0)
    st0 = (lo0, hi0, clo0, chi0, zscore(clo0), zscore(chi0), done0, jnp.sum(1 - done0), i32(0))
    lo, hi, clo, chi = lax.while_loop(cond, step, st0)[:4]
    thr = _key_to_float(lo)
    thr8 = rows8(thr)[None]

    tie = clo > k_eff
    need = k_eff - chi
    n_tie = jnp.sum(jnp.where(tie, 1, 0))
    key_iota = (lax.broadcasted_iota(i32, (grp, SUBLANES, qb), 0) * SUBLANES
                + lax.broadcasted_iota(i32, (grp, SUBLANES, qb), 1))

    def store_bias(k0, sel):
        bias_ref[0, pl.ds(k0, kc), :] = jnp.where(sel, 0.0, neg_inf).reshape(kc, qb).astype(jnp.bfloat16)

    @pl.when(n_tie == 0)
    def _():
        def emit_chunk(c, carry):
            k0 = pl.multiple_of(c * kc, kc)
            store_bias(k0, as3(sc_ref[pl.ds(k0, kc), :]) >= thr8)
            return carry
        lax.fori_loop(0, nch, emit_chunk, 0)

    @pl.when(n_tie > 0)
    def _():
        def cnt_upto(m):
            m8 = rows8(m)[None]
            return count_rows(lambda x3, k0: jnp.logical_and(x3 == thr8, key_iota + k0 <= m8))

        def body(_, jj):
            jl, jh = jj
            m = (jl + jh) >> 1
            ok = cnt_upto(m) >= need
            return jnp.where(ok, jl, m), jnp.where(ok, m, jh)

        n_it = max(1, (s - 1).bit_length()) + 1
        _, jh = lax.fori_loop(0, n_it, body, (jnp.full((1, qb), -1, i32), jnp.full((1, qb), s - 1, i32)))
        jlim8 = rows8(jnp.where(tie, jh, s))[None]

        def emit_chunk(c, carry):
            k0 = pl.multiple_of(c * kc, kc)
            x3 = as3(sc_ref[pl.ds(k0, kc), :])
            store_bias(k0, jnp.logical_or(x3 > thr8, jnp.logical_and(x3 == thr8, key_iota + k0 <= jlim8)))
            return carry
        lax.fori_loop(0, nch, emit_chunk, 0)

    def fill_chunk(c, carry):
        k0 = pl.multiple_of(c * kc, kc)
        bias_ref[0, pl.ds(k0, kc), :] = jnp.full((kc, qb), neg_inf, jnp.bfloat16)
        return carry

    lax.fori_loop(nch, n_total, fill_chunk, 0)


def _indexer(nkb, qi, wi_t, cq_row, ki, ck_lanes, k_sel):
    s = qi.shape[0]
    qb = min(Q_BLOCK, s)
    grid_spec = pltpu.PrefetchScalarGridSpec(
        num_scalar_prefetch=1,
        grid=(s // qb,),
        in_specs=[pl.BlockSpec((qb, D_ATTN), lambda i, n: (i, 0)),
                  pl.BlockSpec((IDX_HEADS, qb), lambda i, n: (0, i)),
                  pl.BlockSpec((1, qb), lambda i, n: (0, i)),
                  pl.BlockSpec((s, IDX_DIM), lambda i, n: (0, 0)),
                  pl.BlockSpec((s, qb), lambda i, n: (0, 0))],
        out_specs=pl.BlockSpec((1, s, qb), lambda i, n: (i, 0, 0)),
        scratch_shapes=[pltpu.VMEM((s, qb), jnp.float32)],
    )
    return pl.pallas_call(
        functools.partial(_indexer_kernel, k_sel=k_sel),
        grid_spec=grid_spec,
        out_shape=jax.ShapeDtypeStruct((s // qb, s, qb), jnp.bfloat16),
        compiler_params=_cparams(("arbitrary",)),
        name="indexer",
    )(nkb, qi, wi_t, cq_row, ki, ck_lanes)


def _attention_kernel(nkt_ref, q_ref, k_ref, vt_ref, b_ref, o_ref, qm_ref, m_ref, acc_ref):
    i, j = pl.program_id(0), pl.program_id(1)
    tq = q_ref.shape[0]
    n_pairs = D_ATTN // LANES
    bf16 = jnp.bfloat16

    @pl.when(j == 0)
    def _():
        q = q_ref[...]
        lane = lax.broadcasted_iota(jnp.int32, (tq, LANES), 1)
        zero = jnp.zeros((tq, LANES), q.dtype)
        for p in range(n_pairs):
            qp = q[:, p * LANES:(p + 1) * LANES]
            qm_ref[2 * p] = jnp.where(lane < HEAD_DIM, qp, zero)
            qm_ref[2 * p + 1] = jnp.where(lane < HEAD_DIM, zero, qp)
        m_ref[...] = jnp.full(m_ref.shape, -1e30, jnp.float32)
        acc_ref[...] = jnp.zeros(acc_ref.shape, jnp.float32)

    @pl.when(j < nkt_ref[i])
    def _():
        bias = jnp.concatenate([b_ref[r] for r in range(b_ref.shape[0])], axis=1)
        scores = []
        for h in range(N_HEADS):
            kp = k_ref[:, (h // 2) * LANES:(h // 2 + 1) * LANES]
            st = lax.dot_general(kp, qm_ref[h], _NT, preferred_element_type=jnp.float32)
            scores.append(st.astype(bf16) + bias)
        for h in range(N_HEADS):
            p = h // 2
            vtp = vt_ref[p * ATT_V_ROWS:(p + 1) * ATT_V_ROWS, :]
            st = scores[h]
            m_prev = m_ref[h]
            m_new = jnp.maximum(m_prev, jnp.max(st, axis=0, keepdims=True).astype(jnp.float32))
            alpha = jnp.exp2(m_prev - m_new)
            pt = jnp.exp2(st - m_new.astype(bf16))
            acc_ref[h] = alpha * acc_ref[h] + jnp.dot(vtp, pt, preferred_element_type=jnp.float32)
            m_ref[h] = m_new

    @pl.when(j == nkt_ref[i] - 1)
    def _():
        row = lax.broadcasted_iota(jnp.int32, (LANES, tq), 0)
        for p in range(n_pairs):
            a0, a1 = acc_ref[2 * p], acc_ref[2 * p + 1]
            o0 = a0[:LANES] / a0[LANES:LANES + 1]
            o1 = a1[:LANES] / a1[LANES:LANES + 1]
            o_ref[:, p * LANES:(p + 1) * LANES] = jnp.where(row < HEAD_DIM, o0, o1).T.astype(o_ref.dtype)


def _attention(nkt, q, k, vt_aug, bias3):
    s = q.shape[0]
    tq, tk = min(ATT_TQ, s), min(ATT_TK, s)
    qb = bias3.shape[2]
    last = lambda i, j, n: jnp.minimum(j, n[i] - 1)
    grid_spec = pltpu.PrefetchScalarGridSpec(
        num_scalar_prefetch=1,
        grid=(s // tq, s // tk),
        in_specs=[pl.BlockSpec((tq, D_ATTN), lambda i, j, n: (i, 0)),
                  pl.BlockSpec((tk, D_ATTN), lambda i, j, n: (last(i, j, n), 0)),
                  pl.BlockSpec((vt_aug.shape[0], tk), lambda i, j, n: (0, last(i, j, n))),
                  pl.BlockSpec((tq // qb, tk, qb), lambda i, j, n: (i, last(i, j, n), 0))],
        out_specs=pl.BlockSpec((tq, D_ATTN), lambda i, j, n: (i, 0)),
        scratch_shapes=[pltpu.VMEM((N_HEADS, tq, LANES), jnp.bfloat16),
                        pltpu.VMEM((N_HEADS, 1, tq), jnp.float32),
                        pltpu.VMEM((N_HEADS, ATT_V_ROWS, tq), jnp.float32)],
    )
    return pl.pallas_call(
        _attention_kernel,
        grid_spec=grid_spec,
        out_shape=jax.ShapeDtypeStruct((s, D_ATTN), jnp.bfloat16),
        compiler_params=_cparams(("arbitrary", "arbitrary")),
        name="attention",
    )(nkt, q, k, vt_aug, bias3)


def _outproj_kernel(x_ref, u_ref, a_ref, wu_ref, wa_ref, g_ref, wr_ref, br_ref,
                    h_ref, xn_ref, eidx_ref, gate_ref, rank_ref, cnt_ref, carry_ref):
    ts = x_ref.shape[0]

    @pl.when(pl.program_id(0) == 0)
    def _():
        carry_ref[...] = jnp.zeros(carry_ref.shape, jnp.float32)

    h = (x_ref[...]
         + jnp.dot(u_ref[...], wu_ref[...], preferred_element_type=jnp.float32)
         + jnp.dot(a_ref[...], wa_ref[...], preferred_element_type=jnp.float32))
    h_ref[...] = h
    ms = jnp.mean(h * h, axis=-1, keepdims=True)
    xn = h * lax.rsqrt(ms + NORM_EPS) * g_ref[...]
    xn_ref[...] = xn
    logits = jnp.dot(xn, wr_ref[...], preferred_element_type=jnp.float32,
                     precision=lax.Precision.HIGHEST) + br_ref[...]

    lane = lax.broadcasted_iota(jnp.int32, (ts, LANES), 1)
    work = logits
    vals, idxs = [], []
    multi = jnp.zeros((ts, LANES), jnp.float32)
    for _ in range(TOP_K_EXPERTS):
        mx = jnp.max(work, axis=-1, keepdims=True)
        ix = jnp.min(jnp.where(work == mx, lane, LANES), axis=-1, keepdims=True)
        hit = lane == ix
        multi = jnp.where(hit, 1.0, multi)
        work = jnp.where(hit, -jnp.inf, work)
        vals.append(mx)
        idxs.append(ix)
    ex = [jnp.exp(v - vals[0]) for v in vals]
    den = ex[0] + ex[1] + ex[2] + ex[3]

    r = lax.broadcasted_iota(jnp.int32, (ts, ts), 0)
    c = lax.broadcasted_iota(jnp.int32, (ts, ts), 1)
    below = jnp.where(c < r, 1.0, 0.0).astype(jnp.bfloat16)
    prior = jnp.dot(below, multi.astype(jnp.bfloat16), preferred_element_type=jnp.float32) + carry_ref[...]
    eidx = jnp.zeros((ts, LANES), jnp.int32)
    gate = jnp.zeros((ts, LANES), jnp.float32)
    rank = jnp.zeros((ts, LANES), jnp.int32)
    for kk in range(TOP_K_EXPERTS):
        rk = jnp.sum(jnp.where(lane == idxs[kk], prior, 0.0), axis=-1, keepdims=True)
        eidx = jnp.where(lane == kk, idxs[kk], eidx)
        gate = jnp.where(lane == kk, ex[kk] / den, gate)
        rank = jnp.where(lane == kk, rk.astype(jnp.int32), rank)
    eidx_ref[...] = eidx
    gate_ref[...] = gate
    rank_ref[...] = rank
    carry_ref[...] = carry_ref[...] + jnp.sum(multi, axis=0, keepdims=True)
    cnt_ref[...] = carry_ref[...]


def _outproj(x, u, a, w_u, w_a, g, w_r, b_r):
    s = x.shape[0]
    ts = min(ROW_TILE, s)
    row = lambda w: pl.BlockSpec((ts, w), lambda i: (i, 0))
    full = lambda arr: pl.BlockSpec(arr.shape, lambda i: (0,) * arr.ndim)
    return pl.pallas_call(
        _outproj_kernel,
        grid=(s // ts,),
        in_specs=[row(D_MODEL), row(C_CONV), row(D_ATTN), full(w_u), full(w_a), full(g), full(w_r), full(b_r)],
        out_specs=[row(D_MODEL), row(D_MODEL), row(LANES), row(LANES), row(LANES),
                   pl.BlockSpec((1, LANES), lambda i: (0, 0))],
        out_shape=[jax.ShapeDtypeStruct((s, D_MODEL), jnp.float32),
                   jax.ShapeDtypeStruct((s, D_MODEL), jnp.float32),
                   jax.ShapeDtypeStruct((s, LANES), jnp.int32),
                   jax.ShapeDtypeStruct((s, LANES), jnp.float32),
                   jax.ShapeDtypeStruct((s, LANES), jnp.int32),
                   jax.ShapeDtypeStruct((1, LANES), jnp.float32)],
        scratch_shapes=[pltpu.VMEM((1, LANES), jnp.float32)],
        compiler_params=_cparams(("arbitrary",)),
        name="outproj_router",
    )(x, u, a, w_u, w_a, g, w_r, b_r)


def _row_copy(src_ref, dst_ref, sem, src_row, dst_row):
    return pltpu.make_async_copy(src_ref.at[pl.ds(src_row, 1), :], dst_ref.at[pl.ds(dst_row, 1), :], sem)


def _dispatch_kernel(dest_ref, xn_ref, xs_in_ref, xs_ref, sem):
    del xs_in_ref
    tt = xn_ref.shape[0]

    def start(r, carry):
        for kk in range(TOP_K_EXPERTS):
            _row_copy(xn_ref, xs_ref, sem, r, dest_ref[0, 0, r * TOP_K_EXPERTS + kk]).start(priority=kk % 2)
        return carry

    def wait(r, carry):
        for kk in range(TOP_K_EXPERTS):
            _row_copy(xn_ref, xs_ref, sem, r, dest_ref[0, 0, r * TOP_K_EXPERTS + kk]).wait()
        return carry

    lax.fori_loop(0, tt, start, 0)
    lax.fori_loop(0, tt, wait, 0)


def _dispatch(dest3, xn, xs_init):
    s = xn.shape[0]
    tt = min(ROUTE_TILE, s)
    return pl.pallas_call(
        _dispatch_kernel,
        grid=(s // tt,),
        in_specs=[pl.BlockSpec((1, 1, tt * TOP_K_EXPERTS), lambda i: (i, 0, 0), memory_space=pltpu.SMEM),
                  pl.BlockSpec((tt, D_MODEL), lambda i: (i, 0)),
                  pl.BlockSpec(memory_space=pl.ANY)],
        out_specs=pl.BlockSpec(memory_space=pl.ANY),
        out_shape=jax.ShapeDtypeStruct(xs_init.shape, xs_init.dtype),
        scratch_shapes=[pltpu.SemaphoreType.DMA(())],
        input_output_aliases={2: 0},
        compiler_params=_cparams(("arbitrary",)),
        name="dispatch",
    )(dest3, xn, xs_init)


def _experts_kernel(be_ref, nb_ref, xs_ref, wgu_ref, bgu_ref, wdn_ref, bdn_ref, ys_ref, wgu_bf, wdn_bf):
    b = pl.program_id(0)
    prev = be_ref[jnp.maximum(b - 1, 0)]
    fresh = jnp.logical_or(b == 0, be_ref[b] != prev)

    @pl.when(jnp.logical_and(b < nb_ref[0], fresh))
    def _():
        wgu_bf[...] = wgu_ref[0].astype(jnp.bfloat16)
        wdn_bf[...] = wdn_ref[0].astype(jnp.bfloat16)

    @pl.when(b < nb_ref[0])
    def _():
        xb = xs_ref[...].astype(jnp.bfloat16)
        gu = jnp.dot(xb, wgu_bf[...], preferred_element_type=jnp.float32) + bgu_ref[0]
        g = jnp.minimum(gu[:, :D_EXPERT], SWIGLU_LIMIT)
        u = jnp.clip(gu[:, D_EXPERT:], -SWIGLU_LIMIT, SWIGLU_LIMIT)
        hdn = g * jax.nn.sigmoid(SWIGLU_ALPHA * g) * (u + 1.0)
        ys_ref[...] = jnp.dot(hdn.astype(jnp.bfloat16), wdn_bf[...],
                              preferred_element_type=jnp.float32) + bdn_ref[0]

    @pl.when(b >= nb_ref[0])
    def _():
        ys_ref[...] = jnp.zeros(ys_ref.shape, ys_ref.dtype)


def _experts(block_expert, n_used, xs, w_gu, b_gu, w_dn, b_dn):
    n_rows = xs.shape[0]
    blk = EXPERT_BLOCK
    row_map = lambda b, be, nb: (jnp.maximum(jnp.minimum(b, nb[0] - 1), 0), 0)
    exp_map = lambda b, be, nb: (be[b], 0, 0)
    grid_spec = pltpu.PrefetchScalarGridSpec(
        num_scalar_prefetch=2,
        grid=(n_rows // blk,),
        in_specs=[pl.BlockSpec((blk, D_MODEL), row_map),
                  pl.BlockSpec((1, D_MODEL, 2 * D_EXPERT), exp_map),
                  pl.BlockSpec((1, 1, 2 * D_EXPERT), exp_map),
                  pl.BlockSpec((1, D_EXPERT, D_MODEL), exp_map),
                  pl.BlockSpec((1, 1, D_MODEL), exp_map)],
        out_specs=pl.BlockSpec((blk, D_MODEL), lambda b, be, nb: (b, 0)),
        scratch_shapes=[pltpu.VMEM((D_MODEL, 2 * D_EXPERT), jnp.bfloat16),
                        pltpu.VMEM((D_EXPERT, D_MODEL), jnp.bfloat16)],
    )
    return pl.pallas_call(
        _experts_kernel,
        grid_spec=grid_spec,
        out_shape=jax.ShapeDtypeStruct((n_rows, D_MODEL), jnp.float32),
        compiler_params=_cparams(("arbitrary",)),
        name="experts",
    )(block_expert, n_used, xs, w_gu, b_gu, w_dn, b_dn)


def _combine_kernel(dest_ref, h_ref, gate_ref, g_ref, ys_ref, o_ref, buf_ref, sem):
    tt = h_ref.shape[0]

    def start(r, carry):
        for kk in range(TOP_K_EXPERTS):
            _row_copy(ys_ref, buf_ref.at[kk], sem, dest_ref[0, 0, r * TOP_K_EXPERTS + kk], r).start(priority=kk % 2)
        return carry

    def wait(r, carry):
        for kk in range(TOP_K_EXPERTS):
            _row_copy(ys_ref, buf_ref.at[kk], sem, dest_ref[0, 0, r * TOP_K_EXPERTS + kk], r).wait()
        return carry

    lax.fori_loop(0, tt, start, 0)
    lax.fori_loop(0, tt, wait, 0)
    gate = gate_ref[...]
    h = h_ref[...]
    for kk in range(TOP_K_EXPERTS):
        h = h + gate[:, kk:kk + 1] * buf_ref[kk]
    ms = jnp.mean(h * h, axis=-1, keepdims=True)
    o_ref[...] = h * lax.rsqrt(ms + NORM_EPS) * g_ref[...]


def _combine(dest3, h, gate, g_final, ys):
    s = h.shape[0]
    tt = min(ROUTE_TILE, s)
    return pl.pallas_call(
        _combine_kernel,
        grid=(s // tt,),
        in_specs=[pl.BlockSpec((1, 1, tt * TOP_K_EXPERTS), lambda i: (i, 0, 0), memory_space=pltpu.SMEM),
                  pl.BlockSpec((tt, D_MODEL), lambda i: (i, 0)),
                  pl.BlockSpec((tt, LANES), lambda i: (i, 0)),
                  pl.BlockSpec((1, D_MODEL), lambda i: (0, 0)),
                  pl.BlockSpec(memory_space=pl.ANY)],
        out_specs=pl.BlockSpec((tt, D_MODEL), lambda i: (i, 0)),
        out_shape=jax.ShapeDtypeStruct((s, D_MODEL), jnp.float32),
        scratch_shapes=[pltpu.VMEM((TOP_K_EXPERTS, tt, D_MODEL), jnp.float32),
                        pltpu.SemaphoreType.DMA(())],
        compiler_params=_cparams(("arbitrary",)),
        name="combine",
    )(dest3, h, gate, g_final, ys)


def _rope_tables(pos):
    half = HEAD_DIM // 2
    inv = ROPE_THETA ** (-jnp.arange(half, dtype=jnp.float32) / half)
    ang = pos.astype(jnp.float32)[:, None] * inv
    cos, sin = jnp.cos(ang), jnp.sin(ang)
    zero = jnp.zeros_like(sin)
    reps = LANES // HEAD_DIM
    cos_t = jnp.tile(jnp.concatenate([cos, cos], axis=-1), (1, reps))
    sin_lo = jnp.tile(jnp.concatenate([-sin, zero], axis=-1), (1, reps))
    sin_hi = jnp.tile(jnp.concatenate([zero, sin], axis=-1), (1, reps))
    return cos_t, sin_lo, sin_hi


def _block_bounds(chunk, q_rows, k_rows):
    cq_max = jnp.max(chunk.reshape(-1, q_rows), axis=1)
    ck_min = jnp.min(chunk.reshape(-1, k_rows), axis=1)
    need = ck_min[None, :] <= cq_max[:, None]
    last = jnp.max(jnp.where(need, jnp.arange(ck_min.shape[0], dtype=jnp.int32)[None, :] + 1, 1), axis=1)
    return last.astype(jnp.int32)


def _layer(h, pos, norm_mix_g, w_in, idx_k_norm_g, idx_k_norm_b, conv_w, conv_b, conv_norm_g, conv_norm_b,
           w_out, norm_ffn_g, w_router, b_router, w_gate_up, b_gate_up, w_down, b_down, out_gain):
    s = h.shape[0]
    f32, bf16 = jnp.float32, jnp.bfloat16
    n_main = 2 * C_CONV + 3 * D_ATTN + IDX_HEADS * IDX_DIM
    w_main = w_in[:, :n_main].astype(bf16)
    w_tail = jnp.pad(w_in[:, n_main:], ((0, 0), (0, LANES - (IDX_DIM + IDX_HEADS)))).astype(bf16)
    cos_t, sin_lo, sin_hi = _rope_tables(pos)
    cw = jnp.pad(conv_w, ((0, CONV_HALO - CONV_WIDTH), (0, 0)))
    kng = jnp.pad(idx_k_norm_g, (0, LANES - IDX_DIM))[None, :]
    knb = jnp.pad(idx_k_norm_b, (0, LANES - IDX_DIM))[None, :]
    u, q, k, v, qi, tail = _inproj(h, norm_mix_g[None, :], w_main, w_tail, cos_t, sin_lo, sin_hi, cw,
                                   conv_b[None, :], conv_norm_g[None, :], conv_norm_b[None, :], kng, knb)

    chunk = pos // CHUNK
    k_sel = min(TOPK_KEYS_MAX, s // 4)
    qb = min(Q_BLOCK, s)
    ki = tail[:, :IDX_DIM].astype(bf16)
    wi_t = tail[:, IDX_DIM:IDX_DIM + IDX_HEADS].T
    bias3 = _indexer(_block_bounds(chunk, qb, qb), qi, wi_t, chunk[None, :], ki,
                     jnp.broadcast_to(chunk[:, None], (s, qb)), k_sel)
    vt_aug = jnp.concatenate(
        [v.T.reshape(D_ATTN // LANES, LANES, s), jnp.ones((D_ATTN // LANES, ATT_V_ROWS - LANES, s), bf16)],
        axis=1).reshape(-1, s)
    a = _attention(_block_bounds(chunk, min(ATT_TQ, s), min(ATT_TK, s)), q, k, vt_aug, bias3)

    w_r = jnp.pad(w_router, ((0, 0), (0, LANES - N_EXPERTS)))
    b_r = jnp.pad(b_router, (0, LANES - N_EXPERTS), constant_values=-jnp.inf)[None, :]
    h1, xn, eidx, gate, rank, counts = _outproj(h, u, a, w_out[:C_CONV].astype(bf16), w_out[C_CONV:].astype(bf16),
                                                norm_ffn_g[None, :], w_r, b_r)

    blk = EXPERT_BLOCK
    cnt = counts[0, :N_EXPERTS].astype(jnp.int32)
    padded = (cnt + blk - 1) // blk * blk
    end = jnp.cumsum(padded)
    start = end - padded
    n_blocks = s * TOP_K_EXPERTS // blk + N_EXPERTS
    dest = start[eidx[:, :TOP_K_EXPERTS]] + rank[:, :TOP_K_EXPERTS]
    tt = min(ROUTE_TILE, s)
    dest3 = dest.reshape(s // tt, 1, tt * TOP_K_EXPERTS)
    block_row = jnp.arange(n_blocks, dtype=jnp.int32) * blk
    block_expert = jnp.minimum(jnp.sum(end[None, :] <= block_row[:, None], axis=1), N_EXPERTS - 1).astype(jnp.int32)
    n_used = (end[-1:] // blk).astype(jnp.int32)

    xs = _dispatch(dest3, xn, jnp.zeros((n_blocks * blk, D_MODEL), f32))
    ys = _experts(block_expert, n_used, xs, w_gate_up, b_gate_up[:, None, :], w_down, b_down[:, None, :])
    return _combine(dest3, h1, gate, out_gain[None, :], ys)


def kernel(x, positions, norm_mix_g, w_in, idx_k_norm_g, idx_k_norm_b, conv_w, conv_b, conv_norm_g, conv_norm_b,
           w_out, norm_ffn_g, w_router, b_router, w_gate_up, b_gate_up, w_down, b_down, norm_final_g):
    assert x.shape[0] == 1 and norm_mix_g.shape[0] == 1, "single sequence, single layer"
    y = _layer(x[0], positions[0], norm_mix_g[0], w_in[0], idx_k_norm_g[0], idx_k_norm_b[0], conv_w[0],
               conv_b[0], conv_norm_g[0], conv_norm_b[0], w_out[0], norm_ffn_g[0], w_router[0], b_router[0],
               w_gate_up[0], b_gate_up[0], w_down[0], b_down[0], norm_final_g)
    return y[None]
```

```python
import functools
import math

import jax
import jax.numpy as jnp
from jax import lax
from jax.experimental import pallas as pl
from jax.experimental.pallas import tpu as pltpu

D_MODEL = 1024
CHUNK = 64
C_CONV = 512
CONV_WIDTH = 31
N_HEADS = 8
HEAD_DIM = 64
D_ATTN = N_HEADS * HEAD_DIM
IDX_HEADS = 8
IDX_DIM = 64
TOPK_KEYS_MAX = 256
ROPE_THETA = 10000.0
N_EXPERTS = 32
TOP_K_EXPERTS = 4
D_EXPERT = 1024
SWIGLU_LIMIT = 7.0
SWIGLU_ALPHA = 1.702
NORM_EPS = 1e-5

LANES = 128
SUBLANES = 8
CONV_HALO = 32
ROW_TILE = 256
Q_BLOCK = 128
KEY_CHUNK = 512
ATT_TQ = 512
ATT_TK = 512
ATT_V_ROWS = LANES + 16
Q_SCALE = HEAD_DIM ** -0.5 * math.log2(math.e)
EXPERT_BLOCK = 256
ROUTE_TILE = 256
ROUTE_UNROLL = 4
VMEM_LIMIT = 56 * 1024 * 1024

_NT = (((1,), (1,)), ((), ()))


def _cparams(sem):
    return pltpu.CompilerParams(dimension_semantics=sem, vmem_limit_bytes=VMEM_LIMIT)


def _rope128(xp, cos_t, sin_lo, sin_hi):
    return xp * cos_t + pltpu.roll(xp, 96, 1) * sin_lo + pltpu.roll(xp, 32, 1) * sin_hi


def _inproj_kernel(x_ref, g_ref, wm_ref, wt_ref, cos_ref, slo_ref, shi_ref, cw_ref, cb_ref,
                   cng_ref, cnb_ref, kng_ref, knb_ref,
                   u_ref, q_ref, k_ref, v_ref, qi_ref, tail_ref, ubuf_ref, ushift_ref):
    ts = x_ref.shape[0]
    x = x_ref[...]
    ms = jnp.mean(x * x, axis=-1, keepdims=True)
    hn = (x * lax.rsqrt(ms + NORM_EPS) * g_ref[...]).astype(jnp.bfloat16)
    proj = jnp.dot(hn, wm_ref[...], preferred_element_type=jnp.float32)
    tail = jnp.dot(hn, wt_ref[...], preferred_element_type=jnp.float32)

    u = proj[:, 0:C_CONV] * jax.nn.sigmoid(proj[:, C_CONV:2 * C_CONV])

    @pl.when(pl.program_id(0) == 0)
    def _():
        ubuf_ref[0:CONV_HALO, :] = jnp.zeros((CONV_HALO, C_CONV), jnp.float32)

    ubuf_ref[CONV_HALO:CONV_HALO + ts, :] = u
    span = ts + CONV_HALO - SUBLANES
    for b in range(1, SUBLANES):
        ushift_ref[b - 1] = ubuf_ref[b:b + span, :]
    acc = jnp.zeros((ts, C_CONV), jnp.float32) + cb_ref[...]
    for kk in range(CONV_WIDTH):
        off = CONV_HALO - (CONV_WIDTH - 1) + kk
        a, b = off // SUBLANES * SUBLANES, off % SUBLANES
        tap = ubuf_ref[a:a + ts, :] if b == 0 else ushift_ref[b - 1, a:a + ts, :]
        acc = acc + tap * cw_ref[kk:kk + 1, :]
    ubuf_ref[0:CONV_HALO, :] = ubuf_ref[ts:ts + CONV_HALO, :]
    mu = jnp.mean(acc, axis=-1, keepdims=True)
    d = acc - mu
    var = jnp.mean(d * d, axis=-1, keepdims=True)
    yn = d * lax.rsqrt(var + NORM_EPS) * cng_ref[...] + cnb_ref[...]
    u_ref[...] = (yn * jax.nn.sigmoid(yn)).astype(jnp.bfloat16)

    cos_t, sin_lo, sin_hi = cos_ref[...], slo_ref[...], shi_ref[...]
    base = 2 * C_CONV
    for p in range(D_ATTN // LANES):
        lo = p * LANES
        qp = proj[:, base + lo:base + lo + LANES]
        q_ref[:, lo:lo + LANES] = (_rope128(qp, cos_t, sin_lo, sin_hi) * Q_SCALE).astype(jnp.bfloat16)
        kp = proj[:, base + D_ATTN + lo:base + D_ATTN + lo + LANES]
        k_ref[:, lo:lo + LANES] = _rope128(kp, cos_t, sin_lo, sin_hi).astype(jnp.bfloat16)
        qip = proj[:, base + 3 * D_ATTN + lo:base + 3 * D_ATTN + lo + LANES]
        qi_ref[:, lo:lo + LANES] = _rope128(qip, cos_t, sin_lo, sin_hi).astype(jnp.bfloat16)
    v_ref[...] = proj[:, base + 2 * D_ATTN:base + 3 * D_ATTN].astype(jnp.bfloat16)

    lane = lax.broadcasted_iota(jnp.int32, tail.shape, 1)
    is_k = lane < IDX_DIM
    kmu = jnp.sum(jnp.where(is_k, tail, 0.0), axis=-1, keepdims=True) * (1.0 / IDX_DIM)
    kd = jnp.where(is_k, tail - kmu, 0.0)
    kvar = jnp.sum(kd * kd, axis=-1, keepdims=True) * (1.0 / IDX_DIM)
    kn = kd * lax.rsqrt(kvar + NORM_EPS) * kng_ref[...] + knb_ref[...]
    kr = _rope128(kn, cos_t, sin_lo, sin_hi)
    wi = tail * (IDX_HEADS ** -0.5 * IDX_DIM ** -0.5)
    tail_ref[...] = jnp.where(is_k, kr, jnp.where(lane < IDX_DIM + IDX_HEADS, wi, 0.0))


def _inproj(x, g, w_main, w_tail, cos_t, sin_lo, sin_hi, cw, cb, cng, cnb, kng, knb):
    s = x.shape[0]
    ts = min(ROW_TILE, s)
    row = lambda w: pl.BlockSpec((ts, w), lambda i: (i, 0))
    full = lambda a: pl.BlockSpec(a.shape, lambda i: (0,) * a.ndim)
    return pl.pallas_call(
        _inproj_kernel,
        grid=(s // ts,),
        in_specs=[row(D_MODEL), full(g), full(w_main), full(w_tail), row(LANES), row(LANES), row(LANES),
                  full(cw), full(cb), full(cng), full(cnb), full(kng), full(knb)],
        out_specs=[row(C_CONV), row(D_ATTN), row(D_ATTN), row(D_ATTN), row(D_ATTN), row(LANES)],
        out_shape=[jax.ShapeDtypeStruct((s, C_CONV), jnp.bfloat16),
                   jax.ShapeDtypeStruct((s, D_ATTN), jnp.bfloat16),
                   jax.ShapeDtypeStruct((s, D_ATTN), jnp.bfloat16),
                   jax.ShapeDtypeStruct((s, D_ATTN), jnp.bfloat16),
                   jax.ShapeDtypeStruct((s, D_ATTN), jnp.bfloat16),
                   jax.ShapeDtypeStruct((s, LANES), jnp.float32)],
        scratch_shapes=[pltpu.VMEM((ts + CONV_HALO, C_CONV), jnp.float32),
                        pltpu.VMEM((SUBLANES - 1, ts + CONV_HALO - SUBLANES, C_CONV), jnp.float32)],
        compiler_params=_cparams(("arbitrary",)),
        name="inproj",
    )(x, g, w_main, w_tail, cos_t, sin_lo, sin_hi, cw, cb, cng, cnb, kng, knb)


def _float_to_key(x):
    b = lax.bitcast_convert_type(x, jnp.int32)
    return jnp.where(b < 0, b ^ jnp.int32(0x7FFFFFFF), b)


def _key_to_float(k):
    b = jnp.where(k < 0, k ^ jnp.int32(0x7FFFFFFF), k)
    return lax.bitcast_convert_type(b, jnp.float32)


def _avg_floor(a, b):
    return (a >> 1) + (b >> 1) + (a & b & 1)


def _probit_upper(q):
    p = jnp.minimum(q, 1.0 - q)
    t = jnp.sqrt(-2.0 * jnp.log(p))
    z = t - (2.515517 + t * (0.802853 + t * 0.010328)) / (1.0 + t * (1.432788 + t * (0.189269 + t * 0.001308)))
    return jnp.where(q <= 0.5, z, -z)


def _indexer_kernel(nkb_ref, qi_ref, wi_ref, cq_ref, ki_ref, ck_ref, bias_ref, sc_ref, *, k_sel):
    qb = qi_ref.shape[0]
    s = ki_ref.shape[0]
    kc = min(KEY_CHUNK, s)
    grp = kc // SUBLANES
    n_total = s // kc
    nch = (nkb_ref[pl.program_id(0)] * Q_BLOCK + kc - 1) // kc
    f32, i32 = jnp.float32, jnp.int32
    neg_inf, pos_inf = f32(-jnp.inf), f32(jnp.inf)

    def as3(a):
        return a.reshape(grp, SUBLANES, qb)

    def rows8(v):
        return jnp.broadcast_to(v, (SUBLANES, qb))

    qi = qi_ref[...]
    q_pairs = [jnp.concatenate([qi[:, (2 * p) * IDX_DIM:(2 * p + 1) * IDX_DIM],
                                qi[:, (2 * p + 1) * IDX_DIM:(2 * p + 2) * IDX_DIM]], axis=0)
               for p in range(IDX_HEADS // 2)]
    w8 = [rows8(wi_ref[h:h + 1, :]) for h in range(IDX_HEADS)]
    cq8 = rows8(cq_ref[...])

    def score_chunk(c, carry):
        rmax, rmin, nadm, nge0, ngt0 = carry
        k0 = pl.multiple_of(c * kc, kc)
        kic = ki_ref[pl.ds(k0, kc), :]
        acc = jnp.zeros((grp, SUBLANES, qb), f32)
        for p in range(IDX_HEADS // 2):
            sp = lax.dot_general(kic, q_pairs[p], _NT, preferred_element_type=f32)
            acc = acc + w8[2 * p][None] * jnp.maximum(as3(sp[:, :qb]), 0.0)
            acc = acc + w8[2 * p + 1][None] * jnp.maximum(as3(sp[:, qb:]), 0.0)
        adm = as3(ck_ref[pl.ds(k0, kc), :]) <= cq8[None]
        val = jnp.where(adm, acc, neg_inf)
        sc_ref[pl.ds(k0, kc), :] = val.reshape(kc, qb)
        rmax = jnp.maximum(rmax, jnp.max(val, axis=0))
        rmin = jnp.minimum(rmin, jnp.min(jnp.where(adm, acc, pos_inf), axis=0))
        nadm = nadm + jnp.sum(jnp.where(adm, 1, 0), axis=0)
        nge0 = nge0 + jnp.sum(jnp.where(val >= 0.0, 1, 0), axis=0)
        ngt0 = ngt0 + jnp.sum(jnp.where(val > 0.0, 1, 0), axis=0)
        return rmax, rmin, nadm, nge0, ngt0

    zero8 = jnp.zeros((SUBLANES, qb), i32)
    rmax8, rmin8, nadm8, nge8, ngt8 = lax.fori_loop(
        0, nch, score_chunk,
        (jnp.full((SUBLANES, qb), neg_inf, f32), jnp.full((SUBLANES, qb), pos_inf, f32), zero8, zero8, zero8))
    row_max = jnp.max(rmax8, axis=0, keepdims=True)
    row_min = jnp.min(rmin8, axis=0, keepdims=True)
    n_adm = jnp.sum(nadm8, axis=0, keepdims=True)
    c_ge0 = jnp.sum(nge8, axis=0, keepdims=True)
    c_gt0 = jnp.sum(ngt8, axis=0, keepdims=True)

    def count_rows(pred):
        def body(c, acc):
            k0 = pl.multiple_of(c * kc, kc)
            m = pred(as3(sc_ref[pl.ds(k0, kc), :]), k0)
            return acc + jnp.sum(jnp.where(m, 1, 0), axis=0)
        acc = lax.fori_loop(0, nch, body, jnp.zeros((SUBLANES, qb), i32))
        return jnp.sum(acc, axis=0, keepdims=True)

    def count_ge(cand):
        c8 = rows8(cand)[None]
        return count_rows(lambda x3, k0: x3 >= c8)

    k_eff = jnp.minimum(n_adm, k_sel)
    inv_n = 1.0 / (n_adm.astype(f32) + 1.0)

    def zscore(cnt):
        return _probit_upper((cnt.astype(f32) + 0.5) * inv_n)

    z_target = zscore(k_eff)

    above = c_gt0 >= k_eff
    below = c_ge0 < k_eff
    zero_key = jnp.zeros((1, qb), i32)
    lo0 = jnp.where(below, _float_to_key(row_min), zero_key)
    clo0 = jnp.where(below, n_adm, c_ge0)
    hi0 = jnp.where(above, _float_to_key(row_max) + 1, jnp.where(below, zero_key - 1, zero_key + 1))
    chi0 = jnp.where(above, 0, jnp.where(below, c_ge0, c_gt0))

    def settled(lo, hi, clo):
        return jnp.logical_or(clo == k_eff, _avg_floor(lo, hi) == lo)

    def cond(st):
        return st[7] > 0

    def step(st):
        lo, hi, clo, chi, z_lo, z_hi, done, _, phase = st
        lo_f, hi_f = _key_to_float(lo), _key_to_float(hi)
        frac = jnp.clip((z_target - z_lo) / (z_hi - z_lo), 0.02, 0.98)
        guess = lo_f + (hi_f - lo_f) * frac
        guess_ok = jnp.logical_and(guess == guess, jnp.abs(guess) < pos_inf)
        mid = jnp.maximum(_avg_floor(lo, hi), lo + 1)
        cand = jnp.clip(_float_to_key(jnp.where(guess_ok, guess, lo_f)), lo + 1, hi - 1)
        cand = jnp.where(jnp.logical_or(phase == 2, jnp.logical_not(guess_ok)), mid, cand)
        cnt = count_ge(_key_to_float(cand))
        z_c = zscore(cnt)
        ge = cnt >= k_eff
        up = jnp.logical_and(done == 0, ge)
        dn = jnp.logical_and(done == 0, jnp.logical_not(ge))
        lo = jnp.where(up, cand, lo)
        clo = jnp.where(up, cnt, clo)
        z_lo = jnp.where(up, z_c, z_lo)
        hi = jnp.where(dn, cand, hi)
        chi = jnp.where(dn, cnt, chi)
        z_hi = jnp.where(dn, z_c, z_hi)
        done = jnp.where(settled(lo, hi, clo), 1, done)
        return lo, hi, clo, chi, z_lo, z_hi, done, jnp.sum(1 - done), jnp.where(phase == 2, 0, phase + 1)

    done0 = jnp.where(settled(lo0, hi0, clo0), 1, 0)
    st0 = (lo0, hi0, clo0, chi0, zscore(clo0), zscore(chi0), done0, jnp.sum(1 - done0), i32(0))
    lo, hi, clo, chi = lax.while_loop(cond, step, st0)[:4]
    thr = _key_to_float(lo)
    thr8 = rows8(thr)[None]

    tie = clo > k_eff
    need = k_eff - chi
    n_tie = jnp.sum(jnp.where(tie, 1, 0))
    key_iota = (lax.broadcasted_iota(i32, (grp, SUBLANES, qb), 0) * SUBLANES
                + lax.broadcasted_iota(i32, (grp, SUBLANES, qb), 1))

    def store_bias(k0, sel):
        bias_ref[0, pl.ds(k0, kc), :] = jnp.where(sel, 0.0, neg_inf).reshape(kc, qb).astype(jnp.bfloat16)

    @pl.when(n_tie == 0)
    def _():
        def emit_chunk(c, carry):
            k0 = pl.multiple_of(c * kc, kc)
            store_bias(k0, as3(sc_ref[pl.ds(k0, kc), :]) >= thr8)
            return carry
        lax.fori_loop(0, nch, emit_chunk, 0)

    @pl.when(n_tie > 0)
    def _():
        def cnt_upto(m):
            m8 = rows8(m)[None]
            return count_rows(lambda x3, k0: jnp.logical_and(x3 == thr8, key_iota + k0 <= m8))

        def body(_, jj):
            jl, jh = jj
            m = (jl + jh) >> 1
            ok = cnt_upto(m) >= need
            return jnp.where(ok, jl, m), jnp.where(ok, m, jh)

        n_it = max(1, (s - 1).bit_length()) + 1
        _, jh = lax.fori_loop(0, n_it, body, (jnp.full((1, qb), -1, i32), jnp.full((1, qb), s - 1, i32)))
        jlim8 = rows8(jnp.where(tie, jh, s))[None]

        def emit_chunk(c, carry):
            k0 = pl.multiple_of(c * kc, kc)
            x3 = as3(sc_ref[pl.ds(k0, kc), :])
            store_bias(k0, jnp.logical_or(x3 > thr8, jnp.logical_and(x3 == thr8, key_iota + k0 <= jlim8)))
            return carry
        lax.fori_loop(0, nch, emit_chunk, 0)

    def fill_chunk(c, carry):
        k0 = pl.multiple_of(c * kc, kc)
        bias_ref[0, pl.ds(k0, kc), :] = jnp.full((kc, qb), neg_inf, jnp.bfloat16)
        return carry

    lax.fori_loop(nch, n_total, fill_chunk, 0)


def _indexer(nkb, qi, wi_t, cq_row, ki, ck_lanes, k_sel):
    s = qi.shape[0]
    qb = min(Q_BLOCK, s)
    grid_spec = pltpu.PrefetchScalarGridSpec(
        num_scalar_prefetch=1,
        grid=(s // qb,),
        in_specs=[pl.BlockSpec((qb, D_ATTN), lambda i, n: (i, 0)),
                  pl.BlockSpec((IDX_HEADS, qb), lambda i, n: (0, i)),
                  pl.BlockSpec((1, qb), lambda i, n: (0, i)),
                  pl.BlockSpec((s, IDX_DIM), lambda i, n: (0, 0)),
                  pl.BlockSpec((s, qb), lambda i, n: (0, 0))],
        out_specs=pl.BlockSpec((1, s, qb), lambda i, n: (i, 0, 0)),
        scratch_shapes=[pltpu.VMEM((s, qb), jnp.float32)],
    )
    return pl.pallas_call(
        functools.partial(_indexer_kernel, k_sel=k_sel),
        grid_spec=grid_spec,
        out_shape=jax.ShapeDtypeStruct((s // qb, s, qb), jnp.bfloat16),
        compiler_params=_cparams(("arbitrary",)),
        name="indexer",
    )(nkb, qi, wi_t, cq_row, ki, ck_lanes)


def _attention_kernel(nkt_ref, q_ref, k_ref, vt_ref, b_ref, o_ref, qm_ref, m_ref, acc_ref):
    i, j = pl.program_id(0), pl.program_id(1)
    tq = q_ref.shape[0]
    n_pairs = D_ATTN // LANES
    bf16 = jnp.bfloat16

    @pl.when(j == 0)
    def _():
        q = q_ref[...]
        lane = lax.broadcasted_iota(jnp.int32, (tq, LANES), 1)
        zero = jnp.zeros((tq, LANES), q.dtype)
        for p in range(n_pairs):
            qp = q[:, p * LANES:(p + 1) * LANES]
            qm_ref[2 * p] = jnp.where(lane < HEAD_DIM, qp, zero)
            qm_ref[2 * p + 1] = jnp.where(lane < HEAD_DIM, zero, qp)
        m_ref[...] = jnp.full(m_ref.shape, -1e30, jnp.float32)
        acc_ref[...] = jnp.zeros(acc_ref.shape, jnp.float32)

    @pl.when(j < nkt_ref[i])
    def _():
        bias = jnp.concatenate([b_ref[r] for r in range(b_ref.shape[0])], axis=1)
        scores = []
        for h in range(N_HEADS):
            kp = k_ref[:, (h // 2) * LANES:(h // 2 + 1) * LANES]
            st = lax.dot_general(kp, qm_ref[h], _NT, preferred_element_type=jnp.float32)
            scores.append(st.astype(bf16) + bias)
        for h in range(N_HEADS):
            p = h // 2
            vtp = vt_ref[p * ATT_V_ROWS:(p + 1) * ATT_V_ROWS, :]
            st = scores[h]
            m_prev = m_ref[h]
            m_new = jnp.maximum(m_prev, jnp.max(st, axis=0, keepdims=True).astype(jnp.float32))
            alpha = jnp.exp2(m_prev - m_new)
            pt = jnp.exp2(st - m_new.astype(bf16))
            acc_ref[h] = alpha * acc_ref[h] + jnp.dot(vtp, pt, preferred_element_type=jnp.float32)
            m_ref[h] = m_new

    @pl.when(j == nkt_ref[i] - 1)
    def _():
        row = lax.broadcasted_iota(jnp.int32, (LANES, tq), 0)
        for p in range(n_pairs):
            a0, a1 = acc_ref[2 * p], acc_ref[2 * p + 1]
            o0 = a0[:LANES] / a0[LANES:LANES + 1]
            o1 = a1[:LANES] / a1[LANES:LANES + 1]
            o_ref[:, p * LANES:(p + 1) * LANES] = jnp.where(row < HEAD_DIM, o0, o1).T.astype(o_ref.dtype)


def _attention(nkt, q, k, vt_aug, bias3):
    s = q.shape[0]
    tq, tk = min(ATT_TQ, s), min(ATT_TK, s)
    qb = bias3.shape[2]
    last = lambda i, j, n: jnp.minimum(j, n[i] - 1)
    grid_spec = pltpu.PrefetchScalarGridSpec(
        num_scalar_prefetch=1,
        grid=(s // tq, s // tk),
        in_specs=[pl.BlockSpec((tq, D_ATTN), lambda i, j, n: (i, 0)),
                  pl.BlockSpec((tk, D_ATTN), lambda i, j, n: (last(i, j, n), 0)),
                  pl.BlockSpec((vt_aug.shape[0], tk), lambda i, j, n: (0, last(i, j, n))),
                  pl.BlockSpec((tq // qb, tk, qb), lambda i, j, n: (i, last(i, j, n), 0))],
        out_specs=pl.BlockSpec((tq, D_ATTN), lambda i, j, n: (i, 0)),
        scratch_shapes=[pltpu.VMEM((N_HEADS, tq, LANES), jnp.bfloat16),
                        pltpu.VMEM((N_HEADS, 1, tq), jnp.float32),
                        pltpu.VMEM((N_HEADS, ATT_V_ROWS, tq), jnp.float32)],
    )
    return pl.pallas_call(
        _attention_kernel,
        grid_spec=grid_spec,
        out_shape=jax.ShapeDtypeStruct((s, D_ATTN), jnp.bfloat16),
        compiler_params=_cparams(("arbitrary", "arbitrary")),
        name="attention",
    )(nkt, q, k, vt_aug, bias3)


def _outproj_kernel(x_ref, u_ref, a_ref, wu_ref, wa_ref, g_ref, wr_ref, br_ref,
                    h_ref, xn_ref, eidx_ref, gate_ref, rank_ref, cnt_ref, carry_ref):
    ts = x_ref.shape[0]

    @pl.when(pl.program_id(0) == 0)
    def _():
        carry_ref[...] = jnp.zeros(carry_ref.shape, jnp.float32)

    h = (x_ref[...]
         + jnp.dot(u_ref[...], wu_ref[...], preferred_element_type=jnp.float32)
         + jnp.dot(a_ref[...], wa_ref[...], preferred_element_type=jnp.float32))
    h_ref[...] = h
    ms = jnp.mean(h * h, axis=-1, keepdims=True)
    xn = h * lax.rsqrt(ms + NORM_EPS) * g_ref[...]
    xn_ref[...] = xn
    logits_t = lax.dot_general(wr_ref[...], xn, _NT, preferred_element_type=jnp.float32,
                               precision=lax.Precision.HIGHEST) + br_ref[...]
    erow = lax.broadcasted_iota(jnp.int32, (N_EXPERTS, ts), 0)
    work = logits_t
    vals, idxs = [], []
    multi = jnp.zeros((N_EXPERTS, ts), jnp.float32)
    for _ in range(TOP_K_EXPERTS):
        mx = jnp.max(work, axis=0, keepdims=True)
        ix = jnp.min(jnp.where(work == mx, erow, N_EXPERTS), axis=0, keepdims=True)
        hit = erow == ix
        multi = jnp.where(hit, 1.0, multi)
        work = jnp.where(hit, -jnp.inf, work)
        vals.append(mx)
        idxs.append(ix)
    ex = [jnp.exp(v - vals[0]) for v in vals]
    den = ex[0] + ex[1] + ex[2] + ex[3]

    r = lax.broadcasted_iota(jnp.int32, (ts, ts), 0)
    c = lax.broadcasted_iota(jnp.int32, (ts, ts), 1)
    earlier = jnp.where(r < c, 1.0, 0.0).astype(jnp.bfloat16)
    prior = jnp.dot(multi.astype(jnp.bfloat16), earlier, preferred_element_type=jnp.float32) + carry_ref[...]
    row8 = lax.broadcasted_iota(jnp.int32, (SUBLANES, ts), 0)
    eidx = jnp.zeros((SUBLANES, ts), jnp.int32)
    gate = jnp.zeros((SUBLANES, ts), jnp.float32)
    rank = jnp.zeros((SUBLANES, ts), jnp.int32)
    for kk in range(TOP_K_EXPERTS):
        rk = jnp.sum(jnp.where(erow == idxs[kk], prior, 0.0), axis=0, keepdims=True)
        eidx = jnp.where(row8 == kk, idxs[kk], eidx)
        gate = jnp.where(row8 == kk, ex[kk] / den, gate)
        rank = jnp.where(row8 == kk, rk.astype(jnp.int32), rank)
    eidx_ref[...] = eidx
    gate_ref[...] = gate
    rank_ref[...] = rank
    carry_ref[...] = carry_ref[...] + jnp.sum(multi, axis=1, keepdims=True)
    cnt_ref[...] = jnp.broadcast_to(carry_ref[...], cnt_ref.shape)


def _outproj(x, u, a, w_u, w_a, g, w_r, b_r):
    s = x.shape[0]
    ts = min(ROW_TILE, s)
    row = lambda w: pl.BlockSpec((ts, w), lambda i: (i, 0))
    col = lambda: pl.BlockSpec((SUBLANES, ts), lambda i: (0, i))
    full = lambda arr: pl.BlockSpec(arr.shape, lambda i: (0,) * arr.ndim)
    return pl.pallas_call(
        _outproj_kernel,
        grid=(s // ts,),
        in_specs=[row(D_MODEL), row(C_CONV), row(D_ATTN), full(w_u), full(w_a), full(g), full(w_r), full(b_r)],
        out_specs=[row(D_MODEL), row(D_MODEL), col(), col(), col(),
                   pl.BlockSpec((N_EXPERTS, LANES), lambda i: (0, 0))],
        out_shape=[jax.ShapeDtypeStruct((s, D_MODEL), jnp.float32),
                   jax.ShapeDtypeStruct((s, D_MODEL), jnp.float32),
                   jax.ShapeDtypeStruct((SUBLANES, s), jnp.int32),
                   jax.ShapeDtypeStruct((SUBLANES, s), jnp.float32),
                   jax.ShapeDtypeStruct((SUBLANES, s), jnp.int32),
                   jax.ShapeDtypeStruct((N_EXPERTS, LANES), jnp.float32)],
        scratch_shapes=[pltpu.VMEM((N_EXPERTS, 1), jnp.float32)],
        compiler_params=_cparams(("arbitrary",)),
        name="outproj_router",
    )(x, u, a, w_u, w_a, g, w_r, b_r)


def _row_copy(src_ref, dst_ref, sem, src_row, dst_row):
    return pltpu.make_async_copy(src_ref.at[pl.ds(src_row, 1), :], dst_ref.at[pl.ds(dst_row, 1), :], sem)


def _dispatch_kernel(dest_ref, xn_ref, xs_in_ref, xs_ref, sem):
    del xs_in_ref
    tt = xn_ref.shape[0]

    def start(r, carry):
        for kk in range(TOP_K_EXPERTS):
            _row_copy(xn_ref, xs_ref, sem, r, dest_ref[0, 0, r * TOP_K_EXPERTS + kk]).start(priority=kk % 2)
        return carry

    def wait(r, carry):
        for kk in range(TOP_K_EXPERTS):
            _row_copy(xn_ref, xs_ref, sem, r, dest_ref[0, 0, r * TOP_K_EXPERTS + kk]).wait()
        return carry

    lax.fori_loop(0, tt, start, 0, unroll=ROUTE_UNROLL)
    lax.fori_loop(0, tt, wait, 0, unroll=ROUTE_UNROLL)


def _dispatch(dest3, xn, xs_init):
    s = xn.shape[0]
    tt = min(ROUTE_TILE, s)
    return pl.pallas_call(
        _dispatch_kernel,
        grid=(s // tt,),
        in_specs=[pl.BlockSpec((1, 1, tt * TOP_K_EXPERTS), lambda i: (i, 0, 0), memory_space=pltpu.SMEM),
                  pl.BlockSpec((tt, D_MODEL), lambda i: (i, 0)),
                  pl.BlockSpec(memory_space=pl.ANY)],
        out_specs=pl.BlockSpec(memory_space=pl.ANY),
        out_shape=jax.ShapeDtypeStruct(xs_init.shape, xs_init.dtype),
        scratch_shapes=[pltpu.SemaphoreType.DMA(())],
        input_output_aliases={2: 0},
        compiler_params=_cparams(("arbitrary",)),
        name="dispatch",
    )(dest3, xn, xs_init)


def _experts_kernel(be_ref, nb_ref, xs_ref, wgu_ref, bgu_ref, wdn_ref, bdn_ref, ys_ref, wgu_bf, wdn_bf):
    b = pl.program_id(0)
    prev = be_ref[jnp.maximum(b - 1, 0)]
    fresh = jnp.logical_or(b == 0, be_ref[b] != prev)

    @pl.when(jnp.logical_and(b < nb_ref[0], fresh))
    def _():
        wgu_bf[...] = wgu_ref[0].astype(jnp.bfloat16)
        wdn_bf[...] = wdn_ref[0].astype(jnp.bfloat16)

    @pl.when(b < nb_ref[0])
    def _():
        half = xs_ref.shape[0] // 2
        gus = [jnp.dot(xs_ref[r * half:(r + 1) * half, :].astype(jnp.bfloat16), wgu_bf[...],
                       preferred_element_type=jnp.float32) + bgu_ref[0] for r in range(2)]
        for r in range(2):
            g = jnp.minimum(gus[r][:, :D_EXPERT], SWIGLU_LIMIT)
            u = jnp.clip(gus[r][:, D_EXPERT:], -SWIGLU_LIMIT, SWIGLU_LIMIT)
            hdn = g * jax.nn.sigmoid(SWIGLU_ALPHA * g) * (u + 1.0)
            ys_ref[r * half:(r + 1) * half, :] = jnp.dot(hdn.astype(jnp.bfloat16), wdn_bf[...],
                                                        preferred_element_type=jnp.float32) + bdn_ref[0]

    @pl.when(b >= nb_ref[0])
    def _():
        ys_ref[...] = jnp.zeros(ys_ref.shape, ys_ref.dtype)


def _experts(block_expert, n_used, xs, w_gu, b_gu, w_dn, b_dn):
    n_rows = xs.shape[0]
    blk = EXPERT_BLOCK
    row_map = lambda b, be, nb: (jnp.maximum(jnp.minimum(b, nb[0] - 1), 0), 0)
    exp_map = lambda b, be, nb: (be[b], 0, 0)
    grid_spec = pltpu.PrefetchScalarGridSpec(
        num_scalar_prefetch=2,
        grid=(n_rows // blk,),
        in_specs=[pl.BlockSpec((blk, D_MODEL), row_map),
                  pl.BlockSpec((1, D_MODEL, 2 * D_EXPERT), exp_map),
                  pl.BlockSpec((1, 1, 2 * D_EXPERT), exp_map),
                  pl.BlockSpec((1, D_EXPERT, D_MODEL), exp_map),
                  pl.BlockSpec((1, 1, D_MODEL), exp_map)],
        out_specs=pl.BlockSpec((blk, D_MODEL), lambda b, be, nb: (b, 0)),
        scratch_shapes=[pltpu.VMEM((D_MODEL, 2 * D_EXPERT), jnp.bfloat16),
                        pltpu.VMEM((D_EXPERT, D_MODEL), jnp.bfloat16)],
    )
    return pl.pallas_call(
        _experts_kernel,
        grid_spec=grid_spec,
        out_shape=jax.ShapeDtypeStruct((n_rows, D_MODEL), jnp.float32),
        compiler_params=_cparams(("arbitrary",)),
        name="experts",
    )(block_expert, n_used, xs, w_gu, b_gu, w_dn, b_dn)


def _combine_kernel(dest_ref, h_ref, gate_ref, g_ref, ys_ref, o_ref, buf_ref, sem):
    tt = h_ref.shape[0]

    def start(r, carry):
        for kk in range(TOP_K_EXPERTS):
            _row_copy(ys_ref, buf_ref.at[kk], sem, dest_ref[0, 0, r * TOP_K_EXPERTS + kk], r).start(priority=kk % 2)
        return carry

    def wait(r, carry):
        for kk in range(TOP_K_EXPERTS):
            _row_copy(ys_ref, buf_ref.at[kk], sem, dest_ref[0, 0, r * TOP_K_EXPERTS + kk], r).wait()
        return carry

    lax.fori_loop(0, tt, start, 0, unroll=ROUTE_UNROLL)
    lax.fori_loop(0, tt, wait, 0, unroll=ROUTE_UNROLL)
    gate = gate_ref[...]
    h = h_ref[...]
    for kk in range(TOP_K_EXPERTS):
        h = h + gate[:, kk:kk + 1] * buf_ref[kk]
    ms = jnp.mean(h * h, axis=-1, keepdims=True)
    o_ref[...] = h * lax.rsqrt(ms + NORM_EPS) * g_ref[...]


def _combine(dest3, h, gate, g_final, ys):
    s = h.shape[0]
    tt = min(ROUTE_TILE, s)
    return pl.pallas_call(
        _combine_kernel,
        grid=(s // tt,),
        in_specs=[pl.BlockSpec((1, 1, tt * TOP_K_EXPERTS), lambda i: (i, 0, 0), memory_space=pltpu.SMEM),
                  pl.BlockSpec((tt, D_MODEL), lambda i: (i, 0)),
                  pl.BlockSpec((tt, TOP_K_EXPERTS), lambda i: (i, 0)),
                  pl.BlockSpec((1, D_MODEL), lambda i: (0, 0)),
                  pl.BlockSpec(memory_space=pl.ANY)],
        out_specs=pl.BlockSpec((tt, D_MODEL), lambda i: (i, 0)),
        out_shape=jax.ShapeDtypeStruct((s, D_MODEL), jnp.float32),
        scratch_shapes=[pltpu.VMEM((TOP_K_EXPERTS, tt, D_MODEL), jnp.float32),
                        pltpu.SemaphoreType.DMA(())],
        compiler_params=_cparams(("arbitrary",)),
        name="combine",
    )(dest3, h, gate, g_final, ys)


def _rope_tables(pos):
    half = HEAD_DIM // 2
    inv = ROPE_THETA ** (-jnp.arange(half, dtype=jnp.float32) / half)
    ang = pos.astype(jnp.float32)[:, None] * inv
    cos, sin = jnp.cos(ang), jnp.sin(ang)
    zero = jnp.zeros_like(sin)
    reps = LANES // HEAD_DIM
    cos_t = jnp.tile(jnp.concatenate([cos, cos], axis=-1), (1, reps))
    sin_lo = jnp.tile(jnp.concatenate([-sin, zero], axis=-1), (1, reps))
    sin_hi = jnp.tile(jnp.concatenate([zero, sin], axis=-1), (1, reps))
    return cos_t, sin_lo, sin_hi


def _block_bounds(chunk, q_rows, k_rows):
    cq_max = jnp.max(chunk.reshape(-1, q_rows), axis=1)
    ck_min = jnp.min(chunk.reshape(-1, k_rows), axis=1)
    need = ck_min[None, :] <= cq_max[:, None]
    last = jnp.max(jnp.where(need, jnp.arange(ck_min.shape[0], dtype=jnp.int32)[None, :] + 1, 1), axis=1)
    return last.astype(jnp.int32)


def _layer(h, pos, norm_mix_g, w_in, idx_k_norm_g, idx_k_norm_b, conv_w, conv_b, conv_norm_g, conv_norm_b,
           w_out, norm_ffn_g, w_router, b_router, w_gate_up, b_gate_up, w_down, b_down, out_gain):
    s = h.shape[0]
    f32, bf16 = jnp.float32, jnp.bfloat16
    n_main = 2 * C_CONV + 3 * D_ATTN + IDX_HEADS * IDX_DIM
    w_main = w_in[:, :n_main].astype(bf16)
    w_tail = jnp.pad(w_in[:, n_main:], ((0, 0), (0, LANES - (IDX_DIM + IDX_HEADS)))).astype(bf16)
    cos_t, sin_lo, sin_hi = _rope_tables(pos)
    cw = jnp.pad(conv_w, ((0, CONV_HALO - CONV_WIDTH), (0, 0)))
    kng = jnp.pad(idx_k_norm_g, (0, LANES - IDX_DIM))[None, :]
    knb = jnp.pad(idx_k_norm_b, (0, LANES - IDX_DIM))[None, :]
    u, q, k, v, qi, tail = _inproj(h, norm_mix_g[None, :], w_main, w_tail, cos_t, sin_lo, sin_hi, cw,
                                   conv_b[None, :], conv_norm_g[None, :], conv_norm_b[None, :], kng, knb)

    chunk = pos // CHUNK
    k_sel = min(TOPK_KEYS_MAX, s // 4)
    qb = min(Q_BLOCK, s)
    ki = tail[:, :IDX_DIM].astype(bf16)
    wi_t = tail[:, IDX_DIM:IDX_DIM + IDX_HEADS].T
    bias3 = _indexer(_block_bounds(chunk, qb, qb), qi, wi_t, chunk[None, :], ki,
                     jnp.broadcast_to(chunk[:, None], (s, qb)), k_sel)
    vt_aug = jnp.concatenate(
        [v.T.reshape(D_ATTN // LANES, LANES, s), jnp.ones((D_ATTN // LANES, ATT_V_ROWS - LANES, s), bf16)],
        axis=1).reshape(-1, s)
    a = _attention(_block_bounds(chunk, min(ATT_TQ, s), min(ATT_TK, s)), q, k, vt_aug, bias3)

    h1, xn, eidx_t, gate_t, rank_t, counts = _outproj(
        h, u, a, w_out[:C_CONV].astype(bf16), w_out[C_CONV:].astype(bf16), norm_ffn_g[None, :],
        w_router.T, b_router[:, None])
    eidx, rank = eidx_t[:TOP_K_EXPERTS].T, rank_t[:TOP_K_EXPERTS].T
    gate = gate_t[:TOP_K_EXPERTS].T

    blk = EXPERT_BLOCK
    cnt = counts[:, 0].astype(jnp.int32)
    padded = (cnt + blk - 1) // blk * blk
    end = jnp.cumsum(padded)
    start = end - padded
    n_blocks = s * TOP_K_EXPERTS // blk + N_EXPERTS
    dest = start[eidx] + rank
    tt = min(ROUTE_TILE, s)
    dest3 = dest.reshape(s // tt, 1, tt * TOP_K_EXPERTS)
    block_row = jnp.arange(n_blocks, dtype=jnp.int32) * blk
    block_expert = jnp.minimum(jnp.sum(end[None, :] <= block_row[:, None], axis=1), N_EXPERTS - 1).astype(jnp.int32)
    n_used = (end[-1:] // blk).astype(jnp.int32)

    xs = _dispatch(dest3, xn, jnp.zeros((n_blocks * blk, D_MODEL), f32))
    ys = _experts(block_expert, n_used, xs, w_gate_up, b_gate_up[:, None, :], w_down, b_down[:, None, :])
    return _combine(dest3, h1, gate, out_gain[None, :], ys)


def kernel(x, positions, norm_mix_g, w_in, idx_k_norm_g, idx_k_norm_b, conv_w, conv_b, conv_norm_g, conv_norm_b,
           w_out, norm_ffn_g, w_router, b_router, w_gate_up, b_gate_up, w_down, b_down, norm_final_g):
    assert x.shape[0] == 1 and norm_mix_g.shape[0] == 1, "single sequence, single layer"
    y = _layer(x[0], positions[0], norm_mix_g[0], w_in[0], idx_k_norm_g[0], idx_k_norm_b[0], conv_w[0],
               conv_b[0], conv_norm_g[0], conv_norm_b[0], w_out[0], norm_ffn_g[0], w_router[0], b_router[0],
               w_gate_up[0], b_gate_up[0], w_down[0], b_down[0], norm_final_g)
    return y[None]
```

```python
import functools
import math

import jax
import jax.numpy as jnp
from jax import lax
from jax.experimental import pallas as pl
from jax.experimental.pallas import tpu as pltpu

D_MODEL = 1024
CHUNK = 64
C_CONV = 512
CONV_WIDTH = 31
N_HEADS = 8
HEAD_DIM = 64
D_ATTN = N_HEADS * HEAD_DIM
IDX_HEADS = 8
IDX_DIM = 64
TOPK_KEYS_MAX = 256
ROPE_THETA = 10000.0
N_EXPERTS = 32
TOP_K_EXPERTS = 4
D_EXPERT = 1024
SWIGLU_LIMIT = 7.0
SWIGLU_ALPHA = 1.702
NORM_EPS = 1e-5

LANES = 128
SUBLANES = 8
CONV_HALO = 32
ROW_TILE = 256
Q_BLOCK = 128
KEY_CHUNK = 512
SCORE_UNROLL = 2
ATT_TQ = 512
ATT_TK = 512
ATT_V_ROWS = LANES + 16
Q_SCALE = HEAD_DIM ** -0.5 * math.log2(math.e)
EXPERT_BLOCK = 256
ROUTE_TILE = 256
ROUTE_UNROLL = 4
VMEM_LIMIT = 56 * 1024 * 1024

_NT = (((1,), (1,)), ((), ()))


def _cparams(sem):
    return pltpu.CompilerParams(dimension_semantics=sem, vmem_limit_bytes=VMEM_LIMIT)


def _rope128(xp, cos_t, sin_lo, sin_hi):
    return xp * cos_t + pltpu.roll(xp, 96, 1) * sin_lo + pltpu.roll(xp, 32, 1) * sin_hi


def _inproj_kernel(x_ref, g_ref, wm_ref, wt_ref, cos_ref, slo_ref, shi_ref, cw_ref, cb_ref,
                   cng_ref, cnb_ref, kng_ref, knb_ref,
                   u_ref, q_ref, k_ref, v_ref, qi_ref, tail_ref, ubuf_ref, ushift_ref):
    ts = x_ref.shape[0]
    x = x_ref[...]
    ms = jnp.mean(x * x, axis=-1, keepdims=True)
    hn = (x * lax.rsqrt(ms + NORM_EPS) * g_ref[...]).astype(jnp.bfloat16)
    proj = jnp.dot(hn, wm_ref[...], preferred_element_type=jnp.float32)
    tail = jnp.dot(hn, wt_ref[...], preferred_element_type=jnp.float32)

    u = proj[:, 0:C_CONV] * jax.nn.sigmoid(proj[:, C_CONV:2 * C_CONV])

    @pl.when(pl.program_id(0) == 0)
    def _():
        ubuf_ref[0:CONV_HALO, :] = jnp.zeros((CONV_HALO, C_CONV), jnp.float32)

    ubuf_ref[CONV_HALO:CONV_HALO + ts, :] = u
    span = ts + CONV_HALO - SUBLANES
    for b in range(1, SUBLANES):
        ushift_ref[b - 1] = ubuf_ref[b:b + span, :]
    acc = jnp.zeros((ts, C_CONV), jnp.float32) + cb_ref[...]
    for kk in range(CONV_WIDTH):
        off = CONV_HALO - (CONV_WIDTH - 1) + kk
        a, b = off // SUBLANES * SUBLANES, off % SUBLANES
        tap = ubuf_ref[a:a + ts, :] if b == 0 else ushift_ref[b - 1, a:a + ts, :]
        acc = acc + tap * cw_ref[kk:kk + 1, :]
    ubuf_ref[0:CONV_HALO, :] = ubuf_ref[ts:ts + CONV_HALO, :]
    mu = jnp.mean(acc, axis=-1, keepdims=True)
    d = acc - mu
    var = jnp.mean(d * d, axis=-1, keepdims=True)
    yn = d * lax.rsqrt(var + NORM_EPS) * cng_ref[...] + cnb_ref[...]
    u_ref[...] = (yn * jax.nn.sigmoid(yn)).astype(jnp.bfloat16)

    cos_t, sin_lo, sin_hi = cos_ref[...], slo_ref[...], shi_ref[...]
    base = 2 * C_CONV
    for p in range(D_ATTN // LANES):
        lo = p * LANES
        qp = proj[:, base + lo:base + lo + LANES]
        q_ref[:, lo:lo + LANES] = (_rope128(qp, cos_t, sin_lo, sin_hi) * Q_SCALE).astype(jnp.bfloat16)
        kp = proj[:, base + D_ATTN + lo:base + D_ATTN + lo + LANES]
        k_ref[:, lo:lo + LANES] = _rope128(kp, cos_t, sin_lo, sin_hi).astype(jnp.bfloat16)
        qip = proj[:, base + 3 * D_ATTN + lo:base + 3 * D_ATTN + lo + LANES]
        qi_ref[:, lo:lo + LANES] = _rope128(qip, cos_t, sin_lo, sin_hi).astype(jnp.bfloat16)
    v_ref[...] = proj[:, base + 2 * D_ATTN:base + 3 * D_ATTN].astype(jnp.bfloat16)

    lane = lax.broadcasted_iota(jnp.int32, tail.shape, 1)
    is_k = lane < IDX_DIM
    kmu = jnp.sum(jnp.where(is_k, tail, 0.0), axis=-1, keepdims=True) * (1.0 / IDX_DIM)
    kd = jnp.where(is_k, tail - kmu, 0.0)
    kvar = jnp.sum(kd * kd, axis=-1, keepdims=True) * (1.0 / IDX_DIM)
    kn = kd * lax.rsqrt(kvar + NORM_EPS) * kng_ref[...] + knb_ref[...]
    kr = _rope128(kn, cos_t, sin_lo, sin_hi)
    wi = tail * (IDX_HEADS ** -0.5 * IDX_DIM ** -0.5)
    tail_ref[...] = jnp.where(is_k, kr, jnp.where(lane < IDX_DIM + IDX_HEADS, wi, 0.0))


def _inproj(x, g, w_main, w_tail, cos_t, sin_lo, sin_hi, cw, cb, cng, cnb, kng, knb):
    s = x.shape[0]
    ts = min(ROW_TILE, s)
    row = lambda w: pl.BlockSpec((ts, w), lambda i: (i, 0))
    full = lambda a: pl.BlockSpec(a.shape, lambda i: (0,) * a.ndim)
    return pl.pallas_call(
        _inproj_kernel,
        grid=(s // ts,),
        in_specs=[row(D_MODEL), full(g), full(w_main), full(w_tail), row(LANES), row(LANES), row(LANES),
                  full(cw), full(cb), full(cng), full(cnb), full(kng), full(knb)],
        out_specs=[row(C_CONV), row(D_ATTN), row(D_ATTN), row(D_ATTN), row(D_ATTN), row(LANES)],
        out_shape=[jax.ShapeDtypeStruct((s, C_CONV), jnp.bfloat16),
                   jax.ShapeDtypeStruct((s, D_ATTN), jnp.bfloat16),
                   jax.ShapeDtypeStruct((s, D_ATTN), jnp.bfloat16),
                   jax.ShapeDtypeStruct((s, D_ATTN), jnp.bfloat16),
                   jax.ShapeDtypeStruct((s, D_ATTN), jnp.bfloat16),
                   jax.ShapeDtypeStruct((s, LANES), jnp.float32)],
        scratch_shapes=[pltpu.VMEM((ts + CONV_HALO, C_CONV), jnp.float32),
                        pltpu.VMEM((SUBLANES - 1, ts + CONV_HALO - SUBLANES, C_CONV), jnp.float32)],
        compiler_params=_cparams(("arbitrary",)),
        name="inproj",
    )(x, g, w_main, w_tail, cos_t, sin_lo, sin_hi, cw, cb, cng, cnb, kng, knb)


def _float_to_key(x):
    b = lax.bitcast_convert_type(x, jnp.int32)
    return jnp.where(b < 0, b ^ jnp.int32(0x7FFFFFFF), b)


def _key_to_float(k):
    b = jnp.where(k < 0, k ^ jnp.int32(0x7FFFFFFF), k)
    return lax.bitcast_convert_type(b, jnp.float32)


def _avg_floor(a, b):
    return (a >> 1) + (b >> 1) + (a & b & 1)


def _probit_upper(q):
    p = jnp.minimum(q, 1.0 - q)
    t = jnp.sqrt(-2.0 * jnp.log(p))
    z = t - (2.515517 + t * (0.802853 + t * 0.010328)) / (1.0 + t * (1.432788 + t * (0.189269 + t * 0.001308)))
    return jnp.where(q <= 0.5, z, -z)


def _indexer_kernel(nkb_ref, qi_ref, wi_ref, cq_ref, ki_ref, ck_ref, bias_ref, sc_ref, *, k_sel):
    qb = qi_ref.shape[0]
    s = ki_ref.shape[0]
    kc = min(KEY_CHUNK, s)
    grp = kc // SUBLANES
    n_total = s // kc
    nch = (nkb_ref[pl.program_id(0)] * Q_BLOCK + kc - 1) // kc
    f32, i32 = jnp.float32, jnp.int32
    neg_inf, pos_inf = f32(-jnp.inf), f32(jnp.inf)

    def as3(a):
        return a.reshape(grp, SUBLANES, qb)

    def rows8(v):
        return jnp.broadcast_to(v, (SUBLANES, qb))

    qi = qi_ref[...]
    q_pairs = [jnp.concatenate([qi[:, (2 * p) * IDX_DIM:(2 * p + 1) * IDX_DIM],
                                qi[:, (2 * p + 1) * IDX_DIM:(2 * p + 2) * IDX_DIM]], axis=0)
               for p in range(IDX_HEADS // 2)]
    w8 = [rows8(wi_ref[h:h + 1, :]) for h in range(IDX_HEADS)]
    cq8 = rows8(cq_ref[0:1, :])
    n_adm = cq_ref[1:2, :]

    def score_chunks(c, carry):
        rmax, rmin, nge0, ngt0 = carry
        starts = [pl.multiple_of((c * SCORE_UNROLL + r) * kc, kc) for r in range(SCORE_UNROLL)]
        sps = [[lax.dot_general(ki_ref[pl.ds(k0, kc), :], q_pairs[p], _NT, preferred_element_type=f32)
                for p in range(IDX_HEADS // 2)] for k0 in starts]
        for k0, sp in zip(starts, sps):
            acc = jnp.zeros((grp, SUBLANES, qb), f32)
            for p in range(IDX_HEADS // 2):
                acc = acc + w8[2 * p][None] * jnp.maximum(as3(sp[p][:, :qb]), 0.0)
                acc = acc + w8[2 * p + 1][None] * jnp.maximum(as3(sp[p][:, qb:]), 0.0)
            adm = as3(ck_ref[pl.ds(k0, kc), :]) <= cq8[None]
            val = jnp.where(adm, acc, neg_inf)
            sc_ref[pl.ds(k0, kc), :] = val.reshape(kc, qb)
            rmax = jnp.maximum(rmax, jnp.max(val, axis=0))
            rmin = jnp.minimum(rmin, jnp.min(acc, axis=0))
            nge0 = nge0 + jnp.sum(jnp.where(val >= 0.0, 1, 0), axis=0)
            ngt0 = ngt0 + jnp.sum(jnp.where(val > 0.0, 1, 0), axis=0)
        return rmax, rmin, nge0, ngt0

    assert n_total % SCORE_UNROLL == 0
    zero8 = jnp.zeros((SUBLANES, qb), i32)
    rmax8, rmin8, nge8, ngt8 = lax.fori_loop(
        0, (nch + SCORE_UNROLL - 1) // SCORE_UNROLL, score_chunks,
        (jnp.full((SUBLANES, qb), neg_inf, f32), jnp.full((SUBLANES, qb), pos_inf, f32), zero8, zero8))
    row_max = jnp.max(rmax8, axis=0, keepdims=True)
    row_min = jnp.min(rmin8, axis=0, keepdims=True)
    c_ge0 = jnp.sum(nge8, axis=0, keepdims=True)
    c_gt0 = jnp.sum(ngt8, axis=0, keepdims=True)

    def count_rows(pred):
        def body(c, acc):
            k0 = pl.multiple_of(c * kc, kc)
            m = pred(as3(sc_ref[pl.ds(k0, kc), :]), k0)
            return acc + jnp.sum(jnp.where(m, 1, 0), axis=0)
        acc = lax.fori_loop(0, nch, body, jnp.zeros((SUBLANES, qb), i32))
        return jnp.sum(acc, axis=0, keepdims=True)

    def count_ge(cand):
        c8 = rows8(cand)[None]
        return count_rows(lambda x3, k0: x3 >= c8)

    k_eff = jnp.minimum(n_adm, k_sel)
    inv_n = 1.0 / (n_adm.astype(f32) + 1.0)

    def zscore(cnt):
        return _probit_upper((cnt.astype(f32) + 0.5) * inv_n)

    z_target = zscore(k_eff)

    above = c_gt0 >= k_eff
    below = c_ge0 < k_eff
    zero_key = jnp.zeros((1, qb), i32)
    lo0 = jnp.where(below, _float_to_key(row_min), zero_key)
    clo0 = jnp.where(below, n_adm, c_ge0)
    hi0 = jnp.where(above, _float_to_key(row_max) + 1, jnp.where(below, zero_key - 1, zero_key + 1))
    chi0 = jnp.where(above, 0, jnp.where(below, c_ge0, c_gt0))

    def settled(lo, hi, clo):
        return jnp.logical_or(clo == k_eff, _avg_floor(lo, hi) == lo)

    def cond(st):
        return st[7] > 0

    def step(st):
        lo, hi, clo, chi, z_lo, z_hi, done, _, phase = st
        lo_f, hi_f = _key_to_float(lo), _key_to_float(hi)
        frac = jnp.clip((z_target - z_lo) / (z_hi - z_lo), 0.02, 0.98)
        guess = lo_f + (hi_f - lo_f) * frac
        guess_ok = jnp.logical_and(guess == guess, jnp.abs(guess) < pos_inf)
        mid = jnp.maximum(_avg_floor(lo, hi), lo + 1)
        cand = jnp.clip(_float_to_key(jnp.where(guess_ok, guess, lo_f)), lo + 1, hi - 1)
        cand = jnp.where(jnp.logical_or(phase == 2, jnp.logical_not(guess_ok)), mid, cand)
        cnt = count_ge(_key_to_float(cand))
        z_c = zscore(cnt)
        ge = cnt >= k_eff
        up = jnp.logical_and(done == 0, ge)
        dn = jnp.logical_and(done == 0, jnp.logical_not(ge))
        lo = jnp.where(up, cand, lo)
        clo = jnp.where(up, cnt, clo)
        z_lo = jnp.where(up, z_c, z_lo)
        hi = jnp.where(dn, cand, hi)
        chi = jnp.where(dn, cnt, chi)
        z_hi = jnp.where(dn, z_c, z_hi)
        done = jnp.where(settled(lo, hi, clo), 1, done)
        return lo, hi, clo, chi, z_lo, z_hi, done, jnp.sum(1 - done), jnp.where(phase == 2, 0, phase + 1)

    done0 = jnp.where(settled(lo0, hi0, clo0), 1, 0)
    st0 = (lo0, hi0, clo0, chi0, zscore(clo0), zscore(chi0), done0, jnp.sum(1 - done0), i32(0))
    lo, hi, clo, chi = lax.while_loop(cond, step, st0)[:4]
    thr = _key_to_float(lo)
    thr8 = rows8(thr)[None]

    tie = clo > k_eff
    need = k_eff - chi
    n_tie = jnp.sum(jnp.where(tie, 1, 0))
    key_iota = (lax.broadcasted_iota(i32, (grp, SUBLANES, qb), 0) * SUBLANES
                + lax.broadcasted_iota(i32, (grp, SUBLANES, qb), 1))

    def store_bias(k0, sel):
        bias_ref[0, pl.ds(k0, kc), :] = jnp.where(sel, 0.0, neg_inf).reshape(kc, qb).astype(jnp.bfloat16)

    @pl.when(n_tie == 0)
    def _():
        def emit_chunk(c, carry):
            k0 = pl.multiple_of(c * kc, kc)
            store_bias(k0, as3(sc_ref[pl.ds(k0, kc), :]) >= thr8)
            return carry
        lax.fori_loop(0, nch, emit_chunk, 0)

    @pl.when(n_tie > 0)
    def _():
        def cnt_upto(m):
            m8 = rows8(m)[None]
            return count_rows(lambda x3, k0: jnp.logical_and(x3 == thr8, key_iota + k0 <= m8))

        def body(_, jj):
            jl, jh = jj
            m = (jl + jh) >> 1
            ok = cnt_upto(m) >= need
            return jnp.where(ok, jl, m), jnp.where(ok, m, jh)

        n_it = max(1, (s - 1).bit_length()) + 1
        _, jh = lax.fori_loop(0, n_it, body, (jnp.full((1, qb), -1, i32), jnp.full((1, qb), s - 1, i32)))
        jlim8 = rows8(jnp.where(tie, jh, s))[None]

        def emit_chunk(c, carry):
            k0 = pl.multiple_of(c * kc, kc)
            x3 = as3(sc_ref[pl.ds(k0, kc), :])
            store_bias(k0, jnp.logical_or(x3 > thr8, jnp.logical_and(x3 == thr8, key_iota + k0 <= jlim8)))
            return carry
        lax.fori_loop(0, nch, emit_chunk, 0)

    def fill_chunk(c, carry):
        k0 = pl.multiple_of(c * kc, kc)
        bias_ref[0, pl.ds(k0, kc), :] = jnp.full((kc, qb), neg_inf, jnp.bfloat16)
        return carry

    lax.fori_loop(nch, n_total, fill_chunk, 0)


def _indexer(nkb, qi, wi_t, cq_row, ki, ck_lanes, k_sel):
    s = qi.shape[0]
    qb = min(Q_BLOCK, s)
    grid_spec = pltpu.PrefetchScalarGridSpec(
        num_scalar_prefetch=1,
        grid=(s // qb,),
        in_specs=[pl.BlockSpec((qb, D_ATTN), lambda i, n: (i, 0)),
                  pl.BlockSpec((IDX_HEADS, qb), lambda i, n: (0, i)),
                  pl.BlockSpec((2, qb), lambda i, n: (0, i)),
                  pl.BlockSpec((s, IDX_DIM), lambda i, n: (0, 0)),
                  pl.BlockSpec((s, qb), lambda i, n: (0, 0))],
        out_specs=pl.BlockSpec((1, s, qb), lambda i, n: (i, 0, 0)),
        scratch_shapes=[pltpu.VMEM((s, qb), jnp.float32)],
    )
    return pl.pallas_call(
        functools.partial(_indexer_kernel, k_sel=k_sel),
        grid_spec=grid_spec,
        out_shape=jax.ShapeDtypeStruct((s // qb, s, qb), jnp.bfloat16),
        compiler_params=_cparams(("arbitrary",)),
        name="indexer",
    )(nkb, qi, wi_t, cq_row, ki, ck_lanes)


def _attention_kernel(nkt_ref, q_ref, k_ref, vt_ref, b_ref, o_ref, qm_ref, m_ref, acc_ref):
    i, j = pl.program_id(0), pl.program_id(1)
    tq = q_ref.shape[0]
    n_pairs = D_ATTN // LANES
    bf16 = jnp.bfloat16

    @pl.when(j == 0)
    def _():
        q = q_ref[...]
        lane = lax.broadcasted_iota(jnp.int32, (tq, LANES), 1)
        zero = jnp.zeros((tq, LANES), q.dtype)
        for p in range(n_pairs):
            qp = q[:, p * LANES:(p + 1) * LANES]
            qm_ref[2 * p] = jnp.where(lane < HEAD_DIM, qp, zero)
            qm_ref[2 * p + 1] = jnp.where(lane < HEAD_DIM, zero, qp)
        m_ref[...] = jnp.full(m_ref.shape, -1e30, jnp.float32)
        acc_ref[...] = jnp.zeros(acc_ref.shape, jnp.float32)

    @pl.when(j < nkt_ref[i])
    def _():
        bias = jnp.concatenate([b_ref[r] for r in range(b_ref.shape[0])], axis=1)
        scores = []
        for h in range(N_HEADS):
            kp = k_ref[:, (h // 2) * LANES:(h // 2 + 1) * LANES]
            st = lax.dot_general(kp, qm_ref[h], _NT, preferred_element_type=jnp.float32)
            scores.append(st.astype(bf16) + bias)
        for h in range(N_HEADS):
            p = h // 2
            vtp = vt_ref[p * ATT_V_ROWS:(p + 1) * ATT_V_ROWS, :]
            st = scores[h]
            m_prev = m_ref[h]
            m_new = jnp.maximum(m_prev, jnp.max(st, axis=0, keepdims=True).astype(jnp.float32))
            alpha = jnp.exp2(m_prev - m_new)
            pt = jnp.exp2(st - m_new.astype(bf16))
            acc_ref[h] = alpha * acc_ref[h] + jnp.dot(vtp, pt, preferred_element_type=jnp.float32)
            m_ref[h] = m_new

    @pl.when(j == nkt_ref[i] - 1)
    def _():
        row = lax.broadcasted_iota(jnp.int32, (LANES, tq), 0)
        for p in range(n_pairs):
            a0, a1 = acc_ref[2 * p], acc_ref[2 * p + 1]
            o0 = a0[:LANES] / a0[LANES:LANES + 1]
            o1 = a1[:LANES] / a1[LANES:LANES + 1]
            o_ref[:, p * LANES:(p + 1) * LANES] = jnp.where(row < HEAD_DIM, o0, o1).T.astype(o_ref.dtype)


def _attention(nkt, q, k, vt_aug, bias3):
    s = q.shape[0]
    tq, tk = min(ATT_TQ, s), min(ATT_TK, s)
    qb = bias3.shape[2]
    last = lambda i, j, n: jnp.minimum(j, n[i] - 1)
    grid_spec = pltpu.PrefetchScalarGridSpec(
        num_scalar_prefetch=1,
        grid=(s // tq, s // tk),
        in_specs=[pl.BlockSpec((tq, D_ATTN), lambda i, j, n: (i, 0)),
                  pl.BlockSpec((tk, D_ATTN), lambda i, j, n: (last(i, j, n), 0)),
                  pl.BlockSpec((vt_aug.shape[0], tk), lambda i, j, n: (0, last(i, j, n))),
                  pl.BlockSpec((tq // qb, tk, qb), lambda i, j, n: (i, last(i, j, n), 0))],
        out_specs=pl.BlockSpec((tq, D_ATTN), lambda i, j, n: (i, 0)),
        scratch_shapes=[pltpu.VMEM((N_HEADS, tq, LANES), jnp.bfloat16),
                        pltpu.VMEM((N_HEADS, 1, tq), jnp.float32),
                        pltpu.VMEM((N_HEADS, ATT_V_ROWS, tq), jnp.float32)],
    )
    return pl.pallas_call(
        _attention_kernel,
        grid_spec=grid_spec,
        out_shape=jax.ShapeDtypeStruct((s, D_ATTN), jnp.bfloat16),
        compiler_params=_cparams(("arbitrary", "arbitrary")),
        name="attention",
    )(nkt, q, k, vt_aug, bias3)


def _outproj_kernel(x_ref, u_ref, a_ref, wu_ref, wa_ref, g_ref, wr_ref, br_ref,
                    h_ref, xn_ref, eidx_ref, gate_ref, rank_ref, cnt_ref, carry_ref):
    ts = x_ref.shape[0]

    @pl.when(pl.program_id(0) == 0)
    def _():
        carry_ref[...] = jnp.zeros(carry_ref.shape, jnp.float32)

    h = (x_ref[...]
         + jnp.dot(u_ref[...], wu_ref[...], preferred_element_type=jnp.float32)
         + jnp.dot(a_ref[...], wa_ref[...], preferred_element_type=jnp.float32))
    h_ref[...] = h
    ms = jnp.mean(h * h, axis=-1, keepdims=True)
    xn = h * lax.rsqrt(ms + NORM_EPS) * g_ref[...]
    xn_ref[...] = xn
    logits_t = lax.dot_general(wr_ref[...], xn, _NT, preferred_element_type=jnp.float32,
                               precision=lax.Precision.HIGHEST) + br_ref[...]
    erow = lax.broadcasted_iota(jnp.int32, (N_EXPERTS, ts), 0)
    work = logits_t
    vals, idxs = [], []
    multi = jnp.zeros((N_EXPERTS, ts), jnp.float32)
    for _ in range(TOP_K_EXPERTS):
        mx = jnp.max(work, axis=0, keepdims=True)
        ix = jnp.min(jnp.where(work == mx, erow, N_EXPERTS), axis=0, keepdims=True)
        hit = erow == ix
        multi = jnp.where(hit, 1.0, multi)
        work = jnp.where(hit, -jnp.inf, work)
        vals.append(mx)
        idxs.append(ix)
    ex = [jnp.exp(v - vals[0]) for v in vals]
    den = ex[0] + ex[1] + ex[2] + ex[3]

    r = lax.broadcasted_iota(jnp.int32, (ts, ts), 0)
    c = lax.broadcasted_iota(jnp.int32, (ts, ts), 1)
    earlier = jnp.where(r < c, 1.0, 0.0).astype(jnp.bfloat16)
    prior = jnp.dot(multi.astype(jnp.bfloat16), earlier, preferred_element_type=jnp.float32) + carry_ref[...]
    row8 = lax.broadcasted_iota(jnp.int32, (SUBLANES, ts), 0)
    eidx = jnp.zeros((SUBLANES, ts), jnp.int32)
    gate = jnp.zeros((SUBLANES, ts), jnp.float32)
    rank = jnp.zeros((SUBLANES, ts), jnp.int32)
    for kk in range(TOP_K_EXPERTS):
        rk = jnp.sum(jnp.where(erow == idxs[kk], prior, 0.0), axis=0, keepdims=True)
        eidx = jnp.where(row8 == kk, idxs[kk], eidx)
        gate = jnp.where(row8 == kk, ex[kk] / den, gate)
        rank = jnp.where(row8 == kk, rk.astype(jnp.int32), rank)
    eidx_ref[...] = eidx
    gate_ref[...] = gate
    rank_ref[...] = rank
    carry_ref[...] = carry_ref[...] + jnp.sum(multi, axis=1, keepdims=True)
    cnt_ref[...] = jnp.broadcast_to(carry_ref[...], cnt_ref.shape)


def _outproj(x, u, a, w_u, w_a, g, w_r, b_r):
    s = x.shape[0]
    ts = min(ROW_TILE, s)
    row = lambda w: pl.BlockSpec((ts, w), lambda i: (i, 0))
    col = lambda: pl.BlockSpec((SUBLANES, ts), lambda i: (0, i))
    full = lambda arr: pl.BlockSpec(arr.shape, lambda i: (0,) * arr.ndim)
    return pl.pallas_call(
        _outproj_kernel,
        grid=(s // ts,),
        in_specs=[row(D_MODEL), row(C_CONV), row(D_ATTN), full(w_u), full(w_a), full(g), full(w_r), full(b_r)],
        out_specs=[row(D_MODEL), row(D_MODEL), col(), col(), col(),
                   pl.BlockSpec((N_EXPERTS, LANES), lambda i: (0, 0))],
        out_shape=[jax.ShapeDtypeStruct((s, D_MODEL), jnp.float32),
                   jax.ShapeDtypeStruct((s, D_MODEL), jnp.float32),
                   jax.ShapeDtypeStruct((SUBLANES, s), jnp.int32),
                   jax.ShapeDtypeStruct((SUBLANES, s), jnp.float32),
                   jax.ShapeDtypeStruct((SUBLANES, s), jnp.int32),
                   jax.ShapeDtypeStruct((N_EXPERTS, LANES), jnp.float32)],
        scratch_shapes=[pltpu.VMEM((N_EXPERTS, 1), jnp.float32)],
        compiler_params=_cparams(("arbitrary",)),
        name="outproj_router",
    )(x, u, a, w_u, w_a, g, w_r, b_r)


def _row_copy(src_ref, dst_ref, sem, src_row, dst_row):
    return pltpu.make_async_copy(src_ref.at[pl.ds(src_row, 1), :], dst_ref.at[pl.ds(dst_row, 1), :], sem)


def _dispatch_kernel(dest_ref, xn_ref, xs_in_ref, xs_ref, sem):
    del xs_in_ref
    tt = xn_ref.shape[0]

    def start(r, carry):
        for kk in range(TOP_K_EXPERTS):
            _row_copy(xn_ref, xs_ref, sem, r, dest_ref[0, 0, r * TOP_K_EXPERTS + kk]).start(priority=kk % 2)
        return carry

    def wait(r, carry):
        for kk in range(TOP_K_EXPERTS):
            _row_copy(xn_ref, xs_ref, sem, r, dest_ref[0, 0, r * TOP_K_EXPERTS + kk]).wait()
        return carry

    lax.fori_loop(0, tt, start, 0, unroll=ROUTE_UNROLL)
    lax.fori_loop(0, tt, wait, 0, unroll=ROUTE_UNROLL)


def _dispatch(dest3, xn, xs_init):
    s = xn.shape[0]
    tt = min(ROUTE_TILE, s)
    return pl.pallas_call(
        _dispatch_kernel,
        grid=(s // tt,),
        in_specs=[pl.BlockSpec((1, 1, tt * TOP_K_EXPERTS), lambda i: (i, 0, 0), memory_space=pltpu.SMEM),
                  pl.BlockSpec((tt, D_MODEL), lambda i: (i, 0)),
                  pl.BlockSpec(memory_space=pl.ANY)],
        out_specs=pl.BlockSpec(memory_space=pl.ANY),
        out_shape=jax.ShapeDtypeStruct(xs_init.shape, xs_init.dtype),
        scratch_shapes=[pltpu.SemaphoreType.DMA(())],
        input_output_aliases={2: 0},
        compiler_params=_cparams(("arbitrary",)),
        name="dispatch",
    )(dest3, xn, xs_init)


def _expert_weight_copies(wgu_hbm, wdn_hbm, wgu_buf, wdn_buf, sem, expert, slot):
    return (pltpu.make_async_copy(wgu_hbm.at[expert], wgu_buf.at[slot], sem.at[0, slot]),
            pltpu.make_async_copy(wdn_hbm.at[expert], wdn_buf.at[slot], sem.at[1, slot]))


def _experts_kernel(be_ref, nb_ref, run_ref, nxt_ref, xs_ref, wgu_hbm, bgu_ref, wdn_hbm, bdn_ref, ys_ref,
                    wgu_buf, wdn_buf, wgu_bf, wdn_bf, sem):
    b = pl.program_id(0)
    active = b < nb_ref[0]
    fresh = jnp.logical_or(b == 0, be_ref[b] != be_ref[jnp.maximum(b - 1, 0)])
    slot = run_ref[b] % 2
    copies = functools.partial(_expert_weight_copies, wgu_hbm, wdn_hbm, wgu_buf, wdn_buf, sem)

    @pl.when(jnp.logical_and(active, b == 0))
    def _():
        for cp in copies(be_ref[b], slot):
            cp.start()

    @pl.when(jnp.logical_and(active, fresh))
    def _():
        for cp in copies(be_ref[b], slot):
            cp.wait()

        @pl.when(nxt_ref[b] >= 0)
        def _():
            for cp in copies(nxt_ref[b], 1 - slot):
                cp.start()

        wgu_bf[...] = wgu_buf[slot].astype(jnp.bfloat16)
        wdn_bf[...] = wdn_buf[slot].astype(jnp.bfloat16)

    @pl.when(active)
    def _():
        half = xs_ref.shape[0] // 2
        gus = [jnp.dot(xs_ref[r * half:(r + 1) * half, :].astype(jnp.bfloat16), wgu_bf[...],
                       preferred_element_type=jnp.float32) + bgu_ref[0] for r in range(2)]
        for r in range(2):
            g = jnp.minimum(gus[r][:, :D_EXPERT], SWIGLU_LIMIT)
            u = jnp.clip(gus[r][:, D_EXPERT:], -SWIGLU_LIMIT, SWIGLU_LIMIT)
            hdn = g * jax.nn.sigmoid(SWIGLU_ALPHA * g) * (u + 1.0)
            ys_ref[r * half:(r + 1) * half, :] = jnp.dot(hdn.astype(jnp.bfloat16), wdn_bf[...],
                                                        preferred_element_type=jnp.float32) + bdn_ref[0]

    @pl.when(jnp.logical_not(active))
    def _():
        ys_ref[...] = jnp.zeros(ys_ref.shape, ys_ref.dtype)


def _experts(block_expert, n_used, run_id, next_expert, xs, w_gu, b_gu, w_dn, b_dn):
    n_rows = xs.shape[0]
    blk = EXPERT_BLOCK
    row_map = lambda b, be, nb, run, nxt: (jnp.maximum(jnp.minimum(b, nb[0] - 1), 0), 0)
    exp_map = lambda b, be, nb, run, nxt: (be[b], 0, 0)
    grid_spec = pltpu.PrefetchScalarGridSpec(
        num_scalar_prefetch=4,
        grid=(n_rows // blk,),
        in_specs=[pl.BlockSpec((blk, D_MODEL), row_map),
                  pl.BlockSpec(memory_space=pl.ANY),
                  pl.BlockSpec((1, 1, 2 * D_EXPERT), exp_map),
                  pl.BlockSpec(memory_space=pl.ANY),
                  pl.BlockSpec((1, 1, D_MODEL), exp_map)],
        out_specs=pl.BlockSpec((blk, D_MODEL), lambda b, be, nb, run, nxt: (b, 0)),
        scratch_shapes=[pltpu.VMEM((2, D_MODEL, 2 * D_EXPERT), jnp.float32),
                        pltpu.VMEM((2, D_EXPERT, D_MODEL), jnp.float32),
                        pltpu.VMEM((D_MODEL, 2 * D_EXPERT), jnp.bfloat16),
                        pltpu.VMEM((D_EXPERT, D_MODEL), jnp.bfloat16),
                        pltpu.SemaphoreType.DMA((2, 2))],
    )
    return pl.pallas_call(
        _experts_kernel,
        grid_spec=grid_spec,
        out_shape=jax.ShapeDtypeStruct((n_rows, D_MODEL), jnp.float32),
        compiler_params=_cparams(("arbitrary",)),
        name="experts",
    )(block_expert, n_used, run_id, next_expert, xs, w_gu, b_gu, w_dn, b_dn)


def _combine_kernel(dest_ref, h_ref, gate_ref, g_ref, ys_ref, o_ref, buf_ref, sem):
    tt = h_ref.shape[0]

    def start(r, carry):
        for kk in range(TOP_K_EXPERTS):
            _row_copy(ys_ref, buf_ref.at[kk], sem, dest_ref[0, 0, r * TOP_K_EXPERTS + kk], r).start(priority=kk % 2)
        return carry

    def wait(r, carry):
        for kk in range(TOP_K_EXPERTS):
            _row_copy(ys_ref, buf_ref.at[kk], sem, dest_ref[0, 0, r * TOP_K_EXPERTS + kk], r).wait()
        return carry

    lax.fori_loop(0, tt, start, 0, unroll=ROUTE_UNROLL)
    lax.fori_loop(0, tt, wait, 0, unroll=ROUTE_UNROLL)
    gate = gate_ref[...]
    h = h_ref[...]
    for kk in range(TOP_K_EXPERTS):
        h = h + gate[:, kk:kk + 1] * buf_ref[kk]
    ms = jnp.mean(h * h, axis=-1, keepdims=True)
    o_ref[...] = h * lax.rsqrt(ms + NORM_EPS) * g_ref[...]


def _combine(dest3, h, gate, g_final, ys):
    s = h.shape[0]
    tt = min(ROUTE_TILE, s)
    return pl.pallas_call(
        _combine_kernel,
        grid=(s // tt,),
        in_specs=[pl.BlockSpec((1, 1, tt * TOP_K_EXPERTS), lambda i: (i, 0, 0), memory_space=pltpu.SMEM),
                  pl.BlockSpec((tt, D_MODEL), lambda i: (i, 0)),
                  pl.BlockSpec((tt, TOP_K_EXPERTS), lambda i: (i, 0)),
                  pl.BlockSpec((1, D_MODEL), lambda i: (0, 0)),
                  pl.BlockSpec(memory_space=pl.ANY)],
        out_specs=pl.BlockSpec((tt, D_MODEL), lambda i: (i, 0)),
        out_shape=jax.ShapeDtypeStruct((s, D_MODEL), jnp.float32),
        scratch_shapes=[pltpu.VMEM((TOP_K_EXPERTS, tt, D_MODEL), jnp.float32),
                        pltpu.SemaphoreType.DMA(())],
        compiler_params=_cparams(("arbitrary",)),
        name="combine",
    )(dest3, h, gate, g_final, ys)


def _rope_tables(pos):
    half = HEAD_DIM // 2
    inv = ROPE_THETA ** (-jnp.arange(half, dtype=jnp.float32) / half)
    ang = pos.astype(jnp.float32)[:, None] * inv
    cos, sin = jnp.cos(ang), jnp.sin(ang)
    zero = jnp.zeros_like(sin)
    reps = LANES // HEAD_DIM
    cos_t = jnp.tile(jnp.concatenate([cos, cos], axis=-1), (1, reps))
    sin_lo = jnp.tile(jnp.concatenate([-sin, zero], axis=-1), (1, reps))
    sin_hi = jnp.tile(jnp.concatenate([zero, sin], axis=-1), (1, reps))
    return cos_t, sin_lo, sin_hi


def _block_bounds(chunk, q_rows, k_rows):
    cq_max = jnp.max(chunk.reshape(-1, q_rows), axis=1)
    ck_min = jnp.min(chunk.reshape(-1, k_rows), axis=1)
    need = ck_min[None, :] <= cq_max[:, None]
    last = jnp.max(jnp.where(need, jnp.arange(ck_min.shape[0], dtype=jnp.int32)[None, :] + 1, 1), axis=1)
    return last.astype(jnp.int32)


def _layer(h, pos, norm_mix_g, w_in, idx_k_norm_g, idx_k_norm_b, conv_w, conv_b, conv_norm_g, conv_norm_b,
           w_out, norm_ffn_g, w_router, b_router, w_gate_up, b_gate_up, w_down, b_down, out_gain):
    s = h.shape[0]
    f32, bf16 = jnp.float32, jnp.bfloat16
    n_main = 2 * C_CONV + 3 * D_ATTN + IDX_HEADS * IDX_DIM
    w_main = w_in[:, :n_main].astype(bf16)
    w_tail = jnp.pad(w_in[:, n_main:], ((0, 0), (0, LANES - (IDX_DIM + IDX_HEADS)))).astype(bf16)
    cos_t, sin_lo, sin_hi = _rope_tables(pos)
    cw = jnp.pad(conv_w, ((0, CONV_HALO - CONV_WIDTH), (0, 0)))
    kng = jnp.pad(idx_k_norm_g, (0, LANES - IDX_DIM))[None, :]
    knb = jnp.pad(idx_k_norm_b, (0, LANES - IDX_DIM))[None, :]
    u, q, k, v, qi, tail = _inproj(h, norm_mix_g[None, :], w_main, w_tail, cos_t, sin_lo, sin_hi, cw,
                                   conv_b[None, :], conv_norm_g[None, :], conv_norm_b[None, :], kng, knb)

    chunk = pos // CHUNK
    k_sel = min(TOPK_KEYS_MAX, s // 4)
    qb = min(Q_BLOCK, s)
    ki = tail[:, :IDX_DIM].astype(bf16)
    wi_t = tail[:, IDX_DIM:IDX_DIM + IDX_HEADS].T
    n_adm = jnp.searchsorted(jnp.sort(chunk), chunk, side='right').astype(jnp.int32)
    bias3 = _indexer(_block_bounds(chunk, qb, qb), qi, wi_t, jnp.stack([chunk, n_adm]), ki,
                     jnp.broadcast_to(chunk[:, None], (s, qb)), k_sel)
    vt_aug = jnp.concatenate(
        [v.T.reshape(D_ATTN // LANES, LANES, s), jnp.ones((D_ATTN // LANES, ATT_V_ROWS - LANES, s), bf16)],
        axis=1).reshape(-1, s)
    a = _attention(_block_bounds(chunk, min(ATT_TQ, s), min(ATT_TK, s)), q, k, vt_aug, bias3)

    h1, xn, eidx_t, gate_t, rank_t, counts = _outproj(
        h, u, a, w_out[:C_CONV].astype(bf16), w_out[C_CONV:].astype(bf16), norm_ffn_g[None, :],
        w_router.T, b_router[:, None])
    eidx, rank = eidx_t[:TOP_K_EXPERTS].T, rank_t[:TOP_K_EXPERTS].T
    gate = gate_t[:TOP_K_EXPERTS].T

    blk = EXPERT_BLOCK
    cnt = counts[:, 0].astype(jnp.int32)
    padded = (cnt + blk - 1) // blk * blk
    end = jnp.cumsum(padded)
    start = end - padded
    n_blocks = s * TOP_K_EXPERTS // blk + N_EXPERTS
    dest = start[eidx] + rank
    tt = min(ROUTE_TILE, s)
    dest3 = dest.reshape(s // tt, 1, tt * TOP_K_EXPERTS)
    block_row = jnp.arange(n_blocks, dtype=jnp.int32) * blk
    block_expert = jnp.minimum(jnp.sum(end[None, :] <= block_row[:, None], axis=1), N_EXPERTS - 1).astype(jnp.int32)
    n_used = (end[-1:] // blk).astype(jnp.int32)

    xs = _dispatch(dest3, xn, jnp.zeros((n_blocks * blk, D_MODEL), f32))
    expert_ids = jnp.arange(N_EXPERTS, dtype=jnp.int32)
    has_rows = cnt > 0
    run_of_expert = jnp.cumsum(has_rows.astype(jnp.int32)) - 1
    later = jnp.logical_and(expert_ids[None, :] > expert_ids[:, None], has_rows[None, :])
    next_of_expert = jnp.where(jnp.any(later, axis=1), jnp.argmax(later, axis=1), -1).astype(jnp.int32)
    ys = _experts(block_expert, n_used, run_of_expert[block_expert], next_of_expert[block_expert], xs,
                  w_gate_up, b_gate_up[:, None, :], w_down, b_down[:, None, :])
    return _combine(dest3, h1, gate, out_gain[None, :], ys)


def kernel(x, positions, norm_mix_g, w_in, idx_k_norm_g, idx_k_norm_b, conv_w, conv_b, conv_norm_g, conv_norm_b,
           w_out, norm_ffn_g, w_router, b_router, w_gate_up, b_gate_up, w_down, b_down, norm_final_g):
    assert x.shape[0] == 1 and norm_mix_g.shape[0] == 1, "single sequence, single layer"
    y = _layer(x[0], positions[0], norm_mix_g[0], w_in[0], idx_k_norm_g[0], idx_k_norm_b[0], conv_w[0],
               conv_b[0], conv_norm_g[0], conv_norm_b[0], w_out[0], norm_ffn_g[0], w_router[0], b_router[0],
               w_gate_up[0], b_gate_up[0], w_down[0], b_down[0], norm_final_g)
    return y[None]
```

```python
import functools
import math

import jax
import jax.numpy as jnp
from jax import lax
from jax.experimental import pallas as pl
from jax.experimental.pallas import tpu as pltpu

D_MODEL = 1024
CHUNK = 64
C_CONV = 512
CONV_WIDTH = 31
N_HEADS = 8
HEAD_DIM = 64
D_ATTN = N_HEADS * HEAD_DIM
IDX_HEADS = 8
IDX_DIM = 64
TOPK_KEYS_MAX = 256
ROPE_THETA = 10000.0
N_EXPERTS = 32
TOP_K_EXPERTS = 4
D_EXPERT = 1024
SWIGLU_LIMIT = 7.0
SWIGLU_ALPHA = 1.702
NORM_EPS = 1e-5

LANES = 128
SUBLANES = 8
CONV_HALO = 32
ROW_TILE = 256
Q_BLOCK = 128
KEY_CHUNK = 512
SCORE_UNROLL = 2
ATT_TQ = 512
ATT_TK = 512
ATT_V_ROWS = LANES + 16
Q_SCALE = HEAD_DIM ** -0.5 * math.log2(math.e)
EXPERT_BLOCK = 256
ROUTE_TILE = 256
ROUTE_UNROLL = 4
VMEM_LIMIT = 56 * 1024 * 1024

_NT = (((1,), (1,)), ((), ()))


def _cparams(sem):
    return pltpu.CompilerParams(dimension_semantics=sem, vmem_limit_bytes=VMEM_LIMIT)


def _rope128(xp, cos_t, sin_lo, sin_hi):
    return xp * cos_t + pltpu.roll(xp, 96, 1) * sin_lo + pltpu.roll(xp, 32, 1) * sin_hi


def _inproj_kernel(x_ref, g_ref, wm_ref, wt_ref, cos_ref, slo_ref, shi_ref, cw_ref, cb_ref,
                   cng_ref, cnb_ref, kng_ref, knb_ref,
                   u_ref, q_ref, k_ref, v_ref, qi_ref, tail_ref, ubuf_ref, ushift_ref):
    ts = x_ref.shape[0]
    x = x_ref[...]
    ms = jnp.mean(x * x, axis=-1, keepdims=True)
    hn = (x * lax.rsqrt(ms + NORM_EPS) * g_ref[...]).astype(jnp.bfloat16)
    proj = jnp.dot(hn, wm_ref[...], preferred_element_type=jnp.float32)
    tail = jnp.dot(hn, wt_ref[...], preferred_element_type=jnp.float32)

    u = proj[:, 0:C_CONV] * jax.nn.sigmoid(proj[:, C_CONV:2 * C_CONV])

    @pl.when(pl.program_id(0) == 0)
    def _():
        ubuf_ref[0:CONV_HALO, :] = jnp.zeros((CONV_HALO, C_CONV), jnp.float32)

    ubuf_ref[CONV_HALO:CONV_HALO + ts, :] = u
    span = ts + CONV_HALO - SUBLANES
    for b in range(1, SUBLANES):
        ushift_ref[b - 1] = ubuf_ref[b:b + span, :]
    acc = jnp.zeros((ts, C_CONV), jnp.float32) + cb_ref[...]
    for kk in range(CONV_WIDTH):
        off = CONV_HALO - (CONV_WIDTH - 1) + kk
        a, b = off // SUBLANES * SUBLANES, off % SUBLANES
        tap = ubuf_ref[a:a + ts, :] if b == 0 else ushift_ref[b - 1, a:a + ts, :]
        acc = acc + tap * cw_ref[kk:kk + 1, :]
    ubuf_ref[0:CONV_HALO, :] = ubuf_ref[ts:ts + CONV_HALO, :]
    mu = jnp.mean(acc, axis=-1, keepdims=True)
    d = acc - mu
    var = jnp.mean(d * d, axis=-1, keepdims=True)
    yn = d * lax.rsqrt(var + NORM_EPS) * cng_ref[...] + cnb_ref[...]
    u_ref[...] = (yn * jax.nn.sigmoid(yn)).astype(jnp.bfloat16)

    cos_t, sin_lo, sin_hi = cos_ref[...], slo_ref[...], shi_ref[...]
    base = 2 * C_CONV
    for p in range(D_ATTN // LANES):
        lo = p * LANES
        qp = proj[:, base + lo:base + lo + LANES]
        q_ref[:, lo:lo + LANES] = (_rope128(qp, cos_t, sin_lo, sin_hi) * Q_SCALE).astype(jnp.bfloat16)
        kp = proj[:, base + D_ATTN + lo:base + D_ATTN + lo + LANES]
        k_ref[:, lo:lo + LANES] = _rope128(kp, cos_t, sin_lo, sin_hi).astype(jnp.bfloat16)
        qip = proj[:, base + 3 * D_ATTN + lo:base + 3 * D_ATTN + lo + LANES]
        qi_ref[:, lo:lo + LANES] = _rope128(qip, cos_t, sin_lo, sin_hi).astype(jnp.bfloat16)
    v_ref[...] = proj[:, base + 2 * D_ATTN:base + 3 * D_ATTN].astype(jnp.bfloat16)

    lane = lax.broadcasted_iota(jnp.int32, tail.shape, 1)
    is_k = lane < IDX_DIM
    kmu = jnp.sum(jnp.where(is_k, tail, 0.0), axis=-1, keepdims=True) * (1.0 / IDX_DIM)
    kd = jnp.where(is_k, tail - kmu, 0.0)
    kvar = jnp.sum(kd * kd, axis=-1, keepdims=True) * (1.0 / IDX_DIM)
    kn = kd * lax.rsqrt(kvar + NORM_EPS) * kng_ref[...] + knb_ref[...]
    kr = _rope128(kn, cos_t, sin_lo, sin_hi)
    wi = tail * (IDX_HEADS ** -0.5 * IDX_DIM ** -0.5)
    tail_ref[...] = jnp.where(is_k, kr, jnp.where(lane < IDX_DIM + IDX_HEADS, wi, 0.0))


def _inproj(x, g, w_main, w_tail, cos_t, sin_lo, sin_hi, cw, cb, cng, cnb, kng, knb):
    s = x.shape[0]
    ts = min(ROW_TILE, s)
    row = lambda w: pl.BlockSpec((ts, w), lambda i: (i, 0))
    full = lambda a: pl.BlockSpec(a.shape, lambda i: (0,) * a.ndim)
    return pl.pallas_call(
        _inproj_kernel,
        grid=(s // ts,),
        in_specs=[row(D_MODEL), full(g), full(w_main), full(w_tail), row(LANES), row(LANES), row(LANES),
                  full(cw), full(cb), full(cng), full(cnb), full(kng), full(knb)],
        out_specs=[row(C_CONV), row(D_ATTN), row(D_ATTN), row(D_ATTN), row(D_ATTN), row(LANES)],
        out_shape=[jax.ShapeDtypeStruct((s, C_CONV), jnp.bfloat16),
                   jax.ShapeDtypeStruct((s, D_ATTN), jnp.bfloat16),
                   jax.ShapeDtypeStruct((s, D_ATTN), jnp.bfloat16),
                   jax.ShapeDtypeStruct((s, D_ATTN), jnp.bfloat16),
                   jax.ShapeDtypeStruct((s, D_ATTN), jnp.bfloat16),
                   jax.ShapeDtypeStruct((s, LANES), jnp.float32)],
        scratch_shapes=[pltpu.VMEM((ts + CONV_HALO, C_CONV), jnp.float32),
                        pltpu.VMEM((SUBLANES - 1, ts + CONV_HALO - SUBLANES, C_CONV), jnp.float32)],
        compiler_params=_cparams(("arbitrary",)),
        name="inproj",
    )(x, g, w_main, w_tail, cos_t, sin_lo, sin_hi, cw, cb, cng, cnb, kng, knb)


def _float_to_key(x):
    b = lax.bitcast_convert_type(x, jnp.int32)
    return jnp.where(b < 0, b ^ jnp.int32(0x7FFFFFFF), b)


def _key_to_float(k):
    b = jnp.where(k < 0, k ^ jnp.int32(0x7FFFFFFF), k)
    return lax.bitcast_convert_type(b, jnp.float32)


def _avg_floor(a, b):
    return (a >> 1) + (b >> 1) + (a & b & 1)


def _probit_upper(q):
    p = jnp.minimum(q, 1.0 - q)
    t = jnp.sqrt(-2.0 * jnp.log(p))
    z = t - (2.515517 + t * (0.802853 + t * 0.010328)) / (1.0 + t * (1.432788 + t * (0.189269 + t * 0.001308)))
    return jnp.where(q <= 0.5, z, -z)


def _indexer_kernel(nkb_ref, qi_ref, wi_ref, cq_ref, ki_ref, ck_ref, bias_ref, sc_ref, *, k_sel):
    qb = qi_ref.shape[0]
    s = ki_ref.shape[0]
    kc = min(KEY_CHUNK, s)
    grp = kc // SUBLANES
    n_total = s // kc
    nch = (nkb_ref[pl.program_id(0)] * Q_BLOCK + kc - 1) // kc
    f32, i32 = jnp.float32, jnp.int32
    neg_inf, pos_inf = f32(-jnp.inf), f32(jnp.inf)

    def as3(a):
        return a.reshape(grp, SUBLANES, qb)

    def rows8(v):
        return jnp.broadcast_to(v, (SUBLANES, qb))

    qi = qi_ref[...]
    q_pairs = [jnp.concatenate([qi[:, (2 * p) * IDX_DIM:(2 * p + 1) * IDX_DIM],
                                qi[:, (2 * p + 1) * IDX_DIM:(2 * p + 2) * IDX_DIM]], axis=0)
               for p in range(IDX_HEADS // 2)]
    w8 = [rows8(wi_ref[h:h + 1, :]) for h in range(IDX_HEADS)]
    cq8 = rows8(cq_ref[...])

    def score_chunks(c, carry):
        rmax, rmin, nadm, nge0, ngt0 = carry
        starts = [pl.multiple_of((c * SCORE_UNROLL + r) * kc, kc) for r in range(SCORE_UNROLL)]
        sps = [[lax.dot_general(ki_ref[pl.ds(k0, kc), :], q_pairs[p], _NT, preferred_element_type=f32)
                for p in range(IDX_HEADS // 2)] for k0 in starts]
        for k0, sp in zip(starts, sps):
            acc = jnp.zeros((grp, SUBLANES, qb), f32)
            for p in range(IDX_HEADS // 2):
                acc = acc + w8[2 * p][None] * jnp.maximum(as3(sp[p][:, :qb]), 0.0)
                acc = acc + w8[2 * p + 1][None] * jnp.maximum(as3(sp[p][:, qb:]), 0.0)
            adm = as3(ck_ref[pl.ds(k0, kc), :]) <= cq8[None]
            val = jnp.where(adm, acc, neg_inf)
            sc_ref[pl.ds(k0, kc), :] = val.reshape(kc, qb)
            rmax = jnp.maximum(rmax, jnp.max(val, axis=0))
            rmin = jnp.minimum(rmin, jnp.min(acc, axis=0))
            nadm = nadm + jnp.sum(jnp.where(adm, 1, 0), axis=0)
            nge0 = nge0 + jnp.sum(jnp.where(val >= 0.0, 1, 0), axis=0)
            ngt0 = ngt0 + jnp.sum(jnp.where(val > 0.0, 1, 0), axis=0)
        return rmax, rmin, nadm, nge0, ngt0

    assert n_total % SCORE_UNROLL == 0
    zero8 = jnp.zeros((SUBLANES, qb), i32)
    rmax8, rmin8, nadm8, nge8, ngt8 = lax.fori_loop(
        0, (nch + SCORE_UNROLL - 1) // SCORE_UNROLL, score_chunks,
        (jnp.full((SUBLANES, qb), neg_inf, f32), jnp.full((SUBLANES, qb), pos_inf, f32), zero8, zero8, zero8))
    n_adm = jnp.sum(nadm8, axis=0, keepdims=True)
    row_max = jnp.max(rmax8, axis=0, keepdims=True)
    row_min = jnp.min(rmin8, axis=0, keepdims=True)
    c_ge0 = jnp.sum(nge8, axis=0, keepdims=True)
    c_gt0 = jnp.sum(ngt8, axis=0, keepdims=True)

    def count_rows(pred):
        def body(c, acc):
            k0 = pl.multiple_of(c * kc, kc)
            m = pred(as3(sc_ref[pl.ds(k0, kc), :]), k0)
            parts = [acc, jnp.zeros_like(acc), jnp.zeros_like(acc), jnp.zeros_like(acc)]
            for g in range(grp):
                parts[g % 4] = jnp.where(m[g], parts[g % 4] + 1, parts[g % 4])
            return (parts[0] + parts[1]) + (parts[2] + parts[3])
        acc = lax.fori_loop(0, nch, body, jnp.zeros((SUBLANES, qb), i32))
        return jnp.sum(acc, axis=0, keepdims=True)

    def count_ge(cand):
        c8 = rows8(cand)[None]
        return count_rows(lambda x3, k0: x3 >= c8)

    k_eff = jnp.minimum(n_adm, k_sel)
    inv_n = 1.0 / (n_adm.astype(f32) + 1.0)

    def zscore(cnt):
        return _probit_upper((cnt.astype(f32) + 0.5) * inv_n)

    z_target = zscore(k_eff)

    above = c_gt0 >= k_eff
    below = c_ge0 < k_eff
    zero_key = jnp.zeros((1, qb), i32)
    lo0 = jnp.where(below, _float_to_key(row_min), zero_key)
    clo0 = jnp.where(below, n_adm, c_ge0)
    hi0 = jnp.where(above, _float_to_key(row_max) + 1, jnp.where(below, zero_key - 1, zero_key + 1))
    chi0 = jnp.where(above, 0, jnp.where(below, c_ge0, c_gt0))

    def settled(lo, hi, clo):
        return jnp.logical_or(clo == k_eff, _avg_floor(lo, hi) == lo)

    def cond(st):
        return st[7] > 0

    def step(st):
        lo, hi, clo, chi, z_lo, z_hi, done, _, phase = st
        lo_f, hi_f = _key_to_float(lo), _key_to_float(hi)
        frac = jnp.clip((z_target - z_lo) / (z_hi - z_lo), 0.02, 0.98)
        guess = lo_f + (hi_f - lo_f) * frac
        guess_ok = jnp.logical_and(guess == guess, jnp.abs(guess) < pos_inf)
        mid = jnp.maximum(_avg_floor(lo, hi), lo + 1)
        cand = jnp.clip(_float_to_key(jnp.where(guess_ok, guess, lo_f)), lo + 1, hi - 1)
        cand = jnp.where(jnp.logical_or(phase == 2, jnp.logical_not(guess_ok)), mid, cand)
        cnt = count_ge(_key_to_float(cand))
        z_c = zscore(cnt)
        ge = cnt >= k_eff
        up = jnp.logical_and(done == 0, ge)
        dn = jnp.logical_and(done == 0, jnp.logical_not(ge))
        lo = jnp.where(up, cand, lo)
        clo = jnp.where(up, cnt, clo)
        z_lo = jnp.where(up, z_c, z_lo)
        hi = jnp.where(dn, cand, hi)
        chi = jnp.where(dn, cnt, chi)
        z_hi = jnp.where(dn, z_c, z_hi)
        done = jnp.where(settled(lo, hi, clo), 1, done)
        return lo, hi, clo, chi, z_lo, z_hi, done, jnp.sum(1 - done), jnp.where(phase == 2, 0, phase + 1)

    done0 = jnp.where(settled(lo0, hi0, clo0), 1, 0)
    st0 = (lo0, hi0, clo0, chi0, zscore(clo0), zscore(chi0), done0, jnp.sum(1 - done0), i32(0))
    lo, hi, clo, chi = lax.while_loop(cond, step, st0)[:4]
    thr = _key_to_float(lo)
    thr8 = rows8(thr)[None]

    tie = clo > k_eff
    need = k_eff - chi
    n_tie = jnp.sum(jnp.where(tie, 1, 0))
    key_iota = (lax.broadcasted_iota(i32, (grp, SUBLANES, qb), 0) * SUBLANES
                + lax.broadcasted_iota(i32, (grp, SUBLANES, qb), 1))

    def store_bias(k0, sel):
        bias_ref[0, pl.ds(k0, kc), :] = jnp.where(sel, 0.0, neg_inf).reshape(kc, qb).astype(jnp.bfloat16)

    @pl.when(n_tie == 0)
    def _():
        def emit_chunk(c, carry):
            k0 = pl.multiple_of(c * kc, kc)
            store_bias(k0, as3(sc_ref[pl.ds(k0, kc), :]) >= thr8)
            return carry
        lax.fori_loop(0, nch, emit_chunk, 0)

    @pl.when(n_tie > 0)
    def _():
        def cnt_upto(m):
            m8 = rows8(m)[None]
            return count_rows(lambda x3, k0: jnp.logical_and(x3 == thr8, key_iota + k0 <= m8))

        def body(_, jj):
            jl, jh = jj
            m = (jl + jh) >> 1
            ok = cnt_upto(m) >= need
            return jnp.where(ok, jl, m), jnp.where(ok, m, jh)

        n_it = max(1, (s - 1).bit_length()) + 1
        _, jh = lax.fori_loop(0, n_it, body, (jnp.full((1, qb), -1, i32), jnp.full((1, qb), s - 1, i32)))
        jlim8 = rows8(jnp.where(tie, jh, s))[None]

        def emit_chunk(c, carry):
            k0 = pl.multiple_of(c * kc, kc)
            x3 = as3(sc_ref[pl.ds(k0, kc), :])
            store_bias(k0, jnp.logical_or(x3 > thr8, jnp.logical_and(x3 == thr8, key_iota + k0 <= jlim8)))
            return carry
        lax.fori_loop(0, nch, emit_chunk, 0)

    def fill_chunk(c, carry):
        k0 = pl.multiple_of(c * kc, kc)
        bias_ref[0, pl.ds(k0, kc), :] = jnp.full((kc, qb), neg_inf, jnp.bfloat16)
        return carry

    lax.fori_loop(nch, n_total, fill_chunk, 0)


def _indexer(nkb, qi, wi_t, cq_row, ki, ck_lanes, k_sel):
    s = qi.shape[0]
    qb = min(Q_BLOCK, s)
    grid_spec = pltpu.PrefetchScalarGridSpec(
        num_scalar_prefetch=1,
        grid=(s // qb,),
        in_specs=[pl.BlockSpec((qb, D_ATTN), lambda i, n: (i, 0)),
                  pl.BlockSpec((IDX_HEADS, qb), lambda i, n: (0, i)),
                  pl.BlockSpec((1, qb), lambda i, n: (0, i)),
                  pl.BlockSpec((s, IDX_DIM), lambda i, n: (0, 0)),
                  pl.BlockSpec((s, qb), lambda i, n: (0, 0))],
        out_specs=pl.BlockSpec((1, s, qb), lambda i, n: (i, 0, 0)),
        scratch_shapes=[pltpu.VMEM((s, qb), jnp.float32)],
    )
    return pl.pallas_call(
        functools.partial(_indexer_kernel, k_sel=k_sel),
        grid_spec=grid_spec,
        out_shape=jax.ShapeDtypeStruct((s // qb, s, qb), jnp.bfloat16),
        compiler_params=_cparams(("arbitrary",)),
        name="indexer",
    )(nkb, qi, wi_t, cq_row, ki, ck_lanes)


def _attention_kernel(nkt_ref, q_ref, k_ref, vt_ref, b_ref, o_ref, qm_ref, m_ref, acc_ref):
    i, j = pl.program_id(0), pl.program_id(1)
    tq = q_ref.shape[0]
    n_pairs = D_ATTN // LANES
    bf16 = jnp.bfloat16

    @pl.when(j == 0)
    def _():
        q = q_ref[...]
        lane = lax.broadcasted_iota(jnp.int32, (tq, LANES), 1)
        zero = jnp.zeros((tq, LANES), q.dtype)
        for p in range(n_pairs):
            qp = q[:, p * LANES:(p + 1) * LANES]
            qm_ref[2 * p] = jnp.where(lane < HEAD_DIM, qp, zero)
            qm_ref[2 * p + 1] = jnp.where(lane < HEAD_DIM, zero, qp)
        m_ref[...] = jnp.full(m_ref.shape, -1e30, jnp.float32)
        acc_ref[...] = jnp.zeros(acc_ref.shape, jnp.float32)

    @pl.when(j < nkt_ref[i])
    def _():
        bias = jnp.concatenate([b_ref[r] for r in range(b_ref.shape[0])], axis=1)
        scores = []
        for h in range(N_HEADS):
            kp = k_ref[:, (h // 2) * LANES:(h // 2 + 1) * LANES]
            st = lax.dot_general(kp, qm_ref[h], _NT, preferred_element_type=jnp.float32)
            scores.append(st.astype(bf16) + bias)
        for h in range(N_HEADS):
            p = h // 2
            vtp = vt_ref[p * ATT_V_ROWS:(p + 1) * ATT_V_ROWS, :]
            st = scores[h]
            m_prev = m_ref[h]
            m_new = jnp.maximum(m_prev, jnp.max(st, axis=0, keepdims=True).astype(jnp.float32))
            alpha = jnp.exp2(m_prev - m_new)
            pt = jnp.exp2(st - m_new.astype(bf16))
            acc_ref[h] = alpha * acc_ref[h] + jnp.dot(vtp, pt, preferred_element_type=jnp.float32)
            m_ref[h] = m_new

    @pl.when(j == nkt_ref[i] - 1)
    def _():
        row = lax.broadcasted_iota(jnp.int32, (LANES, tq), 0)
        for p in range(n_pairs):
            a0, a1 = acc_ref[2 * p], acc_ref[2 * p + 1]
            o0 = a0[:LANES] / a0[LANES:LANES + 1]
            o1 = a1[:LANES] / a1[LANES:LANES + 1]
            o_ref[:, p * LANES:(p + 1) * LANES] = jnp.where(row < HEAD_DIM, o0, o1).T.astype(o_ref.dtype)


def _attention(nkt, q, k, vt_aug, bias3):
    s = q.shape[0]
    tq, tk = min(ATT_TQ, s), min(ATT_TK, s)
    qb = bias3.shape[2]
    last = lambda i, j, n: jnp.minimum(j, n[i] - 1)
    grid_spec = pltpu.PrefetchScalarGridSpec(
        num_scalar_prefetch=1,
        grid=(s // tq, s // tk),
        in_specs=[pl.BlockSpec((tq, D_ATTN), lambda i, j, n: (i, 0)),
                  pl.BlockSpec((tk, D_ATTN), lambda i, j, n: (last(i, j, n), 0)),
                  pl.BlockSpec((vt_aug.shape[0], tk), lambda i, j, n: (0, last(i, j, n))),
                  pl.BlockSpec((tq // qb, tk, qb), lambda i, j, n: (i, last(i, j, n), 0))],
        out_specs=pl.BlockSpec((tq, D_ATTN), lambda i, j, n: (i, 0)),
        scratch_shapes=[pltpu.VMEM((N_HEADS, tq, LANES), jnp.bfloat16),
                        pltpu.VMEM((N_HEADS, 1, tq), jnp.float32),
                        pltpu.VMEM((N_HEADS, ATT_V_ROWS, tq), jnp.float32)],
    )
    return pl.pallas_call(
        _attention_kernel,
        grid_spec=grid_spec,
        out_shape=jax.ShapeDtypeStruct((s, D_ATTN), jnp.bfloat16),
        compiler_params=_cparams(("arbitrary", "arbitrary")),
        name="attention",
    )(nkt, q, k, vt_aug, bias3)


def _outproj_kernel(x_ref, u_ref, a_ref, wu_ref, wa_ref, g_ref, wr_ref, br_ref,
                    h_ref, xn_ref, eidx_ref, gate_ref, rank_ref, cnt_ref, carry_ref):
    ts = x_ref.shape[0]

    @pl.when(pl.program_id(0) == 0)
    def _():
        carry_ref[...] = jnp.zeros(carry_ref.shape, jnp.float32)

    h = (x_ref[...]
         + jnp.dot(u_ref[...], wu_ref[...], preferred_element_type=jnp.float32)
         + jnp.dot(a_ref[...], wa_ref[...], preferred_element_type=jnp.float32))
    h_ref[...] = h
    ms = jnp.mean(h * h, axis=-1, keepdims=True)
    xn = h * lax.rsqrt(ms + NORM_EPS) * g_ref[...]
    xn_ref[...] = xn
    logits_t = lax.dot_general(wr_ref[...], xn, _NT, preferred_element_type=jnp.float32,
                               precision=lax.Precision.HIGHEST) + br_ref[...]
    erow = lax.broadcasted_iota(jnp.int32, (N_EXPERTS, ts), 0)
    work = logits_t
    vals, idxs = [], []
    multi = jnp.zeros((N_EXPERTS, ts), jnp.float32)
    for _ in range(TOP_K_EXPERTS):
        mx = jnp.max(work, axis=0, keepdims=True)
        ix = jnp.min(jnp.where(work == mx, erow, N_EXPERTS), axis=0, keepdims=True)
        hit = erow == ix
        multi = jnp.where(hit, 1.0, multi)
        work = jnp.where(hit, -jnp.inf, work)
        vals.append(mx)
        idxs.append(ix)
    ex = [jnp.exp(v - vals[0]) for v in vals]
    den = ex[0] + ex[1] + ex[2] + ex[3]

    r = lax.broadcasted_iota(jnp.int32, (ts, ts), 0)
    c = lax.broadcasted_iota(jnp.int32, (ts, ts), 1)
    earlier = jnp.where(r < c, 1.0, 0.0).astype(jnp.bfloat16)
    prior = jnp.dot(multi.astype(jnp.bfloat16), earlier, preferred_element_type=jnp.float32) + carry_ref[...]
    row8 = lax.broadcasted_iota(jnp.int32, (SUBLANES, ts), 0)
    eidx = jnp.zeros((SUBLANES, ts), jnp.int32)
    gate = jnp.zeros((SUBLANES, ts), jnp.float32)
    rank = jnp.zeros((SUBLANES, ts), jnp.int32)
    for kk in range(TOP_K_EXPERTS):
        rk = jnp.sum(jnp.where(erow == idxs[kk], prior, 0.0), axis=0, keepdims=True)
        eidx = jnp.where(row8 == kk, idxs[kk], eidx)
        gate = jnp.where(row8 == kk, ex[kk] / den, gate)
        rank = jnp.where(row8 == kk, rk.astype(jnp.int32), rank)
    eidx_ref[...] = eidx
    gate_ref[...] = gate
    rank_ref[...] = rank
    carry_ref[...] = carry_ref[...] + jnp.sum(multi, axis=1, keepdims=True)
    cnt_ref[...] = jnp.broadcast_to(carry_ref[...], cnt_ref.shape)


def _outproj(x, u, a, w_u, w_a, g, w_r, b_r):
    s = x.shape[0]
    ts = min(ROW_TILE, s)
    row = lambda w: pl.BlockSpec((ts, w), lambda i: (i, 0))
    col = lambda: pl.BlockSpec((SUBLANES, ts), lambda i: (0, i))
    full = lambda arr: pl.BlockSpec(arr.shape, lambda i: (0,) * arr.ndim)
    return pl.pallas_call(
        _outproj_kernel,
        grid=(s // ts,),
        in_specs=[row(D_MODEL), row(C_CONV), row(D_ATTN), full(w_u), full(w_a), full(g), full(w_r), full(b_r)],
        out_specs=[row(D_MODEL), row(D_MODEL), col(), col(), col(),
                   pl.BlockSpec((N_EXPERTS, LANES), lambda i: (0, 0))],
        out_shape=[jax.ShapeDtypeStruct((s, D_MODEL), jnp.float32),
                   jax.ShapeDtypeStruct((s, D_MODEL), jnp.float32),
                   jax.ShapeDtypeStruct((SUBLANES, s), jnp.int32),
                   jax.ShapeDtypeStruct((SUBLANES, s), jnp.float32),
                   jax.ShapeDtypeStruct((SUBLANES, s), jnp.int32),
                   jax.ShapeDtypeStruct((N_EXPERTS, LANES), jnp.float32)],
        scratch_shapes=[pltpu.VMEM((N_EXPERTS, 1), jnp.float32)],
        compiler_params=_cparams(("arbitrary",)),
        name="outproj_router",
    )(x, u, a, w_u, w_a, g, w_r, b_r)


def _row_copy(src_ref, dst_ref, sem, src_row, dst_row):
    return pltpu.make_async_copy(src_ref.at[pl.ds(src_row, 1), :], dst_ref.at[pl.ds(dst_row, 1), :], sem)


def _pad_copies(pad_start_ref, pad_len_ref, zero_ref, xs_ref, zsem, e):
    pos, n = pad_start_ref[e], pad_len_ref[e]
    head = jnp.minimum((-pos) % SUBLANES, n)
    out = []
    for r in range(SUBLANES - 1):
        cp = pltpu.make_async_copy(zero_ref.at[pl.ds(0, 1), :], xs_ref.at[pl.ds(pos + r, 1), :], zsem)
        out.append((r < head, cp))
    pos, n = pos + head, n - head
    for bit in [1 << i for i in reversed(range(SUBLANES.bit_length() - 1, EXPERT_BLOCK.bit_length() - 1))]:
        dst = xs_ref.at[pl.ds(pl.multiple_of(pos, SUBLANES), bit), :]
        out.append(((n & bit) != 0, pltpu.make_async_copy(zero_ref.at[pl.ds(0, bit), :], dst, zsem)))
        pos = pos + (n & bit)
    return out


def _dispatch_kernel(pad_start_ref, pad_len_ref, tail_ref, dest_ref, xn_ref, xs_ref, zero_ref, sem, zsem):
    tt = xn_ref.shape[0]

    @pl.when(pl.program_id(0) == 0)
    def _():
        zero_ref[...] = jnp.zeros(zero_ref.shape, zero_ref.dtype)
        pads = functools.partial(_pad_copies, pad_start_ref, pad_len_ref, zero_ref, xs_ref, zsem)

        def tail_copy(t):
            row0 = pl.multiple_of(tail_ref[0] + t * EXPERT_BLOCK, EXPERT_BLOCK)
            return pltpu.make_async_copy(zero_ref, xs_ref.at[pl.ds(row0, EXPERT_BLOCK), :], zsem)

        def pad_start(e, carry):
            for cond, cp in pads(e):
                pl.when(cond)(cp.start)
            return carry

        def pad_wait(e, carry):
            for cond, cp in pads(e):
                pl.when(cond)(cp.wait)
            return carry

        def tail_start(t, carry):
            tail_copy(t).start()
            return carry

        def tail_wait(t, carry):
            tail_copy(t).wait()
            return carry

        lax.fori_loop(0, N_EXPERTS, pad_start, 0)
        lax.fori_loop(0, tail_ref[1], tail_start, 0)
        lax.fori_loop(0, N_EXPERTS, pad_wait, 0)
        lax.fori_loop(0, tail_ref[1], tail_wait, 0)

    def start(r, carry):
        for kk in range(TOP_K_EXPERTS):
            _row_copy(xn_ref, xs_ref, sem, r, dest_ref[0, 0, r * TOP_K_EXPERTS + kk]).start(priority=kk % 2)
        return carry

    def wait(r, carry):
        for kk in range(TOP_K_EXPERTS):
            _row_copy(xn_ref, xs_ref, sem, r, dest_ref[0, 0, r * TOP_K_EXPERTS + kk]).wait()
        return carry

    lax.fori_loop(0, tt, start, 0, unroll=ROUTE_UNROLL)
    lax.fori_loop(0, tt, wait, 0, unroll=ROUTE_UNROLL)


def _dispatch(pad_start, pad_len, tail, dest3, xn, n_rows):
    s = xn.shape[0]
    tt = min(ROUTE_TILE, s)
    grid_spec = pltpu.PrefetchScalarGridSpec(
        num_scalar_prefetch=3,
        grid=(s // tt,),
        in_specs=[pl.BlockSpec((1, 1, tt * TOP_K_EXPERTS), lambda i, *_: (i, 0, 0), memory_space=pltpu.SMEM),
                  pl.BlockSpec((tt, D_MODEL), lambda i, *_: (i, 0))],
        out_specs=pl.BlockSpec(memory_space=pl.ANY),
        scratch_shapes=[pltpu.VMEM((EXPERT_BLOCK, D_MODEL), xn.dtype),
                        pltpu.SemaphoreType.DMA(()),
                        pltpu.SemaphoreType.DMA(())],
    )
    return pl.pallas_call(
        _dispatch_kernel,
        grid_spec=grid_spec,
        out_shape=jax.ShapeDtypeStruct((n_rows, D_MODEL), xn.dtype),
        compiler_params=_cparams(("arbitrary",)),
        name="dispatch",
    )(pad_start, pad_len, tail, dest3, xn)


def _expert_weight_copies(wgu_hbm, wdn_hbm, wgu_buf, wdn_buf, sem, expert, slot):
    return (pltpu.make_async_copy(wgu_hbm.at[expert], wgu_buf.at[slot], sem.at[0, slot]),
            pltpu.make_async_copy(wdn_hbm.at[expert], wdn_buf.at[slot], sem.at[1, slot]))


def _experts_kernel(be_ref, nb_ref, run_ref, nxt_ref, xs_ref, wgu_hbm, bgu_ref, wdn_hbm, bdn_ref, ys_ref,
                    wgu_buf, wdn_buf, wgu_bf, wdn_bf, sem):
    b = pl.program_id(0)
    active = b < nb_ref[0]
    fresh = jnp.logical_or(b == 0, be_ref[b] != be_ref[jnp.maximum(b - 1, 0)])
    slot = run_ref[b] % 2
    copies = functools.partial(_expert_weight_copies, wgu_hbm, wdn_hbm, wgu_buf, wdn_buf, sem)

    @pl.when(jnp.logical_and(active, b == 0))
    def _():
        for cp in copies(be_ref[b], slot):
            cp.start()

    @pl.when(jnp.logical_and(active, fresh))
    def _():
        for cp in copies(be_ref[b], slot):
            cp.wait()

        @pl.when(nxt_ref[b] >= 0)
        def _():
            for cp in copies(nxt_ref[b], 1 - slot):
                cp.start()

        wgu_bf[...] = wgu_buf[slot].astype(jnp.bfloat16)
        wdn_bf[...] = wdn_buf[slot].astype(jnp.bfloat16)

    @pl.when(active)
    def _():
        half = xs_ref.shape[0] // 2
        gus = [jnp.dot(xs_ref[r * half:(r + 1) * half, :].astype(jnp.bfloat16), wgu_bf[...],
                       preferred_element_type=jnp.float32) + bgu_ref[0] for r in range(2)]
        for r in range(2):
            g = jnp.minimum(gus[r][:, :D_EXPERT], SWIGLU_LIMIT)
            u = jnp.clip(gus[r][:, D_EXPERT:], -SWIGLU_LIMIT, SWIGLU_LIMIT)
            hdn = g * jax.nn.sigmoid(SWIGLU_ALPHA * g) * (u + 1.0)
            ys_ref[r * half:(r + 1) * half, :] = jnp.dot(hdn.astype(jnp.bfloat16), wdn_bf[...],
                                                        preferred_element_type=jnp.float32) + bdn_ref[0]

    @pl.when(jnp.logical_not(active))
    def _():
        ys_ref[...] = jnp.zeros(ys_ref.shape, ys_ref.dtype)


def _experts(block_expert, n_used, run_id, next_expert, xs, w_gu, b_gu, w_dn, b_dn):
    n_rows = xs.shape[0]
    blk = EXPERT_BLOCK
    row_map = lambda b, be, nb, run, nxt: (jnp.maximum(jnp.minimum(b, nb[0] - 1), 0), 0)
    exp_map = lambda b, be, nb, run, nxt: (be[b], 0, 0)
    grid_spec = pltpu.PrefetchScalarGridSpec(
        num_scalar_prefetch=4,
        grid=(n_rows // blk,),
        in_specs=[pl.BlockSpec((blk, D_MODEL), row_map),
                  pl.BlockSpec(memory_space=pl.ANY),
                  pl.BlockSpec((1, 1, 2 * D_EXPERT), exp_map),
                  pl.BlockSpec(memory_space=pl.ANY),
                  pl.BlockSpec((1, 1, D_MODEL), exp_map)],
        out_specs=pl.BlockSpec((blk, D_MODEL), lambda b, be, nb, run, nxt: (b, 0)),
        scratch_shapes=[pltpu.VMEM((2, D_MODEL, 2 * D_EXPERT), jnp.float32),
                        pltpu.VMEM((2, D_EXPERT, D_MODEL), jnp.float32),
                        pltpu.VMEM((D_MODEL, 2 * D_EXPERT), jnp.bfloat16),
                        pltpu.VMEM((D_EXPERT, D_MODEL), jnp.bfloat16),
                        pltpu.SemaphoreType.DMA((2, 2))],
    )
    return pl.pallas_call(
        _experts_kernel,
        grid_spec=grid_spec,
        out_shape=jax.ShapeDtypeStruct((n_rows, D_MODEL), jnp.float32),
        compiler_params=_cparams(("arbitrary",)),
        name="experts",
    )(block_expert, n_used, run_id, next_expert, xs, w_gu, b_gu, w_dn, b_dn)


def _combine_kernel(dest_ref, h_ref, gate_ref, g_ref, ys_ref, o_ref, buf_ref, sem):
    tt = h_ref.shape[0]

    def start(r, carry):
        for kk in range(TOP_K_EXPERTS):
            _row_copy(ys_ref, buf_ref.at[kk], sem, dest_ref[0, 0, r * TOP_K_EXPERTS + kk], r).start(priority=kk % 2)
        return carry

    def wait(r, carry):
        for kk in range(TOP_K_EXPERTS):
            _row_copy(ys_ref, buf_ref.at[kk], sem, dest_ref[0, 0, r * TOP_K_EXPERTS + kk], r).wait()
        return carry

    lax.fori_loop(0, tt, start, 0, unroll=ROUTE_UNROLL)
    lax.fori_loop(0, tt, wait, 0, unroll=ROUTE_UNROLL)
    gate = gate_ref[...]
    h = h_ref[...]
    for kk in range(TOP_K_EXPERTS):
        h = h + gate[:, kk:kk + 1] * buf_ref[kk]
    ms = jnp.mean(h * h, axis=-1, keepdims=True)
    o_ref[...] = h * lax.rsqrt(ms + NORM_EPS) * g_ref[...]


def _combine(dest3, h, gate, g_final, ys):
    s = h.shape[0]
    tt = min(ROUTE_TILE, s)
    return pl.pallas_call(
        _combine_kernel,
        grid=(s // tt,),
        in_specs=[pl.BlockSpec((1, 1, tt * TOP_K_EXPERTS), lambda i: (i, 0, 0), memory_space=pltpu.SMEM),
                  pl.BlockSpec((tt, D_MODEL), lambda i: (i, 0)),
                  pl.BlockSpec((tt, TOP_K_EXPERTS), lambda i: (i, 0)),
                  pl.BlockSpec((1, D_MODEL), lambda i: (0, 0)),
                  pl.BlockSpec(memory_space=pl.ANY)],
        out_specs=pl.BlockSpec((tt, D_MODEL), lambda i: (i, 0)),
        out_shape=jax.ShapeDtypeStruct((s, D_MODEL), jnp.float32),
        scratch_shapes=[pltpu.VMEM((TOP_K_EXPERTS, tt, D_MODEL), jnp.float32),
                        pltpu.SemaphoreType.DMA(())],
        compiler_params=_cparams(("arbitrary",)),
        name="combine",
    )(dest3, h, gate, g_final, ys)


def _rope_tables(pos):
    half = HEAD_DIM // 2
    inv = ROPE_THETA ** (-jnp.arange(half, dtype=jnp.float32) / half)
    ang = pos.astype(jnp.float32)[:, None] * inv
    cos, sin = jnp.cos(ang), jnp.sin(ang)
    zero = jnp.zeros_like(sin)
    reps = LANES // HEAD_DIM
    cos_t = jnp.tile(jnp.concatenate([cos, cos], axis=-1), (1, reps))
    sin_lo = jnp.tile(jnp.concatenate([-sin, zero], axis=-1), (1, reps))
    sin_hi = jnp.tile(jnp.concatenate([zero, sin], axis=-1), (1, reps))
    return cos_t, sin_lo, sin_hi


def _block_bounds(chunk, q_rows, k_rows):
    cq_max = jnp.max(chunk.reshape(-1, q_rows), axis=1)
    ck_min = jnp.min(chunk.reshape(-1, k_rows), axis=1)
    need = ck_min[None, :] <= cq_max[:, None]
    last = jnp.max(jnp.where(need, jnp.arange(ck_min.shape[0], dtype=jnp.int32)[None, :] + 1, 1), axis=1)
    return last.astype(jnp.int32)


def _layer(h, pos, norm_mix_g, w_in, idx_k_norm_g, idx_k_norm_b, conv_w, conv_b, conv_norm_g, conv_norm_b,
           w_out, norm_ffn_g, w_router, b_router, w_gate_up, b_gate_up, w_down, b_down, out_gain):
    s = h.shape[0]
    f32, bf16 = jnp.float32, jnp.bfloat16
    n_main = 2 * C_CONV + 3 * D_ATTN + IDX_HEADS * IDX_DIM
    w_main = w_in[:, :n_main].astype(bf16)
    w_tail = jnp.pad(w_in[:, n_main:], ((0, 0), (0, LANES - (IDX_DIM + IDX_HEADS)))).astype(bf16)
    cos_t, sin_lo, sin_hi = _rope_tables(pos)
    cw = jnp.pad(conv_w, ((0, CONV_HALO - CONV_WIDTH), (0, 0)))
    kng = jnp.pad(idx_k_norm_g, (0, LANES - IDX_DIM))[None, :]
    knb = jnp.pad(idx_k_norm_b, (0, LANES - IDX_DIM))[None, :]
    u, q, k, v, qi, tail = _inproj(h, norm_mix_g[None, :], w_main, w_tail, cos_t, sin_lo, sin_hi, cw,
                                   conv_b[None, :], conv_norm_g[None, :], conv_norm_b[None, :], kng, knb)

    chunk = pos // CHUNK
    k_sel = min(TOPK_KEYS_MAX, s // 4)
    qb = min(Q_BLOCK, s)
    ki = tail[:, :IDX_DIM].astype(bf16)
    wi_t = tail[:, IDX_DIM:IDX_DIM + IDX_HEADS].T
    bias3 = _indexer(_block_bounds(chunk, qb, qb), qi, wi_t, chunk[None, :], ki,
                     jnp.broadcast_to(chunk[:, None], (s, qb)), k_sel)
    vt_aug = jnp.concatenate(
        [v.T.reshape(D_ATTN // LANES, LANES, s), jnp.ones((D_ATTN // LANES, ATT_V_ROWS - LANES, s), bf16)],
        axis=1).reshape(-1, s)
    a = _attention(_block_bounds(chunk, min(ATT_TQ, s), min(ATT_TK, s)), q, k, vt_aug, bias3)

    h1, xn, eidx_t, gate_t, rank_t, counts = _outproj(
        h, u, a, w_out[:C_CONV].astype(bf16), w_out[C_CONV:].astype(bf16), norm_ffn_g[None, :],
        w_router.T, b_router[:, None])
    eidx, rank = eidx_t[:TOP_K_EXPERTS].T, rank_t[:TOP_K_EXPERTS].T
    gate = gate_t[:TOP_K_EXPERTS].T

    blk = EXPERT_BLOCK
    cnt = counts[:, 0].astype(jnp.int32)
    padded = (cnt + blk - 1) // blk * blk
    end = jnp.cumsum(padded)
    start = end - padded
    n_blocks = s * TOP_K_EXPERTS // blk + N_EXPERTS
    dest = start[eidx] + rank
    tt = min(ROUTE_TILE, s)
    dest3 = dest.reshape(s // tt, 1, tt * TOP_K_EXPERTS)
    block_row = jnp.arange(n_blocks, dtype=jnp.int32) * blk
    block_expert = jnp.minimum(jnp.sum(end[None, :] <= block_row[:, None], axis=1), N_EXPERTS - 1).astype(jnp.int32)
    n_used = (end[-1:] // blk).astype(jnp.int32)

    tail = jnp.stack([end[-1], n_blocks - n_used[0]]).astype(jnp.int32)
    xs = _dispatch(start + cnt, padded - cnt, tail, dest3, xn, n_blocks * blk)
    expert_ids = jnp.arange(N_EXPERTS, dtype=jnp.int32)
    has_rows = cnt > 0
    run_of_expert = jnp.cumsum(has_rows.astype(jnp.int32)) - 1
    later = jnp.logical_and(expert_ids[None, :] > expert_ids[:, None], has_rows[None, :])
    next_of_expert = jnp.where(jnp.any(later, axis=1), jnp.argmax(later, axis=1), -1).astype(jnp.int32)
    block_is = block_expert[:, None] == expert_ids[None, :]
    run_id = jnp.sum(jnp.where(block_is, run_of_expert[None, :], 0), axis=1)
    next_expert = jnp.sum(jnp.where(block_is, next_of_expert[None, :], 0), axis=1)
    ys = _experts(block_expert, n_used, run_id, next_expert, xs,
                  w_gate_up, b_gate_up[:, None, :], w_down, b_down[:, None, :])
    return _combine(dest3, h1, gate, out_gain[None, :], ys)


def kernel(x, positions, norm_mix_g, w_in, idx_k_norm_g, idx_k_norm_b, conv_w, conv_b, conv_norm_g, conv_norm_b,
           w_out, norm_ffn_g, w_router, b_router, w_gate_up, b_gate_up, w_down, b_down, norm_final_g):
    assert x.shape[0] == 1 and norm_mix_g.shape[0] == 1, "single sequence, single layer"
    y = _layer(x[0], positions[0], norm_mix_g[0], w_in[0], idx_k_norm_g[0], idx_k_norm_b[0], conv_w[0],
               conv_b[0], conv_norm_g[0], conv_norm_b[0], w_out[0], norm_ffn_g[0], w_router[0], b_router[0],
               w_gate_up[0], b_gate_up[0], w_down[0], b_down[0], norm_final_g)
    return y[None]
```

```python
import functools
import math

import jax
import jax.numpy as jnp
from jax import lax
from jax.experimental import pallas as pl
from jax.experimental.pallas import tpu as pltpu

D_MODEL = 1024
CHUNK = 64
C_CONV = 512
CONV_WIDTH = 31
N_HEADS = 8
HEAD_DIM = 64
D_ATTN = N_HEADS * HEAD_DIM
IDX_HEADS = 8
IDX_DIM = 64
TOPK_KEYS_MAX = 256
ROPE_THETA = 10000.0
N_EXPERTS = 32
TOP_K_EXPERTS = 4
D_EXPERT = 1024
SWIGLU_LIMIT = 7.0
SWIGLU_ALPHA = 1.702
NORM_EPS = 1e-5

LANES = 128
SUBLANES = 8
CONV_HALO = 32
ROW_TILE = 256
Q_BLOCK = 128
KEY_CHUNK = 512
SCORE_UNROLL = 2
SEARCH_BISECT_EVERY = 8
SEARCH_FRAC_CLIP = 0.1
ATT_TQ = 512
ATT_TK = 512
ATT_V_ROWS = LANES + 16
Q_SCALE = HEAD_DIM ** -0.5 * math.log2(math.e)
EXPERT_BLOCK = 256
ROUTE_TILE = 256
ROUTE_UNROLL = 4
VMEM_LIMIT = 56 * 1024 * 1024

_NT = (((1,), (1,)), ((), ()))


def _cparams(sem):
    return pltpu.CompilerParams(dimension_semantics=sem, vmem_limit_bytes=VMEM_LIMIT)


def _rope128(xp, cos_t, sin_lo, sin_hi):
    return xp * cos_t + pltpu.roll(xp, 96, 1) * sin_lo + pltpu.roll(xp, 32, 1) * sin_hi


def _inproj_kernel(x_ref, g_ref, wm_ref, wt_ref, cos_ref, slo_ref, shi_ref, cw_ref, cb_ref,
                   cng_ref, cnb_ref, kng_ref, knb_ref,
                   u_ref, q_ref, k_ref, v_ref, qi_ref, tail_ref, ubuf_ref, ushift_ref):
    ts = x_ref.shape[0]
    x = x_ref[...]
    ms = jnp.mean(x * x, axis=-1, keepdims=True)
    hn = (x * lax.rsqrt(ms + NORM_EPS) * g_ref[...]).astype(jnp.bfloat16)
    proj = jnp.dot(hn, wm_ref[...], preferred_element_type=jnp.float32)
    tail = jnp.dot(hn, wt_ref[...], preferred_element_type=jnp.float32)

    u = proj[:, 0:C_CONV] * jax.nn.sigmoid(proj[:, C_CONV:2 * C_CONV])

    @pl.when(pl.program_id(0) == 0)
    def _():
        ubuf_ref[0:CONV_HALO, :] = jnp.zeros((CONV_HALO, C_CONV), jnp.float32)

    ubuf_ref[CONV_HALO:CONV_HALO + ts, :] = u
    span = ts + CONV_HALO - SUBLANES
    for b in range(1, SUBLANES):
        ushift_ref[b - 1] = ubuf_ref[b:b + span, :]
    acc = jnp.zeros((ts, C_CONV), jnp.float32) + cb_ref[...]
    for kk in range(CONV_WIDTH):
        off = CONV_HALO - (CONV_WIDTH - 1) + kk
        a, b = off // SUBLANES * SUBLANES, off % SUBLANES
        tap = ubuf_ref[a:a + ts, :] if b == 0 else ushift_ref[b - 1, a:a + ts, :]
        acc = acc + tap * cw_ref[kk:kk + 1, :]
    ubuf_ref[0:CONV_HALO, :] = ubuf_ref[ts:ts + CONV_HALO, :]
    mu = jnp.mean(acc, axis=-1, keepdims=True)
    d = acc - mu
    var = jnp.mean(d * d, axis=-1, keepdims=True)
    yn = d * lax.rsqrt(var + NORM_EPS) * cng_ref[...] + cnb_ref[...]
    u_ref[...] = (yn * jax.nn.sigmoid(yn)).astype(jnp.bfloat16)

    cos_t, sin_lo, sin_hi = cos_ref[...], slo_ref[...], shi_ref[...]
    base = 2 * C_CONV
    for p in range(D_ATTN // LANES):
        lo = p * LANES
        qp = proj[:, base + lo:base + lo + LANES]
        q_ref[:, lo:lo + LANES] = (_rope128(qp, cos_t, sin_lo, sin_hi) * Q_SCALE).astype(jnp.bfloat16)
        kp = proj[:, base + D_ATTN + lo:base + D_ATTN + lo + LANES]
        k_ref[:, lo:lo + LANES] = _rope128(kp, cos_t, sin_lo, sin_hi).astype(jnp.bfloat16)
        qip = proj[:, base + 3 * D_ATTN + lo:base + 3 * D_ATTN + lo + LANES]
        qi_ref[:, lo:lo + LANES] = _rope128(qip, cos_t, sin_lo, sin_hi).astype(jnp.bfloat16)
    v_ref[...] = proj[:, base + 2 * D_ATTN:base + 3 * D_ATTN].astype(jnp.bfloat16)

    lane = lax.broadcasted_iota(jnp.int32, tail.shape, 1)
    is_k = lane < IDX_DIM
    kmu = jnp.sum(jnp.where(is_k, tail, 0.0), axis=-1, keepdims=True) * (1.0 / IDX_DIM)
    kd = jnp.where(is_k, tail - kmu, 0.0)
    kvar = jnp.sum(kd * kd, axis=-1, keepdims=True) * (1.0 / IDX_DIM)
    kn = kd * lax.rsqrt(kvar + NORM_EPS) * kng_ref[...] + knb_ref[...]
    kr = _rope128(kn, cos_t, sin_lo, sin_hi)
    wi = tail * (IDX_HEADS ** -0.5 * IDX_DIM ** -0.5)
    tail_ref[...] = jnp.where(is_k, kr, jnp.where(lane < IDX_DIM + IDX_HEADS, wi, 0.0))


def _inproj(x, g, w_main, w_tail, cos_t, sin_lo, sin_hi, cw, cb, cng, cnb, kng, knb):
    s = x.shape[0]
    ts = min(ROW_TILE, s)
    row = lambda w: pl.BlockSpec((ts, w), lambda i: (i, 0))
    full = lambda a: pl.BlockSpec(a.shape, lambda i: (0,) * a.ndim)
    return pl.pallas_call(
        _inproj_kernel,
        grid=(s // ts,),
        in_specs=[row(D_MODEL), full(g), full(w_main), full(w_tail), row(LANES), row(LANES), row(LANES),
                  full(cw), full(cb), full(cng), full(cnb), full(kng), full(knb)],
        out_specs=[row(C_CONV), row(D_ATTN), row(D_ATTN), row(D_ATTN), row(D_ATTN), row(LANES)],
        out_shape=[jax.ShapeDtypeStruct((s, C_CONV), jnp.bfloat16),
                   jax.ShapeDtypeStruct((s, D_ATTN), jnp.bfloat16),
                   jax.ShapeDtypeStruct((s, D_ATTN), jnp.bfloat16),
                   jax.ShapeDtypeStruct((s, D_ATTN), jnp.bfloat16),
                   jax.ShapeDtypeStruct((s, D_ATTN), jnp.bfloat16),
                   jax.ShapeDtypeStruct((s, LANES), jnp.float32)],
        scratch_shapes=[pltpu.VMEM((ts + CONV_HALO, C_CONV), jnp.float32),
                        pltpu.VMEM((SUBLANES - 1, ts + CONV_HALO - SUBLANES, C_CONV), jnp.float32)],
        compiler_params=_cparams(("arbitrary",)),
        name="inproj",
    )(x, g, w_main, w_tail, cos_t, sin_lo, sin_hi, cw, cb, cng, cnb, kng, knb)


def _float_to_key(x):
    b = lax.bitcast_convert_type(x, jnp.int32)
    return jnp.where(b < 0, b ^ jnp.int32(0x7FFFFFFF), b)


def _key_to_float(k):
    b = jnp.where(k < 0, k ^ jnp.int32(0x7FFFFFFF), k)
    return lax.bitcast_convert_type(b, jnp.float32)


def _avg_floor(a, b):
    return (a >> 1) + (b >> 1) + (a & b & 1)


def _probit_upper(q):
    p = jnp.minimum(q, 1.0 - q)
    t = jnp.sqrt(-2.0 * jnp.log(p))
    z = t - (2.515517 + t * (0.802853 + t * 0.010328)) / (1.0 + t * (1.432788 + t * (0.189269 + t * 0.001308)))
    return jnp.where(q <= 0.5, z, -z)


def _indexer_kernel(nkb_ref, qi_ref, wi_ref, cq_ref, ki_ref, ck_ref, bias_ref, sc_ref, *, k_sel):
    qb = qi_ref.shape[0]
    s = ki_ref.shape[0]
    kc = min(KEY_CHUNK, s)
    grp = kc // SUBLANES
    n_total = s // kc
    nch = (nkb_ref[pl.program_id(0)] * Q_BLOCK + kc - 1) // kc
    f32, i32 = jnp.float32, jnp.int32
    neg_inf, pos_inf = f32(-jnp.inf), f32(jnp.inf)

    def as3(a):
        return a.reshape(grp, SUBLANES, qb)

    def rows8(v):
        return jnp.broadcast_to(v, (SUBLANES, qb))

    qi = qi_ref[...]
    q_pairs = [jnp.concatenate([qi[:, (2 * p) * IDX_DIM:(2 * p + 1) * IDX_DIM],
                                qi[:, (2 * p + 1) * IDX_DIM:(2 * p + 2) * IDX_DIM]], axis=0)
               for p in range(IDX_HEADS // 2)]
    w8 = [rows8(wi_ref[h:h + 1, :]) for h in range(IDX_HEADS)]
    cq8 = rows8(cq_ref[...])

    def score_chunks(c, carry):
        rmax, rmin, nadm, nge0, ngt0 = carry
        starts = [pl.multiple_of((c * SCORE_UNROLL + r) * kc, kc) for r in range(SCORE_UNROLL)]
        sps = [[lax.dot_general(ki_ref[pl.ds(k0, kc), :], q_pairs[p], _NT, preferred_element_type=f32)
                for p in range(IDX_HEADS // 2)] for k0 in starts]
        for k0, sp in zip(starts, sps):
            acc = jnp.zeros((grp, SUBLANES, qb), f32)
            for p in range(IDX_HEADS // 2):
                acc = acc + w8[2 * p][None] * jnp.maximum(as3(sp[p][:, :qb]), 0.0)
                acc = acc + w8[2 * p + 1][None] * jnp.maximum(as3(sp[p][:, qb:]), 0.0)
            adm = as3(ck_ref[pl.ds(k0, kc), :]) <= cq8[None]
            val = jnp.where(adm, acc, neg_inf)
            sc_ref[pl.ds(k0, kc), :] = val.reshape(kc, qb)
            rmax = jnp.maximum(rmax, jnp.max(val, axis=0))
            rmin = jnp.minimum(rmin, jnp.min(acc, axis=0))
            nadm = nadm + jnp.sum(jnp.where(adm, 1, 0), axis=0)
            nge0 = nge0 + jnp.sum(jnp.where(val >= 0.0, 1, 0), axis=0)
            ngt0 = ngt0 + jnp.sum(jnp.where(val > 0.0, 1, 0), axis=0)
        return rmax, rmin, nadm, nge0, ngt0

    assert n_total % SCORE_UNROLL == 0
    n_steps = (nch + SCORE_UNROLL - 1) // SCORE_UNROLL
    zero8 = jnp.zeros((SUBLANES, qb), i32)
    rmax8, rmin8, nadm8, nge8, ngt8 = lax.fori_loop(
        0, n_steps, score_chunks,
        (jnp.full((SUBLANES, qb), neg_inf, f32), jnp.full((SUBLANES, qb), pos_inf, f32), zero8, zero8, zero8))
    n_adm = jnp.sum(nadm8, axis=0, keepdims=True)
    row_max = jnp.max(rmax8, axis=0, keepdims=True)
    row_min = jnp.min(rmin8, axis=0, keepdims=True)
    c_ge0 = jnp.sum(nge8, axis=0, keepdims=True)
    c_gt0 = jnp.sum(ngt8, axis=0, keepdims=True)

    def count_rows(pred):
        def body(c, acc):
            parts = [acc, jnp.zeros_like(acc), jnp.zeros_like(acc), jnp.zeros_like(acc)]
            for r in range(SCORE_UNROLL):
                k0 = pl.multiple_of((c * SCORE_UNROLL + r) * kc, kc)
                m = pred(as3(sc_ref[pl.ds(k0, kc), :]), k0)
                for g in range(grp):
                    parts[g % 4] = jnp.where(m[g], parts[g % 4] + 1, parts[g % 4])
            return (parts[0] + parts[1]) + (parts[2] + parts[3])
        acc = lax.fori_loop(0, n_steps, body, jnp.zeros((SUBLANES, qb), i32))
        return jnp.sum(acc, axis=0, keepdims=True)

    def count_ge(cand):
        c8 = rows8(cand)[None]
        return count_rows(lambda x3, k0: x3 >= c8)

    k_eff = jnp.minimum(n_adm, k_sel)
    inv_n = 1.0 / (n_adm.astype(f32) + 1.0)

    def zscore(cnt):
        return _probit_upper((cnt.astype(f32) + 0.5) * inv_n)

    z_target = zscore(k_eff)

    above = c_gt0 >= k_eff
    below = c_ge0 < k_eff
    zero_key = jnp.zeros((1, qb), i32)
    lo0 = jnp.where(below, _float_to_key(row_min), zero_key)
    clo0 = jnp.where(below, n_adm, c_ge0)
    hi0 = jnp.where(above, _float_to_key(row_max) + 1, jnp.where(below, zero_key - 1, zero_key + 1))
    chi0 = jnp.where(above, 0, jnp.where(below, c_ge0, c_gt0))

    def settled(lo, hi, clo):
        return jnp.logical_or(clo == k_eff, _avg_floor(lo, hi) == lo)

    def cond(st):
        return st[7] > 0

    def step(st):
        lo, hi, clo, chi, z_lo, z_hi, done, _, phase = st
        lo_f, hi_f = _key_to_float(lo), _key_to_float(hi)
        frac = jnp.clip((z_target - z_lo) / (z_hi - z_lo), SEARCH_FRAC_CLIP, 1.0 - SEARCH_FRAC_CLIP)
        guess = lo_f + (hi_f - lo_f) * frac
        guess_ok = jnp.logical_and(guess == guess, jnp.abs(guess) < pos_inf)
        mid = jnp.maximum(_avg_floor(lo, hi), lo + 1)
        cand = jnp.clip(_float_to_key(jnp.where(guess_ok, guess, lo_f)), lo + 1, hi - 1)
        last_phase = SEARCH_BISECT_EVERY - 1
        cand = jnp.where(jnp.logical_or(phase == last_phase, jnp.logical_not(guess_ok)), mid, cand)
        cnt = count_ge(_key_to_float(cand))
        z_c = zscore(cnt)
        ge = cnt >= k_eff
        up = jnp.logical_and(done == 0, ge)
        dn = jnp.logical_and(done == 0, jnp.logical_not(ge))
        lo = jnp.where(up, cand, lo)
        clo = jnp.where(up, cnt, clo)
        z_lo = jnp.where(up, z_c, z_lo)
        hi = jnp.where(dn, cand, hi)
        chi = jnp.where(dn, cnt, chi)
        z_hi = jnp.where(dn, z_c, z_hi)
        done = jnp.where(settled(lo, hi, clo), 1, done)
        return (lo, hi, clo, chi, z_lo, z_hi, done, jnp.sum(1 - done),
                jnp.where(phase == last_phase, 0, phase + 1))

    done0 = jnp.where(settled(lo0, hi0, clo0), 1, 0)
    st0 = (lo0, hi0, clo0, chi0, zscore(clo0), zscore(chi0), done0, jnp.sum(1 - done0), i32(0))
    lo, hi, clo, chi = lax.while_loop(cond, step, st0)[:4]
    thr = _key_to_float(lo)
    thr8 = rows8(thr)[None]

    tie = clo > k_eff
    need = k_eff - chi
    n_tie = jnp.sum(jnp.where(tie, 1, 0))
    key_iota = (lax.broadcasted_iota(i32, (grp, SUBLANES, qb), 0) * SUBLANES
                + lax.broadcasted_iota(i32, (grp, SUBLANES, qb), 1))

    def store_bias(k0, sel):
        bias_ref[0, pl.ds(k0, kc), :] = jnp.where(sel, 0.0, neg_inf).reshape(kc, qb).astype(jnp.bfloat16)

    @pl.when(n_tie == 0)
    def _():
        def emit_chunk(c, carry):
            k0 = pl.multiple_of(c * kc, kc)
            store_bias(k0, as3(sc_ref[pl.ds(k0, kc), :]) >= thr8)
            return carry
        lax.fori_loop(0, nch, emit_chunk, 0)

    @pl.when(n_tie > 0)
    def _():
        def cnt_upto(m):
            m8 = rows8(m)[None]
            return count_rows(lambda x3, k0: jnp.logical_and(x3 == thr8, key_iota + k0 <= m8))

        def body(_, jj):
            jl, jh = jj
            m = (jl + jh) >> 1
            ok = cnt_upto(m) >= need
            return jnp.where(ok, jl, m), jnp.where(ok, m, jh)

        n_it = max(1, (s - 1).bit_length()) + 1
        _, jh = lax.fori_loop(0, n_it, body, (jnp.full((1, qb), -1, i32), jnp.full((1, qb), s - 1, i32)))
        jlim8 = rows8(jnp.where(tie, jh, s))[None]

        def emit_chunk(c, carry):
            k0 = pl.multiple_of(c * kc, kc)
            x3 = as3(sc_ref[pl.ds(k0, kc), :])
            store_bias(k0, jnp.logical_or(x3 > thr8, jnp.logical_and(x3 == thr8, key_iota + k0 <= jlim8)))
            return carry
        lax.fori_loop(0, nch, emit_chunk, 0)

    def fill_chunk(c, carry):
        k0 = pl.multiple_of(c * kc, kc)
        bias_ref[0, pl.ds(k0, kc), :] = jnp.full((kc, qb), neg_inf, jnp.bfloat16)
        return carry

    lax.fori_loop(nch, n_total, fill_chunk, 0)


def _indexer(nkb, qi, wi_t, cq_row, ki, ck_lanes, k_sel):
    s = qi.shape[0]
    qb = min(Q_BLOCK, s)
    grid_spec = pltpu.PrefetchScalarGridSpec(
        num_scalar_prefetch=1,
        grid=(s // qb,),
        in_specs=[pl.BlockSpec((qb, D_ATTN), lambda i, n: (i, 0)),
                  pl.BlockSpec((IDX_HEADS, qb), lambda i, n: (0, i)),
                  pl.BlockSpec((1, qb), lambda i, n: (0, i)),
                  pl.BlockSpec((s, IDX_DIM), lambda i, n: (0, 0)),
                  pl.BlockSpec((s, qb), lambda i, n: (0, 0))],
        out_specs=pl.BlockSpec((1, s, qb), lambda i, n: (i, 0, 0)),
        scratch_shapes=[pltpu.VMEM((s, qb), jnp.float32)],
    )
    return pl.pallas_call(
        functools.partial(_indexer_kernel, k_sel=k_sel),
        grid_spec=grid_spec,
        out_shape=jax.ShapeDtypeStruct((s // qb, s, qb), jnp.bfloat16),
        compiler_params=_cparams(("arbitrary",)),
        name="indexer",
    )(nkb, qi, wi_t, cq_row, ki, ck_lanes)


def _attention_kernel(nkt_ref, q_ref, k_ref, vt_ref, b_ref, o_ref, qm_ref, m_ref, acc_ref):
    i, j = pl.program_id(0), pl.program_id(1)
    tq = q_ref.shape[0]
    n_pairs = D_ATTN // LANES
    bf16 = jnp.bfloat16

    @pl.when(j == 0)
    def _():
        q = q_ref[...]
        lane = lax.broadcasted_iota(jnp.int32, (tq, LANES), 1)
        zero = jnp.zeros((tq, LANES), q.dtype)
        for p in range(n_pairs):
            qp = q[:, p * LANES:(p + 1) * LANES]
            qm_ref[2 * p] = jnp.where(lane < HEAD_DIM, qp, zero)
            qm_ref[2 * p + 1] = jnp.where(lane < HEAD_DIM, zero, qp)
        m_ref[...] = jnp.full(m_ref.shape, -1e30, jnp.float32)
        acc_ref[...] = jnp.zeros(acc_ref.shape, jnp.float32)

    @pl.when(j < nkt_ref[i])
    def _():
        bias = jnp.concatenate([b_ref[r] for r in range(b_ref.shape[0])], axis=1)
        scores = []
        for h in range(N_HEADS):
            kp = k_ref[:, (h // 2) * LANES:(h // 2 + 1) * LANES]
            st = lax.dot_general(kp, qm_ref[h], _NT, preferred_element_type=jnp.float32)
            scores.append(st.astype(bf16) + bias)
        for h in range(N_HEADS):
            p = h // 2
            vtp = vt_ref[p * ATT_V_ROWS:(p + 1) * ATT_V_ROWS, :]
            st = scores[h]
            m_prev = m_ref[h]
            m_new = jnp.maximum(m_prev, jnp.max(st, axis=0, keepdims=True).astype(jnp.float32))
            alpha = jnp.exp2(m_prev - m_new)
            pt = jnp.exp2(st - m_new.astype(bf16))
            acc_ref[h] = alpha * acc_ref[h] + jnp.dot(vtp, pt, preferred_element_type=jnp.float32)
            m_ref[h] = m_new

    @pl.when(j == nkt_ref[i] - 1)
    def _():
        row = lax.broadcasted_iota(jnp.int32, (LANES, tq), 0)
        for p in range(n_pairs):
            a0, a1 = acc_ref[2 * p], acc_ref[2 * p + 1]
            o0 = a0[:LANES] / a0[LANES:LANES + 1]
            o1 = a1[:LANES] / a1[LANES:LANES + 1]
            o_ref[:, p * LANES:(p + 1) * LANES] = jnp.where(row < HEAD_DIM, o0, o1).T.astype(o_ref.dtype)


def _attention(nkt, q, k, vt_aug, bias3):
    s = q.shape[0]
    tq, tk = min(ATT_TQ, s), min(ATT_TK, s)
    qb = bias3.shape[2]
    last = lambda i, j, n: jnp.minimum(j, n[i] - 1)
    grid_spec = pltpu.PrefetchScalarGridSpec(
        num_scalar_prefetch=1,
        grid=(s // tq, s // tk),
        in_specs=[pl.BlockSpec((tq, D_ATTN), lambda i, j, n: (i, 0)),
                  pl.BlockSpec((tk, D_ATTN), lambda i, j, n: (last(i, j, n), 0)),
                  pl.BlockSpec((vt_aug.shape[0], tk), lambda i, j, n: (0, last(i, j, n))),
                  pl.BlockSpec((tq // qb, tk, qb), lambda i, j, n: (i, last(i, j, n), 0))],
        out_specs=pl.BlockSpec((tq, D_ATTN), lambda i, j, n: (i, 0)),
        scratch_shapes=[pltpu.VMEM((N_HEADS, tq, LANES), jnp.bfloat16),
                        pltpu.VMEM((N_HEADS, 1, tq), jnp.float32),
                        pltpu.VMEM((N_HEADS, ATT_V_ROWS, tq), jnp.float32)],
    )
    return pl.pallas_call(
        _attention_kernel,
        grid_spec=grid_spec,
        out_shape=jax.ShapeDtypeStruct((s, D_ATTN), jnp.bfloat16),
        compiler_params=_cparams(("arbitrary", "arbitrary")),
        name="attention",
    )(nkt, q, k, vt_aug, bias3)


def _outproj_kernel(x_ref, u_ref, a_ref, wu_ref, wa_ref, g_ref, wr_ref, br_ref,
                    h_ref, xn_ref, eidx_ref, gate_ref, rank_ref, cnt_ref, carry_ref):
    ts = x_ref.shape[0]

    @pl.when(pl.program_id(0) == 0)
    def _():
        carry_ref[...] = jnp.zeros(carry_ref.shape, jnp.float32)

    h = (x_ref[...]
         + jnp.dot(u_ref[...], wu_ref[...], preferred_element_type=jnp.float32)
         + jnp.dot(a_ref[...], wa_ref[...], preferred_element_type=jnp.float32))
    h_ref[...] = h
    ms = jnp.mean(h * h, axis=-1, keepdims=True)
    xn = h * lax.rsqrt(ms + NORM_EPS) * g_ref[...]
    xn_ref[...] = xn
    logits_t = lax.dot_general(wr_ref[...], xn, _NT, preferred_element_type=jnp.float32,
                               precision=lax.Precision.HIGHEST) + br_ref[...]
    erow = lax.broadcasted_iota(jnp.int32, (N_EXPERTS, ts), 0)
    work = logits_t
    vals, idxs = [], []
    multi = jnp.zeros((N_EXPERTS, ts), jnp.float32)
    for _ in range(TOP_K_EXPERTS):
        mx = jnp.max(work, axis=0, keepdims=True)
        ix = jnp.min(jnp.where(work == mx, erow, N_EXPERTS), axis=0, keepdims=True)
        hit = erow == ix
        multi = jnp.where(hit, 1.0, multi)
        work = jnp.where(hit, -jnp.inf, work)
        vals.append(mx)
        idxs.append(ix)
    ex = [jnp.exp(v - vals[0]) for v in vals]
    den = ex[0] + ex[1] + ex[2] + ex[3]

    r = lax.broadcasted_iota(jnp.int32, (ts, ts), 0)
    c = lax.broadcasted_iota(jnp.int32, (ts, ts), 1)
    earlier = jnp.where(r < c, 1.0, 0.0).astype(jnp.bfloat16)
    prior = jnp.dot(multi.astype(jnp.bfloat16), earlier, preferred_element_type=jnp.float32) + carry_ref[...]
    row8 = lax.broadcasted_iota(jnp.int32, (SUBLANES, ts), 0)
    eidx = jnp.zeros((SUBLANES, ts), jnp.int32)
    gate = jnp.zeros((SUBLANES, ts), jnp.float32)
    rank = jnp.zeros((SUBLANES, ts), jnp.int32)
    for kk in range(TOP_K_EXPERTS):
        rk = jnp.sum(jnp.where(erow == idxs[kk], prior, 0.0), axis=0, keepdims=True)
        eidx = jnp.where(row8 == kk, idxs[kk], eidx)
        gate = jnp.where(row8 == kk, ex[kk] / den, gate)
        rank = jnp.where(row8 == kk, rk.astype(jnp.int32), rank)
    eidx_ref[...] = eidx
    gate_ref[...] = gate
    rank_ref[...] = rank
    carry_ref[...] = carry_ref[...] + jnp.sum(multi, axis=1, keepdims=True)
    cnt_ref[...] = jnp.broadcast_to(carry_ref[...], cnt_ref.shape)


def _outproj(x, u, a, w_u, w_a, g, w_r, b_r):
    s = x.shape[0]
    ts = min(ROW_TILE, s)
    row = lambda w: pl.BlockSpec((ts, w), lambda i: (i, 0))
    col = lambda: pl.BlockSpec((SUBLANES, ts), lambda i: (0, i))
    full = lambda arr: pl.BlockSpec(arr.shape, lambda i: (0,) * arr.ndim)
    return pl.pallas_call(
        _outproj_kernel,
        grid=(s // ts,),
        in_specs=[row(D_MODEL), row(C_CONV), row(D_ATTN), full(w_u), full(w_a), full(g), full(w_r), full(b_r)],
        out_specs=[row(D_MODEL), row(D_MODEL), col(), col(), col(),
                   pl.BlockSpec((N_EXPERTS, LANES), lambda i: (0, 0))],
        out_shape=[jax.ShapeDtypeStruct((s, D_MODEL), jnp.float32),
                   jax.ShapeDtypeStruct((s, D_MODEL), jnp.float32),
                   jax.ShapeDtypeStruct((SUBLANES, s), jnp.int32),
                   jax.ShapeDtypeStruct((SUBLANES, s), jnp.float32),
                   jax.ShapeDtypeStruct((SUBLANES, s), jnp.int32),
                   jax.ShapeDtypeStruct((N_EXPERTS, LANES), jnp.float32)],
        scratch_shapes=[pltpu.VMEM((N_EXPERTS, 1), jnp.float32)],
        compiler_params=_cparams(("arbitrary",)),
        name="outproj_router",
    )(x, u, a, w_u, w_a, g, w_r, b_r)


def _row_copy(src_ref, dst_ref, sem, src_row, dst_row):
    return pltpu.make_async_copy(src_ref.at[pl.ds(src_row, 1), :], dst_ref.at[pl.ds(dst_row, 1), :], sem)


def _pad_copies(pad_start_ref, pad_len_ref, zero_ref, xs_ref, zsem, e):
    pos, n = pad_start_ref[e], pad_len_ref[e]
    head = jnp.minimum((-pos) % SUBLANES, n)
    out = []
    for r in range(SUBLANES - 1):
        cp = pltpu.make_async_copy(zero_ref.at[pl.ds(0, 1), :], xs_ref.at[pl.ds(pos + r, 1), :], zsem)
        out.append((r < head, cp))
    pos, n = pos + head, n - head
    for bit in [1 << i for i in reversed(range(SUBLANES.bit_length() - 1, EXPERT_BLOCK.bit_length() - 1))]:
        dst = xs_ref.at[pl.ds(pl.multiple_of(pos, SUBLANES), bit), :]
        out.append(((n & bit) != 0, pltpu.make_async_copy(zero_ref.at[pl.ds(0, bit), :], dst, zsem)))
        pos = pos + (n & bit)
    return out


def _dispatch_kernel(pad_start_ref, pad_len_ref, tail_ref, dest_ref, xn_ref, xs_ref, zero_ref, sem, zsem):
    tt = xn_ref.shape[0]

    @pl.when(pl.program_id(0) == 0)
    def _():
        zero_ref[...] = jnp.zeros(zero_ref.shape, zero_ref.dtype)
        pads = functools.partial(_pad_copies, pad_start_ref, pad_len_ref, zero_ref, xs_ref, zsem)

        def tail_copy(t):
            row0 = pl.multiple_of(tail_ref[0] + t * EXPERT_BLOCK, EXPERT_BLOCK)
            return pltpu.make_async_copy(zero_ref, xs_ref.at[pl.ds(row0, EXPERT_BLOCK), :], zsem)

        def pad_start(e, carry):
            for cond, cp in pads(e):
                pl.when(cond)(cp.start)
            return carry

        def pad_wait(e, carry):
            for cond, cp in pads(e):
                pl.when(cond)(cp.wait)
            return carry

        def tail_start(t, carry):
            tail_copy(t).start()
            return carry

        def tail_wait(t, carry):
            tail_copy(t).wait()
            return carry

        lax.fori_loop(0, N_EXPERTS, pad_start, 0)
        lax.fori_loop(0, tail_ref[1], tail_start, 0)
        lax.fori_loop(0, N_EXPERTS, pad_wait, 0)
        lax.fori_loop(0, tail_ref[1], tail_wait, 0)

    def start(r, carry):
        for kk in range(TOP_K_EXPERTS):
            _row_copy(xn_ref, xs_ref, sem, r, dest_ref[0, 0, r * TOP_K_EXPERTS + kk]).start(priority=kk % 2)
        return carry

    def wait(r, carry):
        for kk in range(TOP_K_EXPERTS):
            _row_copy(xn_ref, xs_ref, sem, r, dest_ref[0, 0, r * TOP_K_EXPERTS + kk]).wait()
        return carry

    lax.fori_loop(0, tt, start, 0, unroll=ROUTE_UNROLL)
    lax.fori_loop(0, tt, wait, 0, unroll=ROUTE_UNROLL)


def _dispatch(pad_start, pad_len, tail, dest3, xn, n_rows):
    s = xn.shape[0]
    tt = min(ROUTE_TILE, s)
    grid_spec = pltpu.PrefetchScalarGridSpec(
        num_scalar_prefetch=3,
        grid=(s // tt,),
        in_specs=[pl.BlockSpec((1, 1, tt * TOP_K_EXPERTS), lambda i, *_: (i, 0, 0), memory_space=pltpu.SMEM),
                  pl.BlockSpec((tt, D_MODEL), lambda i, *_: (i, 0))],
        out_specs=pl.BlockSpec(memory_space=pl.ANY),
        scratch_shapes=[pltpu.VMEM((EXPERT_BLOCK, D_MODEL), xn.dtype),
                        pltpu.SemaphoreType.DMA(()),
                        pltpu.SemaphoreType.DMA(())],
    )
    return pl.pallas_call(
        _dispatch_kernel,
        grid_spec=grid_spec,
        out_shape=jax.ShapeDtypeStruct((n_rows, D_MODEL), xn.dtype),
        compiler_params=_cparams(("arbitrary",)),
        name="dispatch",
    )(pad_start, pad_len, tail, dest3, xn)


def _expert_weight_copies(wgu_hbm, wdn_hbm, wgu_buf, wdn_buf, sem, expert, slot):
    return (pltpu.make_async_copy(wgu_hbm.at[expert], wgu_buf.at[slot], sem.at[0, slot]),
            pltpu.make_async_copy(wdn_hbm.at[expert], wdn_buf.at[slot], sem.at[1, slot]))


def _experts_kernel(be_ref, nb_ref, run_ref, nxt_ref, xs_ref, wgu_hbm, bgu_ref, wdn_hbm, bdn_ref, ys_ref,
                    wgu_buf, wdn_buf, wgu_bf, wdn_bf, sem):
    b = pl.program_id(0)
    active = b < nb_ref[0]
    fresh = jnp.logical_or(b == 0, be_ref[b] != be_ref[jnp.maximum(b - 1, 0)])
    slot = run_ref[b] % 2
    copies = functools.partial(_expert_weight_copies, wgu_hbm, wdn_hbm, wgu_buf, wdn_buf, sem)

    @pl.when(jnp.logical_and(active, b == 0))
    def _():
        for cp in copies(be_ref[b], slot):
            cp.start()

    @pl.when(jnp.logical_and(active, fresh))
    def _():
        for cp in copies(be_ref[b], slot):
            cp.wait()

        @pl.when(nxt_ref[b] >= 0)
        def _():
            for cp in copies(nxt_ref[b], 1 - slot):
                cp.start()

        wgu_bf[...] = wgu_buf[slot].astype(jnp.bfloat16)
        wdn_bf[...] = wdn_buf[slot].astype(jnp.bfloat16)

    @pl.when(active)
    def _():
        half = xs_ref.shape[0] // 2
        gus = [jnp.dot(xs_ref[r * half:(r + 1) * half, :].astype(jnp.bfloat16), wgu_bf[...],
                       preferred_element_type=jnp.float32) + bgu_ref[0] for r in range(2)]
        for r in range(2):
            g = jnp.minimum(gus[r][:, :D_EXPERT], SWIGLU_LIMIT)
            u = jnp.clip(gus[r][:, D_EXPERT:], -SWIGLU_LIMIT, SWIGLU_LIMIT)
            hdn = g * jax.nn.sigmoid(SWIGLU_ALPHA * g) * (u + 1.0)
            ys_ref[r * half:(r + 1) * half, :] = jnp.dot(hdn.astype(jnp.bfloat16), wdn_bf[...],
                                                        preferred_element_type=jnp.float32) + bdn_ref[0]

    @pl.when(jnp.logical_not(active))
    def _():
        ys_ref[...] = jnp.zeros(ys_ref.shape, ys_ref.dtype)


def _experts(block_expert, n_used, run_id, next_expert, xs, w_gu, b_gu, w_dn, b_dn):
    n_rows = xs.shape[0]
    blk = EXPERT_BLOCK
    row_map = lambda b, be, nb, run, nxt: (jnp.maximum(jnp.minimum(b, nb[0] - 1), 0), 0)
    exp_map = lambda b, be, nb, run, nxt: (be[b], 0, 0)
    grid_spec = pltpu.PrefetchScalarGridSpec(
        num_scalar_prefetch=4,
        grid=(n_rows // blk,),
        in_specs=[pl.BlockSpec((blk, D_MODEL), row_map),
                  pl.BlockSpec(memory_space=pl.ANY),
                  pl.BlockSpec((1, 1, 2 * D_EXPERT), exp_map),
                  pl.BlockSpec(memory_space=pl.ANY),
                  pl.BlockSpec((1, 1, D_MODEL), exp_map)],
        out_specs=pl.BlockSpec((blk, D_MODEL), lambda b, be, nb, run, nxt: (b, 0)),
        scratch_shapes=[pltpu.VMEM((2, D_MODEL, 2 * D_EXPERT), jnp.float32),
                        pltpu.VMEM((2, D_EXPERT, D_MODEL), jnp.float32),
                        pltpu.VMEM((D_MODEL, 2 * D_EXPERT), jnp.bfloat16),
                        pltpu.VMEM((D_EXPERT, D_MODEL), jnp.bfloat16),
                        pltpu.SemaphoreType.DMA((2, 2))],
    )
    return pl.pallas_call(
        _experts_kernel,
        grid_spec=grid_spec,
        out_shape=jax.ShapeDtypeStruct((n_rows, D_MODEL), jnp.float32),
        compiler_params=_cparams(("arbitrary",)),
        name="experts",
    )(block_expert, n_used, run_id, next_expert, xs, w_gu, b_gu, w_dn, b_dn)


def _combine_kernel(dest_ref, h_ref, gate_ref, g_ref, ys_ref, o_ref, buf_ref, sem):
    tt = h_ref.shape[0]

    def start(r, carry):
        for kk in range(TOP_K_EXPERTS):
            _row_copy(ys_ref, buf_ref.at[kk], sem, dest_ref[0, 0, r * TOP_K_EXPERTS + kk], r).start(priority=kk % 2)
        return carry

    def wait(r, carry):
        for kk in range(TOP_K_EXPERTS):
            _row_copy(ys_ref, buf_ref.at[kk], sem, dest_ref[0, 0, r * TOP_K_EXPERTS + kk], r).wait()
        return carry

    lax.fori_loop(0, tt, start, 0, unroll=ROUTE_UNROLL)
    lax.fori_loop(0, tt, wait, 0, unroll=ROUTE_UNROLL)
    gate = gate_ref[...]
    h = h_ref[...]
    for kk in range(TOP_K_EXPERTS):
        h = h + gate[:, kk:kk + 1] * buf_ref[kk]
    ms = jnp.mean(h * h, axis=-1, keepdims=True)
    o_ref[...] = h * lax.rsqrt(ms + NORM_EPS) * g_ref[...]


def _combine(dest3, h, gate, g_final, ys):
    s = h.shape[0]
    tt = min(ROUTE_TILE, s)
    return pl.pallas_call(
        _combine_kernel,
        grid=(s // tt,),
        in_specs=[pl.BlockSpec((1, 1, tt * TOP_K_EXPERTS), lambda i: (i, 0, 0), memory_space=pltpu.SMEM),
                  pl.BlockSpec((tt, D_MODEL), lambda i: (i, 0)),
                  pl.BlockSpec((tt, TOP_K_EXPERTS), lambda i: (i, 0)),
                  pl.BlockSpec((1, D_MODEL), lambda i: (0, 0)),
                  pl.BlockSpec(memory_space=pl.ANY)],
        out_specs=pl.BlockSpec((tt, D_MODEL), lambda i: (i, 0)),
        out_shape=jax.ShapeDtypeStruct((s, D_MODEL), jnp.float32),
        scratch_shapes=[pltpu.VMEM((TOP_K_EXPERTS, tt, D_MODEL), jnp.float32),
                        pltpu.SemaphoreType.DMA(())],
        compiler_params=_cparams(("arbitrary",)),
        name="combine",
    )(dest3, h, gate, g_final, ys)


def _rope_tables(pos):
    half = HEAD_DIM // 2
    inv = ROPE_THETA ** (-jnp.arange(half, dtype=jnp.float32) / half)
    ang = pos.astype(jnp.float32)[:, None] * inv
    cos, sin = jnp.cos(ang), jnp.sin(ang)
    zero = jnp.zeros_like(sin)
    reps = LANES // HEAD_DIM
    cos_t = jnp.tile(jnp.concatenate([cos, cos], axis=-1), (1, reps))
    sin_lo = jnp.tile(jnp.concatenate([-sin, zero], axis=-1), (1, reps))
    sin_hi = jnp.tile(jnp.concatenate([zero, sin], axis=-1), (1, reps))
    return cos_t, sin_lo, sin_hi


def _block_bounds(chunk, q_rows, k_rows):
    cq_max = jnp.max(chunk.reshape(-1, q_rows), axis=1)
    ck_min = jnp.min(chunk.reshape(-1, k_rows), axis=1)
    need = ck_min[None, :] <= cq_max[:, None]
    last = jnp.max(jnp.where(need, jnp.arange(ck_min.shape[0], dtype=jnp.int32)[None, :] + 1, 1), axis=1)
    return last.astype(jnp.int32)


def _layer(h, pos, norm_mix_g, w_in, idx_k_norm_g, idx_k_norm_b, conv_w, conv_b, conv_norm_g, conv_norm_b,
           w_out, norm_ffn_g, w_router, b_router, w_gate_up, b_gate_up, w_down, b_down, out_gain):
    s = h.shape[0]
    f32, bf16 = jnp.float32, jnp.bfloat16
    n_main = 2 * C_CONV + 3 * D_ATTN + IDX_HEADS * IDX_DIM
    w_main = w_in[:, :n_main].astype(bf16)
    w_tail = jnp.pad(w_in[:, n_main:], ((0, 0), (0, LANES - (IDX_DIM + IDX_HEADS)))).astype(bf16)
    cos_t, sin_lo, sin_hi = _rope_tables(pos)
    cw = jnp.pad(conv_w, ((0, CONV_HALO - CONV_WIDTH), (0, 0)))
    kng = jnp.pad(idx_k_norm_g, (0, LANES - IDX_DIM))[None, :]
    knb = jnp.pad(idx_k_norm_b, (0, LANES - IDX_DIM))[None, :]
    u, q, k, v, qi, tail = _inproj(h, norm_mix_g[None, :], w_main, w_tail, cos_t, sin_lo, sin_hi, cw,
                                   conv_b[None, :], conv_norm_g[None, :], conv_norm_b[None, :], kng, knb)

    chunk = pos // CHUNK
    k_sel = min(TOPK_KEYS_MAX, s // 4)
    qb = min(Q_BLOCK, s)
    ki = tail[:, :IDX_DIM].astype(bf16)
    wi_t = tail[:, IDX_DIM:IDX_DIM + IDX_HEADS].T
    bias3 = _indexer(_block_bounds(chunk, qb, qb), qi, wi_t, chunk[None, :], ki,
                     jnp.broadcast_to(chunk[:, None], (s, qb)), k_sel)
    vt_aug = jnp.concatenate(
        [v.T.reshape(D_ATTN // LANES, LANES, s), jnp.ones((D_ATTN // LANES, ATT_V_ROWS - LANES, s), bf16)],
        axis=1).reshape(-1, s)
    a = _attention(_block_bounds(chunk, min(ATT_TQ, s), min(ATT_TK, s)), q, k, vt_aug, bias3)

    h1, xn, eidx_t, gate_t, rank_t, counts = _outproj(
        h, u, a, w_out[:C_CONV].astype(bf16), w_out[C_CONV:].astype(bf16), norm_ffn_g[None, :],
        w_router.T, b_router[:, None])
    eidx, rank = eidx_t[:TOP_K_EXPERTS].T, rank_t[:TOP_K_EXPERTS].T
    gate = gate_t[:TOP_K_EXPERTS].T

    blk = EXPERT_BLOCK
    cnt = counts[:, 0].astype(jnp.int32)
    padded = (cnt + blk - 1) // blk * blk
    end = jnp.cumsum(padded)
    start = end - padded
    n_blocks = s * TOP_K_EXPERTS // blk + N_EXPERTS
    dest = start[eidx] + rank
    tt = min(ROUTE_TILE, s)
    dest3 = dest.reshape(s // tt, 1, tt * TOP_K_EXPERTS)
    block_row = jnp.arange(n_blocks, dtype=jnp.int32) * blk
    block_expert = jnp.minimum(jnp.sum(end[None, :] <= block_row[:, None], axis=1), N_EXPERTS - 1).astype(jnp.int32)
    n_used = (end[-1:] // blk).astype(jnp.int32)

    tail = jnp.stack([end[-1], n_blocks - n_used[0]]).astype(jnp.int32)
    xs = _dispatch(start + cnt, padded - cnt, tail, dest3, xn, n_blocks * blk)
    expert_ids = jnp.arange(N_EXPERTS, dtype=jnp.int32)
    has_rows = cnt > 0
    run_of_expert = jnp.cumsum(has_rows.astype(jnp.int32)) - 1
    later = jnp.logical_and(expert_ids[None, :] > expert_ids[:, None], has_rows[None, :])
    next_of_expert = jnp.where(jnp.any(later, axis=1), jnp.argmax(later, axis=1), -1).astype(jnp.int32)
    block_is = block_expert[:, None] == expert_ids[None, :]
    run_id = jnp.sum(jnp.where(block_is, run_of_expert[None, :], 0), axis=1)
    next_expert = jnp.sum(jnp.where(block_is, next_of_expert[None, :], 0), axis=1)
    ys = _experts(block_expert, n_used, run_id, next_expert, xs,
                  w_gate_up, b_gate_up[:, None, :], w_down, b_down[:, None, :])
    return _combine(dest3, h1, gate, out_gain[None, :], ys)


def kernel(x, positions, norm_mix_g, w_in, idx_k_norm_g, idx_k_norm_b, conv_w, conv_b, conv_norm_g, conv_norm_b,
           w_out, norm_ffn_g, w_router, b_router, w_gate_up, b_gate_up, w_down, b_down, norm_final_g):
    assert x.shape[0] == 1 and norm_mix_g.shape[0] == 1, "single sequence, single layer"
    y = _layer(x[0], positions[0], norm_mix_g[0], w_in[0], idx_k_norm_g[0], idx_k_norm_b[0], conv_w[0],
               conv_b[0], conv_norm_g[0], conv_norm_b[0], w_out[0], norm_ffn_g[0], w_router[0], b_router[0],
               w_gate_up[0], b_gate_up[0], w_down[0], b_down[0], norm_final_g)
    return y[None]
```

```python
import functools
import math

import jax
import jax.numpy as jnp
from jax import lax
from jax.experimental import pallas as pl
from jax.experimental.pallas import tpu as pltpu

D_MODEL = 1024
CHUNK = 64
C_CONV = 512
CONV_WIDTH = 31
N_HEADS = 8
HEAD_DIM = 64
D_ATTN = N_HEADS * HEAD_DIM
IDX_HEADS = 8
IDX_DIM = 64
TOPK_KEYS_MAX = 256
ROPE_THETA = 10000.0
N_EXPERTS = 32
TOP_K_EXPERTS = 4
D_EXPERT = 1024
SWIGLU_LIMIT = 7.0
SWIGLU_ALPHA = 1.702
NORM_EPS = 1e-5

LANES = 128
SUBLANES = 8
CONV_HALO = 32
ROW_TILE = 512
OUT_ROW_TILE = 1024
Q_BLOCK = 128
KEY_CHUNK = 512
SCORE_UNROLL = 2
SEARCH_BISECT_EVERY = 8
SEARCH_FRAC_CLIP = 0.1
ATT_TQ = 512
ATT_TK = 512
ATT_V_ROWS = LANES + 16
Q_SCALE = HEAD_DIM ** -0.5 * math.log2(math.e)
EXPERT_BLOCK = 256
ROUTE_TILE = 512
ROUTE_UNROLL = 4
VMEM_LIMIT = 56 * 1024 * 1024

_NT = (((1,), (1,)), ((), ()))


def _cparams(sem):
    return pltpu.CompilerParams(dimension_semantics=sem, vmem_limit_bytes=VMEM_LIMIT)


def _rope128(xp, cos_t, sin_lo, sin_hi):
    return xp * cos_t + pltpu.roll(xp, 96, 1) * sin_lo + pltpu.roll(xp, 32, 1) * sin_hi


def _inproj_kernel(x_ref, g_ref, wm_ref, wt_ref, cos_ref, slo_ref, shi_ref, cw_ref, cb_ref,
                   cng_ref, cnb_ref, kng_ref, knb_ref,
                   u_ref, q_ref, k_ref, v_ref, qi_ref, tail_ref, ubuf_ref, ushift_ref):
    ts = x_ref.shape[0]
    x = x_ref[...]
    ms = jnp.mean(x * x, axis=-1, keepdims=True)
    hn = (x * lax.rsqrt(ms + NORM_EPS) * g_ref[...]).astype(jnp.bfloat16)
    proj = jnp.dot(hn, wm_ref[...], preferred_element_type=jnp.float32)
    tail = jnp.dot(hn, wt_ref[...], preferred_element_type=jnp.float32)

    u = proj[:, 0:C_CONV] * jax.nn.sigmoid(proj[:, C_CONV:2 * C_CONV])

    @pl.when(pl.program_id(0) == 0)
    def _():
        ubuf_ref[0:CONV_HALO, :] = jnp.zeros((CONV_HALO, C_CONV), jnp.float32)

    ubuf_ref[CONV_HALO:CONV_HALO + ts, :] = u
    span = ts + CONV_HALO - SUBLANES
    for b in range(1, SUBLANES):
        ushift_ref[b - 1] = ubuf_ref[b:b + span, :]
    acc = jnp.zeros((ts, C_CONV), jnp.float32) + cb_ref[...]
    for kk in range(CONV_WIDTH):
        off = CONV_HALO - (CONV_WIDTH - 1) + kk
        a, b = off // SUBLANES * SUBLANES, off % SUBLANES
        tap = ubuf_ref[a:a + ts, :] if b == 0 else ushift_ref[b - 1, a:a + ts, :]
        acc = acc + tap * cw_ref[kk:kk + 1, :]
    ubuf_ref[0:CONV_HALO, :] = ubuf_ref[ts:ts + CONV_HALO, :]
    mu = jnp.mean(acc, axis=-1, keepdims=True)
    d = acc - mu
    var = jnp.mean(d * d, axis=-1, keepdims=True)
    yn = d * lax.rsqrt(var + NORM_EPS) * cng_ref[...] + cnb_ref[...]
    u_ref[...] = (yn * jax.nn.sigmoid(yn)).astype(jnp.bfloat16)

    cos_t, sin_lo, sin_hi = cos_ref[...], slo_ref[...], shi_ref[...]
    base = 2 * C_CONV
    for p in range(D_ATTN // LANES):
        lo = p * LANES
        qp = proj[:, base + lo:base + lo + LANES]
        q_ref[:, lo:lo + LANES] = (_rope128(qp, cos_t, sin_lo, sin_hi) * Q_SCALE).astype(jnp.bfloat16)
        kp = proj[:, base + D_ATTN + lo:base + D_ATTN + lo + LANES]
        k_ref[:, lo:lo + LANES] = _rope128(kp, cos_t, sin_lo, sin_hi).astype(jnp.bfloat16)
        qip = proj[:, base + 3 * D_ATTN + lo:base + 3 * D_ATTN + lo + LANES]
        qi_ref[:, lo:lo + LANES] = _rope128(qip, cos_t, sin_lo, sin_hi).astype(jnp.bfloat16)
    v_ref[...] = proj[:, base + 2 * D_ATTN:base + 3 * D_ATTN].astype(jnp.bfloat16)

    lane = lax.broadcasted_iota(jnp.int32, tail.shape, 1)
    is_k = lane < IDX_DIM
    kmu = jnp.sum(jnp.where(is_k, tail, 0.0), axis=-1, keepdims=True) * (1.0 / IDX_DIM)
    kd = jnp.where(is_k, tail - kmu, 0.0)
    kvar = jnp.sum(kd * kd, axis=-1, keepdims=True) * (1.0 / IDX_DIM)
    kn = kd * lax.rsqrt(kvar + NORM_EPS) * kng_ref[...] + knb_ref[...]
    kr = _rope128(kn, cos_t, sin_lo, sin_hi)
    wi = tail * (IDX_HEADS ** -0.5 * IDX_DIM ** -0.5)
    tail_ref[...] = jnp.where(is_k, kr, jnp.where(lane < IDX_DIM + IDX_HEADS, wi, 0.0))


def _inproj(x, g, w_main, w_tail, cos_t, sin_lo, sin_hi, cw, cb, cng, cnb, kng, knb):
    s = x.shape[0]
    ts = min(ROW_TILE, s)
    row = lambda w: pl.BlockSpec((ts, w), lambda i: (i, 0))
    full = lambda a: pl.BlockSpec(a.shape, lambda i: (0,) * a.ndim)
    return pl.pallas_call(
        _inproj_kernel,
        grid=(s // ts,),
        in_specs=[row(D_MODEL), full(g), full(w_main), full(w_tail), row(LANES), row(LANES), row(LANES),
                  full(cw), full(cb), full(cng), full(cnb), full(kng), full(knb)],
        out_specs=[row(C_CONV), row(D_ATTN), row(D_ATTN), row(D_ATTN), row(D_ATTN), row(LANES)],
        out_shape=[jax.ShapeDtypeStruct((s, C_CONV), jnp.bfloat16),
                   jax.ShapeDtypeStruct((s, D_ATTN), jnp.bfloat16),
                   jax.ShapeDtypeStruct((s, D_ATTN), jnp.bfloat16),
                   jax.ShapeDtypeStruct((s, D_ATTN), jnp.bfloat16),
                   jax.ShapeDtypeStruct((s, D_ATTN), jnp.bfloat16),
                   jax.ShapeDtypeStruct((s, LANES), jnp.float32)],
        scratch_shapes=[pltpu.VMEM((ts + CONV_HALO, C_CONV), jnp.float32),
                        pltpu.VMEM((SUBLANES - 1, ts + CONV_HALO - SUBLANES, C_CONV), jnp.float32)],
        compiler_params=_cparams(("arbitrary",)),
        name="inproj",
    )(x, g, w_main, w_tail, cos_t, sin_lo, sin_hi, cw, cb, cng, cnb, kng, knb)


def _float_to_key(x):
    b = lax.bitcast_convert_type(x, jnp.int32)
    return jnp.where(b < 0, b ^ jnp.int32(0x7FFFFFFF), b)


def _key_to_float(k):
    b = jnp.where(k < 0, k ^ jnp.int32(0x7FFFFFFF), k)
    return lax.bitcast_convert_type(b, jnp.float32)


def _avg_floor(a, b):
    return (a >> 1) + (b >> 1) + (a & b & 1)


def _probit_upper(q):
    p = jnp.minimum(q, 1.0 - q)
    t = jnp.sqrt(-2.0 * jnp.log(p))
    z = t - (2.515517 + t * (0.802853 + t * 0.010328)) / (1.0 + t * (1.432788 + t * (0.189269 + t * 0.001308)))
    return jnp.where(q <= 0.5, z, -z)


def _indexer_kernel(nkb_ref, qi_ref, wi_ref, cq_ref, ki_ref, ck_ref, bias_ref, sc_ref, *, k_sel):
    qb = qi_ref.shape[0]
    s = ki_ref.shape[0]
    kc = min(KEY_CHUNK, s)
    grp = kc // SUBLANES
    n_total = s // kc
    nch = (nkb_ref[pl.program_id(0)] * Q_BLOCK + kc - 1) // kc
    f32, i32 = jnp.float32, jnp.int32
    neg_inf, pos_inf = f32(-jnp.inf), f32(jnp.inf)

    def as3(a):
        return a.reshape(grp, SUBLANES, qb)

    def rows8(v):
        return jnp.broadcast_to(v, (SUBLANES, qb))

    qi = qi_ref[...]
    q_pairs = [jnp.concatenate([qi[:, (2 * p) * IDX_DIM:(2 * p + 1) * IDX_DIM],
                                qi[:, (2 * p + 1) * IDX_DIM:(2 * p + 2) * IDX_DIM]], axis=0)
               for p in range(IDX_HEADS // 2)]
    w8 = [rows8(wi_ref[h:h + 1, :]) for h in range(IDX_HEADS)]
    cq8 = rows8(cq_ref[...])

    def score_chunks(c, carry):
        rmax, rmin, nadm, nge0, ngt0 = carry
        starts = [pl.multiple_of((c * SCORE_UNROLL + r) * kc, kc) for r in range(SCORE_UNROLL)]
        sps = [[lax.dot_general(ki_ref[pl.ds(k0, kc), :], q_pairs[p], _NT, preferred_element_type=f32)
                for p in range(IDX_HEADS // 2)] for k0 in starts]
        for k0, sp in zip(starts, sps):
            acc = jnp.zeros((grp, SUBLANES, qb), f32)
            for p in range(IDX_HEADS // 2):
                acc = acc + w8[2 * p][None] * jnp.maximum(as3(sp[p][:, :qb]), 0.0)
                acc = acc + w8[2 * p + 1][None] * jnp.maximum(as3(sp[p][:, qb:]), 0.0)
            adm = as3(ck_ref[pl.ds(k0, kc), :]) <= cq8[None]
            val = jnp.where(adm, acc, neg_inf)
            sc_ref[pl.ds(k0, kc), :] = val.reshape(kc, qb)
            rmax = jnp.maximum(rmax, jnp.max(val, axis=0))
            rmin = jnp.minimum(rmin, jnp.min(acc, axis=0))
            nadm = nadm + jnp.sum(jnp.where(adm, 1, 0), axis=0)
            nge0 = nge0 + jnp.sum(jnp.where(val >= 0.0, 1, 0), axis=0)
            ngt0 = ngt0 + jnp.sum(jnp.where(val > 0.0, 1, 0), axis=0)
        return rmax, rmin, nadm, nge0, ngt0

    assert n_total % SCORE_UNROLL == 0
    n_steps = (nch + SCORE_UNROLL - 1) // SCORE_UNROLL
    zero8 = jnp.zeros((SUBLANES, qb), i32)
    rmax8, rmin8, nadm8, nge8, ngt8 = lax.fori_loop(
        0, n_steps, score_chunks,
        (jnp.full((SUBLANES, qb), neg_inf, f32), jnp.full((SUBLANES, qb), pos_inf, f32), zero8, zero8, zero8))
    n_adm = jnp.sum(nadm8, axis=0, keepdims=True)
    row_max = jnp.max(rmax8, axis=0, keepdims=True)
    row_min = jnp.min(rmin8, axis=0, keepdims=True)
    c_ge0 = jnp.sum(nge8, axis=0, keepdims=True)
    c_gt0 = jnp.sum(ngt8, axis=0, keepdims=True)

    def count_rows(pred):
        def body(c, acc):
            parts = [acc, jnp.zeros_like(acc), jnp.zeros_like(acc), jnp.zeros_like(acc)]
            for r in range(SCORE_UNROLL):
                k0 = pl.multiple_of((c * SCORE_UNROLL + r) * kc, kc)
                m = pred(as3(sc_ref[pl.ds(k0, kc), :]), k0)
                for g in range(grp):
                    parts[g % 4] = jnp.where(m[g], parts[g % 4] + 1, parts[g % 4])
            return (parts[0] + parts[1]) + (parts[2] + parts[3])
        acc = lax.fori_loop(0, n_steps, body, jnp.zeros((SUBLANES, qb), i32))
        return jnp.sum(acc, axis=0, keepdims=True)

    def count_ge(cand):
        c8 = rows8(cand)[None]
        return count_rows(lambda x3, k0: x3 >= c8)

    k_eff = jnp.minimum(n_adm, k_sel)
    inv_n = 1.0 / (n_adm.astype(f32) + 1.0)

    def zscore(cnt):
        return _probit_upper((cnt.astype(f32) + 0.5) * inv_n)

    z_target = zscore(k_eff)

    above = c_gt0 >= k_eff
    below = c_ge0 < k_eff
    zero_key = jnp.zeros((1, qb), i32)
    lo0 = jnp.where(below, _float_to_key(row_min), zero_key)
    clo0 = jnp.where(below, n_adm, c_ge0)
    hi0 = jnp.where(above, _float_to_key(row_max) + 1, jnp.where(below, zero_key - 1, zero_key + 1))
    chi0 = jnp.where(above, 0, jnp.where(below, c_ge0, c_gt0))

    def settled(lo, hi, clo):
        return jnp.logical_or(clo == k_eff, _avg_floor(lo, hi) == lo)

    def cond(st):
        return st[7] > 0

    def step(st):
        lo, hi, clo, chi, z_lo, z_hi, done, _, phase = st
        lo_f, hi_f = _key_to_float(lo), _key_to_float(hi)
        frac = jnp.clip((z_target - z_lo) / (z_hi - z_lo), SEARCH_FRAC_CLIP, 1.0 - SEARCH_FRAC_CLIP)
        guess = lo_f + (hi_f - lo_f) * frac
        guess_ok = jnp.logical_and(guess == guess, jnp.abs(guess) < pos_inf)
        mid = jnp.maximum(_avg_floor(lo, hi), lo + 1)
        cand = jnp.clip(_float_to_key(jnp.where(guess_ok, guess, lo_f)), lo + 1, hi - 1)
        last_phase = SEARCH_BISECT_EVERY - 1
        cand = jnp.where(jnp.logical_or(phase == last_phase, jnp.logical_not(guess_ok)), mid, cand)
        cnt = count_ge(_key_to_float(cand))
        z_c = zscore(cnt)
        ge = cnt >= k_eff
        up = jnp.logical_and(done == 0, ge)
        dn = jnp.logical_and(done == 0, jnp.logical_not(ge))
        lo = jnp.where(up, cand, lo)
        clo = jnp.where(up, cnt, clo)
        z_lo = jnp.where(up, z_c, z_lo)
        hi = jnp.where(dn, cand, hi)
        chi = jnp.where(dn, cnt, chi)
        z_hi = jnp.where(dn, z_c, z_hi)
        done = jnp.where(settled(lo, hi, clo), 1, done)
        return (lo, hi, clo, chi, z_lo, z_hi, done, jnp.sum(1 - done),
                jnp.where(phase == last_phase, 0, phase + 1))

    done0 = jnp.where(settled(lo0, hi0, clo0), 1, 0)
    st0 = (lo0, hi0, clo0, chi0, zscore(clo0), zscore(chi0), done0, jnp.sum(1 - done0), i32(0))
    lo, hi, clo, chi = lax.while_loop(cond, step, st0)[:4]
    thr = _key_to_float(lo)
    thr8 = rows8(thr)[None]

    tie = clo > k_eff
    need = k_eff - chi
    n_tie = jnp.sum(jnp.where(tie, 1, 0))
    key_iota = (lax.broadcasted_iota(i32, (grp, SUBLANES, qb), 0) * SUBLANES
                + lax.broadcasted_iota(i32, (grp, SUBLANES, qb), 1))

    def store_bias(k0, sel):
        bias_ref[0, pl.ds(k0, kc), :] = jnp.where(sel, 0.0, neg_inf).reshape(kc, qb).astype(jnp.bfloat16)

    @pl.when(n_tie == 0)
    def _():
        def emit_chunk(c, carry):
            k0 = pl.multiple_of(c * kc, kc)
            store_bias(k0, as3(sc_ref[pl.ds(k0, kc), :]) >= thr8)
            return carry
        lax.fori_loop(0, nch, emit_chunk, 0)

    @pl.when(n_tie > 0)
    def _():
        def cnt_upto(m):
            m8 = rows8(m)[None]
            return count_rows(lambda x3, k0: jnp.logical_and(x3 == thr8, key_iota + k0 <= m8))

        def body(_, jj):
            jl, jh = jj
            m = (jl + jh) >> 1
            ok = cnt_upto(m) >= need
            return jnp.where(ok, jl, m), jnp.where(ok, m, jh)

        n_it = max(1, (s - 1).bit_length()) + 1
        _, jh = lax.fori_loop(0, n_it, body, (jnp.full((1, qb), -1, i32), jnp.full((1, qb), s - 1, i32)))
        jlim8 = rows8(jnp.where(tie, jh, s))[None]

        def emit_chunk(c, carry):
            k0 = pl.multiple_of(c * kc, kc)
            x3 = as3(sc_ref[pl.ds(k0, kc), :])
            store_bias(k0, jnp.logical_or(x3 > thr8, jnp.logical_and(x3 == thr8, key_iota + k0 <= jlim8)))
            return carry
        lax.fori_loop(0, nch, emit_chunk, 0)

    def fill_chunk(c, carry):
        k0 = pl.multiple_of(c * kc, kc)
        bias_ref[0, pl.ds(k0, kc), :] = jnp.full((kc, qb), neg_inf, jnp.bfloat16)
        return carry

    lax.fori_loop(nch, n_total, fill_chunk, 0)


def _indexer(nkb, qi, wi_t, cq_row, ki, ck_lanes, k_sel):
    s = qi.shape[0]
    qb = min(Q_BLOCK, s)
    grid_spec = pltpu.PrefetchScalarGridSpec(
        num_scalar_prefetch=1,
        grid=(s // qb,),
        in_specs=[pl.BlockSpec((qb, D_ATTN), lambda i, n: (i, 0)),
                  pl.BlockSpec((IDX_HEADS, qb), lambda i, n: (0, i)),
                  pl.BlockSpec((1, qb), lambda i, n: (0, i)),
                  pl.BlockSpec((s, IDX_DIM), lambda i, n: (0, 0)),
                  pl.BlockSpec((s, qb), lambda i, n: (0, 0))],
        out_specs=pl.BlockSpec((1, s, qb), lambda i, n: (i, 0, 0)),
        scratch_shapes=[pltpu.VMEM((s, qb), jnp.float32)],
    )
    return pl.pallas_call(
        functools.partial(_indexer_kernel, k_sel=k_sel),
        grid_spec=grid_spec,
        out_shape=jax.ShapeDtypeStruct((s // qb, s, qb), jnp.bfloat16),
        compiler_params=_cparams(("arbitrary",)),
        name="indexer",
    )(nkb, qi, wi_t, cq_row, ki, ck_lanes)


def _attention_kernel(nkt_ref, q_ref, k_ref, vt_ref, b_ref, o_ref, qm_ref, m_ref, acc_ref):
    i, j = pl.program_id(0), pl.program_id(1)
    tq = q_ref.shape[0]
    n_pairs = D_ATTN // LANES
    bf16 = jnp.bfloat16

    @pl.when(j == 0)
    def _():
        q = q_ref[...]
        lane = lax.broadcasted_iota(jnp.int32, (tq, LANES), 1)
        zero = jnp.zeros((tq, LANES), q.dtype)
        for p in range(n_pairs):
            qp = q[:, p * LANES:(p + 1) * LANES]
            qm_ref[2 * p] = jnp.where(lane < HEAD_DIM, qp, zero)
            qm_ref[2 * p + 1] = jnp.where(lane < HEAD_DIM, zero, qp)
        m_ref[...] = jnp.full(m_ref.shape, -1e30, jnp.float32)
        acc_ref[...] = jnp.zeros(acc_ref.shape, jnp.float32)

    @pl.when(j < nkt_ref[i])
    def _():
        bias = jnp.concatenate([b_ref[r] for r in range(b_ref.shape[0])], axis=1)
        scores = []
        for h in range(N_HEADS):
            kp = k_ref[:, (h // 2) * LANES:(h // 2 + 1) * LANES]
            st = lax.dot_general(kp, qm_ref[h], _NT, preferred_element_type=jnp.float32)
            scores.append(st.astype(bf16) + bias)
        for h in range(N_HEADS):
            p = h // 2
            vtp = vt_ref[p * ATT_V_ROWS:(p + 1) * ATT_V_ROWS, :]
            st = scores[h]
            m_prev = m_ref[h]
            m_new = jnp.maximum(m_prev, jnp.max(st, axis=0, keepdims=True).astype(jnp.float32))
            alpha = jnp.exp2(m_prev - m_new)
            pt = jnp.exp2(st - m_new.astype(bf16))
            acc_ref[h] = alpha * acc_ref[h] + jnp.dot(vtp, pt, preferred_element_type=jnp.float32)
            m_ref[h] = m_new

    @pl.when(j == nkt_ref[i] - 1)
    def _():
        row = lax.broadcasted_iota(jnp.int32, (LANES, tq), 0)
        for p in range(n_pairs):
            a0, a1 = acc_ref[2 * p], acc_ref[2 * p + 1]
            o0 = a0[:LANES] / a0[LANES:LANES + 1]
            o1 = a1[:LANES] / a1[LANES:LANES + 1]
            o_ref[:, p * LANES:(p + 1) * LANES] = jnp.where(row < HEAD_DIM, o0, o1).T.astype(o_ref.dtype)


def _attention(nkt, q, k, vt_aug, bias3):
    s = q.shape[0]
    tq, tk = min(ATT_TQ, s), min(ATT_TK, s)
    qb = bias3.shape[2]
    last = lambda i, j, n: jnp.minimum(j, n[i] - 1)
    grid_spec = pltpu.PrefetchScalarGridSpec(
        num_scalar_prefetch=1,
        grid=(s // tq, s // tk),
        in_specs=[pl.BlockSpec((tq, D_ATTN), lambda i, j, n: (i, 0)),
                  pl.BlockSpec((tk, D_ATTN), lambda i, j, n: (last(i, j, n), 0)),
                  pl.BlockSpec((vt_aug.shape[0], tk), lambda i, j, n: (0, last(i, j, n))),
                  pl.BlockSpec((tq // qb, tk, qb), lambda i, j, n: (i, last(i, j, n), 0))],
        out_specs=pl.BlockSpec((tq, D_ATTN), lambda i, j, n: (i, 0)),
        scratch_shapes=[pltpu.VMEM((N_HEADS, tq, LANES), jnp.bfloat16),
                        pltpu.VMEM((N_HEADS, 1, tq), jnp.float32),
                        pltpu.VMEM((N_HEADS, ATT_V_ROWS, tq), jnp.float32)],
    )
    return pl.pallas_call(
        _attention_kernel,
        grid_spec=grid_spec,
        out_shape=jax.ShapeDtypeStruct((s, D_ATTN), jnp.bfloat16),
        compiler_params=_cparams(("arbitrary", "arbitrary")),
        name="attention",
    )(nkt, q, k, vt_aug, bias3)


def _outproj_kernel(x_ref, u_ref, a_ref, wu_ref, wa_ref, g_ref, wr_ref, br_ref,
                    h_ref, xn_ref, eidx_ref, gate_ref, rank_ref, cnt_ref, carry_ref):
    ts = x_ref.shape[0]

    @pl.when(pl.program_id(0) == 0)
    def _():
        carry_ref[...] = jnp.zeros(carry_ref.shape, jnp.float32)

    h = (x_ref[...]
         + jnp.dot(u_ref[...], wu_ref[...], preferred_element_type=jnp.float32)
         + jnp.dot(a_ref[...], wa_ref[...], preferred_element_type=jnp.float32))
    h_ref[...] = h
    ms = jnp.mean(h * h, axis=-1, keepdims=True)
    xn = h * lax.rsqrt(ms + NORM_EPS) * g_ref[...]
    xn_ref[...] = xn
    logits_t = lax.dot_general(wr_ref[...], xn, _NT, preferred_element_type=jnp.float32,
                               precision=lax.Precision.HIGHEST) + br_ref[...]
    erow = lax.broadcasted_iota(jnp.int32, (N_EXPERTS, ts), 0)
    work = logits_t
    vals, idxs = [], []
    multi = jnp.zeros((N_EXPERTS, ts), jnp.float32)
    for _ in range(TOP_K_EXPERTS):
        mx = jnp.max(work, axis=0, keepdims=True)
        ix = jnp.min(jnp.where(work == mx, erow, N_EXPERTS), axis=0, keepdims=True)
        hit = erow == ix
        multi = jnp.where(hit, 1.0, multi)
        work = jnp.where(hit, -jnp.inf, work)
        vals.append(mx)
        idxs.append(ix)
    ex = [jnp.exp(v - vals[0]) for v in vals]
    den = ex[0] + ex[1] + ex[2] + ex[3]

    r = lax.broadcasted_iota(jnp.int32, (ts, ts), 0)
    c = lax.broadcasted_iota(jnp.int32, (ts, ts), 1)
    earlier = jnp.where(r < c, 1.0, 0.0).astype(jnp.bfloat16)
    prior = jnp.dot(multi.astype(jnp.bfloat16), earlier, preferred_element_type=jnp.float32) + carry_ref[...]
    row8 = lax.broadcasted_iota(jnp.int32, (SUBLANES, ts), 0)
    eidx = jnp.zeros((SUBLANES, ts), jnp.int32)
    gate = jnp.zeros((SUBLANES, ts), jnp.float32)
    rank = jnp.zeros((SUBLANES, ts), jnp.int32)
    for kk in range(TOP_K_EXPERTS):
        rk = jnp.sum(jnp.where(erow == idxs[kk], prior, 0.0), axis=0, keepdims=True)
        eidx = jnp.where(row8 == kk, idxs[kk], eidx)
        gate = jnp.where(row8 == kk, ex[kk] / den, gate)
        rank = jnp.where(row8 == kk, rk.astype(jnp.int32), rank)
    eidx_ref[...] = eidx
    gate_ref[...] = gate
    rank_ref[...] = rank
    carry_ref[...] = carry_ref[...] + jnp.sum(multi, axis=1, keepdims=True)
    cnt_ref[...] = jnp.broadcast_to(carry_ref[...], cnt_ref.shape)


def _outproj(x, u, a, w_u, w_a, g, w_r, b_r):
    s = x.shape[0]
    ts = min(OUT_ROW_TILE, s)
    row = lambda w: pl.BlockSpec((ts, w), lambda i: (i, 0))
    col = lambda: pl.BlockSpec((SUBLANES, ts), lambda i: (0, i))
    full = lambda arr: pl.BlockSpec(arr.shape, lambda i: (0,) * arr.ndim)
    return pl.pallas_call(
        _outproj_kernel,
        grid=(s // ts,),
        in_specs=[row(D_MODEL), row(C_CONV), row(D_ATTN), full(w_u), full(w_a), full(g), full(w_r), full(b_r)],
        out_specs=[row(D_MODEL), row(D_MODEL), col(), col(), col(),
                   pl.BlockSpec((N_EXPERTS, LANES), lambda i: (0, 0))],
        out_shape=[jax.ShapeDtypeStruct((s, D_MODEL), jnp.float32),
                   jax.ShapeDtypeStruct((s, D_MODEL), jnp.float32),
                   jax.ShapeDtypeStruct((SUBLANES, s), jnp.int32),
                   jax.ShapeDtypeStruct((SUBLANES, s), jnp.float32),
                   jax.ShapeDtypeStruct((SUBLANES, s), jnp.int32),
                   jax.ShapeDtypeStruct((N_EXPERTS, LANES), jnp.float32)],
        scratch_shapes=[pltpu.VMEM((N_EXPERTS, 1), jnp.float32)],
        compiler_params=_cparams(("arbitrary",)),
        name="outproj_router",
    )(x, u, a, w_u, w_a, g, w_r, b_r)


def _row_copy(src_ref, dst_ref, sem, src_row, dst_row):
    return pltpu.make_async_copy(src_ref.at[pl.ds(src_row, 1), :], dst_ref.at[pl.ds(dst_row, 1), :], sem)


def _pad_copies(pad_start_ref, pad_len_ref, zero_ref, xs_ref, zsem, e):
    pos, n = pad_start_ref[e], pad_len_ref[e]
    head = jnp.minimum((-pos) % SUBLANES, n)
    out = []
    for r in range(SUBLANES - 1):
        cp = pltpu.make_async_copy(zero_ref.at[pl.ds(0, 1), :], xs_ref.at[pl.ds(pos + r, 1), :], zsem)
        out.append((r < head, cp))
    pos, n = pos + head, n - head
    for bit in [1 << i for i in reversed(range(SUBLANES.bit_length() - 1, EXPERT_BLOCK.bit_length() - 1))]:
        dst = xs_ref.at[pl.ds(pl.multiple_of(pos, SUBLANES), bit), :]
        out.append(((n & bit) != 0, pltpu.make_async_copy(zero_ref.at[pl.ds(0, bit), :], dst, zsem)))
        pos = pos + (n & bit)
    return out


def _dispatch_kernel(pad_start_ref, pad_len_ref, tail_ref, dest_ref, xn_ref, xs_ref, zero_ref, sem, zsem):
    tt = xn_ref.shape[0]

    @pl.when(pl.program_id(0) == 0)
    def _():
        zero_ref[...] = jnp.zeros(zero_ref.shape, zero_ref.dtype)
        pads = functools.partial(_pad_copies, pad_start_ref, pad_len_ref, zero_ref, xs_ref, zsem)

        def tail_copy(t):
            row0 = pl.multiple_of(tail_ref[0] + t * EXPERT_BLOCK, EXPERT_BLOCK)
            return pltpu.make_async_copy(zero_ref, xs_ref.at[pl.ds(row0, EXPERT_BLOCK), :], zsem)

        def pad_start(e, carry):
            for cond, cp in pads(e):
                pl.when(cond)(cp.start)
            return carry

        def pad_wait(e, carry):
            for cond, cp in pads(e):
                pl.when(cond)(cp.wait)
            return carry

        def tail_start(t, carry):
            tail_copy(t).start()
            return carry

        def tail_wait(t, carry):
            tail_copy(t).wait()
            return carry

        lax.fori_loop(0, N_EXPERTS, pad_start, 0)
        lax.fori_loop(0, tail_ref[1], tail_start, 0)
        lax.fori_loop(0, N_EXPERTS, pad_wait, 0)
        lax.fori_loop(0, tail_ref[1], tail_wait, 0)

    def start(r, carry):
        for kk in range(TOP_K_EXPERTS):
            _row_copy(xn_ref, xs_ref, sem, r, dest_ref[0, 0, r * TOP_K_EXPERTS + kk]).start(priority=kk % 2)
        return carry

    lax.fori_loop(0, tt, start, 0, unroll=ROUTE_UNROLL)
    for _ in range(TOP_K_EXPERTS):
        pltpu.make_async_copy(xn_ref, xs_ref.at[pl.ds(0, tt), :], sem).wait()


def _dispatch(pad_start, pad_len, tail, dest3, xn, n_rows):
    s = xn.shape[0]
    tt = min(ROUTE_TILE, s)
    grid_spec = pltpu.PrefetchScalarGridSpec(
        num_scalar_prefetch=3,
        grid=(s // tt,),
        in_specs=[pl.BlockSpec((1, 1, tt * TOP_K_EXPERTS), lambda i, *_: (i, 0, 0), memory_space=pltpu.SMEM),
                  pl.BlockSpec((tt, D_MODEL), lambda i, *_: (i, 0))],
        out_specs=pl.BlockSpec(memory_space=pl.ANY),
        scratch_shapes=[pltpu.VMEM((EXPERT_BLOCK, D_MODEL), xn.dtype),
                        pltpu.SemaphoreType.DMA(()),
                        pltpu.SemaphoreType.DMA(())],
    )
    return pl.pallas_call(
        _dispatch_kernel,
        grid_spec=grid_spec,
        out_shape=jax.ShapeDtypeStruct((n_rows, D_MODEL), xn.dtype),
        compiler_params=_cparams(("arbitrary",)),
        name="dispatch",
    )(pad_start, pad_len, tail, dest3, xn)


def _expert_weight_copies(wgu_hbm, wdn_hbm, wgu_buf, wdn_buf, sem, expert, slot):
    return (pltpu.make_async_copy(wgu_hbm.at[expert], wgu_buf.at[slot], sem.at[0, slot]),
            pltpu.make_async_copy(wdn_hbm.at[expert], wdn_buf.at[slot], sem.at[1, slot]))


def _experts_kernel(be_ref, nb_ref, run_ref, nxt_ref, xs_ref, wgu_hbm, bgu_ref, wdn_hbm, bdn_ref, ys_ref,
                    wgu_buf, wdn_buf, wgu_bf, wdn_bf, sem):
    b = pl.program_id(0)
    active = b < nb_ref[0]
    fresh = jnp.logical_or(b == 0, be_ref[b] != be_ref[jnp.maximum(b - 1, 0)])
    slot = run_ref[b] % 2
    copies = functools.partial(_expert_weight_copies, wgu_hbm, wdn_hbm, wgu_buf, wdn_buf, sem)

    @pl.when(jnp.logical_and(active, b == 0))
    def _():
        for cp in copies(be_ref[b], slot):
            cp.start()

    @pl.when(jnp.logical_and(active, fresh))
    def _():
        for cp in copies(be_ref[b], slot):
            cp.wait()

        @pl.when(nxt_ref[b] >= 0)
        def _():
            for cp in copies(nxt_ref[b], 1 - slot):
                cp.start()

        wgu_bf[...] = wgu_buf[slot].astype(jnp.bfloat16)
        wdn_bf[...] = wdn_buf[slot].astype(jnp.bfloat16)

    @pl.when(active)
    def _():
        half = xs_ref.shape[0] // 2
        gus = [jnp.dot(xs_ref[r * half:(r + 1) * half, :].astype(jnp.bfloat16), wgu_bf[...],
                       preferred_element_type=jnp.float32) + bgu_ref[0] for r in range(2)]
        for r in range(2):
            g = jnp.minimum(gus[r][:, :D_EXPERT], SWIGLU_LIMIT)
            u = jnp.clip(gus[r][:, D_EXPERT:], -SWIGLU_LIMIT, SWIGLU_LIMIT)
            hdn = g * jax.nn.sigmoid(SWIGLU_ALPHA * g) * (u + 1.0)
            ys_ref[r * half:(r + 1) * half, :] = jnp.dot(hdn.astype(jnp.bfloat16), wdn_bf[...],
                                                        preferred_element_type=jnp.float32) + bdn_ref[0]

    @pl.when(jnp.logical_not(active))
    def _():
        ys_ref[...] = jnp.zeros(ys_ref.shape, ys_ref.dtype)


def _experts(block_expert, n_used, run_id, next_expert, xs, w_gu, b_gu, w_dn, b_dn):
    n_rows = xs.shape[0]
    blk = EXPERT_BLOCK
    row_map = lambda b, be, nb, run, nxt: (jnp.maximum(jnp.minimum(b, nb[0] - 1), 0), 0)
    exp_map = lambda b, be, nb, run, nxt: (be[b], 0, 0)
    grid_spec = pltpu.PrefetchScalarGridSpec(
        num_scalar_prefetch=4,
        grid=(n_rows // blk,),
        in_specs=[pl.BlockSpec((blk, D_MODEL), row_map),
                  pl.BlockSpec(memory_space=pl.ANY),
                  pl.BlockSpec((1, 1, 2 * D_EXPERT), exp_map),
                  pl.BlockSpec(memory_space=pl.ANY),
                  pl.BlockSpec((1, 1, D_MODEL), exp_map)],
        out_specs=pl.BlockSpec((blk, D_MODEL), lambda b, be, nb, run, nxt: (b, 0)),
        scratch_shapes=[pltpu.VMEM((2, D_MODEL, 2 * D_EXPERT), jnp.float32),
                        pltpu.VMEM((2, D_EXPERT, D_MODEL), jnp.float32),
                        pltpu.VMEM((D_MODEL, 2 * D_EXPERT), jnp.bfloat16),
                        pltpu.VMEM((D_EXPERT, D_MODEL), jnp.bfloat16),
                        pltpu.SemaphoreType.DMA((2, 2))],
    )
    return pl.pallas_call(
        _experts_kernel,
        grid_spec=grid_spec,
        out_shape=jax.ShapeDtypeStruct((n_rows, D_MODEL), jnp.float32),
        compiler_params=_cparams(("arbitrary",)),
        name="experts",
    )(block_expert, n_used, run_id, next_expert, xs, w_gu, b_gu, w_dn, b_dn)


def _combine_kernel(dest_ref, h_ref, gate_ref, g_ref, ys_ref, o_ref, buf_ref, sem):
    tt = h_ref.shape[0]

    def start(r, carry):
        for kk in range(TOP_K_EXPERTS):
            _row_copy(ys_ref, buf_ref.at[kk], sem, dest_ref[0, 0, r * TOP_K_EXPERTS + kk], r).start(priority=kk % 2)
        return carry

    lax.fori_loop(0, tt, start, 0, unroll=ROUTE_UNROLL)
    for kk in range(TOP_K_EXPERTS):
        pltpu.make_async_copy(ys_ref.at[pl.ds(0, tt), :], buf_ref.at[kk], sem).wait()
    gate = gate_ref[...]
    h = h_ref[...]
    for kk in range(TOP_K_EXPERTS):
        h = h + gate[:, kk:kk + 1] * buf_ref[kk]
    ms = jnp.mean(h * h, axis=-1, keepdims=True)
    o_ref[...] = h * lax.rsqrt(ms + NORM_EPS) * g_ref[...]


def _combine(dest3, h, gate, g_final, ys):
    s = h.shape[0]
    tt = min(ROUTE_TILE, s)
    return pl.pallas_call(
        _combine_kernel,
        grid=(s // tt,),
        in_specs=[pl.BlockSpec((1, 1, tt * TOP_K_EXPERTS), lambda i: (i, 0, 0), memory_space=pltpu.SMEM),
                  pl.BlockSpec((tt, D_MODEL), lambda i: (i, 0)),
                  pl.BlockSpec((tt, TOP_K_EXPERTS), lambda i: (i, 0)),
                  pl.BlockSpec((1, D_MODEL), lambda i: (0, 0)),
                  pl.BlockSpec(memory_space=pl.ANY)],
        out_specs=pl.BlockSpec((tt, D_MODEL), lambda i: (i, 0)),
        out_shape=jax.ShapeDtypeStruct((s, D_MODEL), jnp.float32),
        scratch_shapes=[pltpu.VMEM((TOP_K_EXPERTS, tt, D_MODEL), jnp.float32),
                        pltpu.SemaphoreType.DMA(())],
        compiler_params=_cparams(("arbitrary",)),
        name="combine",
    )(dest3, h, gate, g_final, ys)


def _rope_tables(pos):
    half = HEAD_DIM // 2
    inv = ROPE_THETA ** (-jnp.arange(half, dtype=jnp.float32) / half)
    ang = pos.astype(jnp.float32)[:, None] * inv
    cos, sin = jnp.cos(ang), jnp.sin(ang)
    zero = jnp.zeros_like(sin)
    reps = LANES // HEAD_DIM
    cos_t = jnp.tile(jnp.concatenate([cos, cos], axis=-1), (1, reps))
    sin_lo = jnp.tile(jnp.concatenate([-sin, zero], axis=-1), (1, reps))
    sin_hi = jnp.tile(jnp.concatenate([zero, sin], axis=-1), (1, reps))
    return cos_t, sin_lo, sin_hi


def _block_bounds(chunk, q_rows, k_rows):
    cq_max = jnp.max(chunk.reshape(-1, q_rows), axis=1)
    ck_min = jnp.min(chunk.reshape(-1, k_rows), axis=1)
    need = ck_min[None, :] <= cq_max[:, None]
    last = jnp.max(jnp.where(need, jnp.arange(ck_min.shape[0], dtype=jnp.int32)[None, :] + 1, 1), axis=1)
    return last.astype(jnp.int32)


def _layer(h, pos, norm_mix_g, w_in, idx_k_norm_g, idx_k_norm_b, conv_w, conv_b, conv_norm_g, conv_norm_b,
           w_out, norm_ffn_g, w_router, b_router, w_gate_up, b_gate_up, w_down, b_down, out_gain):
    s = h.shape[0]
    f32, bf16 = jnp.float32, jnp.bfloat16
    n_main = 2 * C_CONV + 3 * D_ATTN + IDX_HEADS * IDX_DIM
    w_main = w_in[:, :n_main].astype(bf16)
    w_tail = jnp.pad(w_in[:, n_main:], ((0, 0), (0, LANES - (IDX_DIM + IDX_HEADS)))).astype(bf16)
    cos_t, sin_lo, sin_hi = _rope_tables(pos)
    cw = jnp.pad(conv_w, ((0, CONV_HALO - CONV_WIDTH), (0, 0)))
    kng = jnp.pad(idx_k_norm_g, (0, LANES - IDX_DIM))[None, :]
    knb = jnp.pad(idx_k_norm_b, (0, LANES - IDX_DIM))[None, :]
    u, q, k, v, qi, tail = _inproj(h, norm_mix_g[None, :], w_main, w_tail, cos_t, sin_lo, sin_hi, cw,
                                   conv_b[None, :], conv_norm_g[None, :], conv_norm_b[None, :], kng, knb)

    chunk = pos // CHUNK
    k_sel = min(TOPK_KEYS_MAX, s // 4)
    qb = min(Q_BLOCK, s)
    ki = tail[:, :IDX_DIM].astype(bf16)
    wi_t = tail[:, IDX_DIM:IDX_DIM + IDX_HEADS].T
    bias3 = _indexer(_block_bounds(chunk, qb, qb), qi, wi_t, chunk[None, :], ki,
                     jnp.broadcast_to(chunk[:, None], (s, qb)), k_sel)
    vt_aug = jnp.concatenate(
        [v.T.reshape(D_ATTN // LANES, LANES, s), jnp.ones((D_ATTN // LANES, ATT_V_ROWS - LANES, s), bf16)],
        axis=1).reshape(-1, s)
    a = _attention(_block_bounds(chunk, min(ATT_TQ, s), min(ATT_TK, s)), q, k, vt_aug, bias3)

    h1, xn, eidx_t, gate_t, rank_t, counts = _outproj(
        h, u, a, w_out[:C_CONV].astype(bf16), w_out[C_CONV:].astype(bf16), norm_ffn_g[None, :],
        w_router.T, b_router[:, None])
    eidx, rank = eidx_t[:TOP_K_EXPERTS].T, rank_t[:TOP_K_EXPERTS].T
    gate = gate_t[:TOP_K_EXPERTS].T

    blk = EXPERT_BLOCK
    cnt = counts[:, 0].astype(jnp.int32)
    padded = (cnt + blk - 1) // blk * blk
    end = jnp.cumsum(padded)
    start = end - padded
    n_blocks = s * TOP_K_EXPERTS // blk + N_EXPERTS
    dest = start[eidx] + rank
    tt = min(ROUTE_TILE, s)
    dest3 = dest.reshape(s // tt, 1, tt * TOP_K_EXPERTS)
    block_row = jnp.arange(n_blocks, dtype=jnp.int32) * blk
    block_expert = jnp.minimum(jnp.sum(end[None, :] <= block_row[:, None], axis=1), N_EXPERTS - 1).astype(jnp.int32)
    n_used = (end[-1:] // blk).astype(jnp.int32)

    tail = jnp.stack([end[-1], n_blocks - n_used[0]]).astype(jnp.int32)
    xs = _dispatch(start + cnt, padded - cnt, tail, dest3, xn, n_blocks * blk)
    expert_ids = jnp.arange(N_EXPERTS, dtype=jnp.int32)
    has_rows = cnt > 0
    run_of_expert = jnp.cumsum(has_rows.astype(jnp.int32)) - 1
    later = jnp.logical_and(expert_ids[None, :] > expert_ids[:, None], has_rows[None, :])
    next_of_expert = jnp.where(jnp.any(later, axis=1), jnp.argmax(later, axis=1), -1).astype(jnp.int32)
    block_is = block_expert[:, None] == expert_ids[None, :]
    run_id = jnp.sum(jnp.where(block_is, run_of_expert[None, :], 0), axis=1)
    next_expert = jnp.sum(jnp.where(block_is, next_of_expert[None, :], 0), axis=1)
    ys = _experts(block_expert, n_used, run_id, next_expert, xs,
                  w_gate_up, b_gate_up[:, None, :], w_down, b_down[:, None, :])
    return _combine(dest3, h1, gate, out_gain[None, :], ys)


def kernel(x, positions, norm_mix_g, w_in, idx_k_norm_g, idx_k_norm_b, conv_w, conv_b, conv_norm_g, conv_norm_b,
           w_out, norm_ffn_g, w_router, b_router, w_gate_up, b_gate_up, w_down, b_down, norm_final_g):
    assert x.shape[0] == 1 and norm_mix_g.shape[0] == 1, "single sequence, single layer"
    y = _layer(x[0], positions[0], norm_mix_g[0], w_in[0], idx_k_norm_g[0], idx_k_norm_b[0], conv_w[0],
               conv_b[0], conv_norm_g[0], conv_norm_b[0], w_out[0], norm_ffn_g[0], w_router[0], b_router[0],
               w_gate_up[0], b_gate_up[0], w_down[0], b_down[0], norm_final_g)
    return y[None]
```

```python
import functools
import math

import jax
import jax.numpy as jnp
from jax import lax
from jax.experimental import pallas as pl
from jax.experimental.pallas import tpu as pltpu

D_MODEL = 1024
CHUNK = 64
C_CONV = 512
CONV_WIDTH = 31
N_HEADS = 8
HEAD_DIM = 64
D_ATTN = N_HEADS * HEAD_DIM
IDX_HEADS = 8
IDX_DIM = 64
TOPK_KEYS_MAX = 256
ROPE_THETA = 10000.0
N_EXPERTS = 32
TOP_K_EXPERTS = 4
D_EXPERT = 1024
SWIGLU_LIMIT = 7.0
SWIGLU_ALPHA = 1.702
NORM_EPS = 1e-5

LANES = 128
SUBLANES = 8
CONV_HALO = 32
ROW_TILE = 512
OUT_ROW_TILE = 1024
PROJ_GROUP = 256
Q_BLOCK = 128
KEY_CHUNK = 512
SCORE_UNROLL = 2
SEARCH_BISECT_EVERY = 8
SEARCH_FRAC_CLIP = 0.1
ATT_TQ = 512
ATT_TK = 512
ATT_V_ROWS = LANES + 16
Q_SCALE = HEAD_DIM ** -0.5 * math.log2(math.e)
EXPERT_BLOCK = 256
ROUTE_TILE = 512
ROUTE_UNROLL = 4
VMEM_LIMIT = 56 * 1024 * 1024

_NT = (((1,), (1,)), ((), ()))


def _cparams(sem):
    return pltpu.CompilerParams(dimension_semantics=sem, vmem_limit_bytes=VMEM_LIMIT)


def _rope128(xp, cos_t, sin_lo, sin_hi):
    return xp * cos_t + pltpu.roll(xp, 96, 1) * sin_lo + pltpu.roll(xp, 32, 1) * sin_hi


def _inproj_kernel(x_ref, g_ref, wm_ref, wt_ref, cos_ref, slo_ref, shi_ref, cw_ref, cb_ref,
                   cng_ref, cnb_ref, kng_ref, knb_ref,
                   u_ref, q_ref, k_ref, v_ref, qi_ref, tail_ref, ubuf_ref, ushift_ref):
    ts = x_ref.shape[0]

    @pl.when(pl.program_id(0) == 0)
    def _():
        ubuf_ref[0:CONV_HALO, :] = jnp.zeros((CONV_HALO, C_CONV), jnp.float32)

    x = x_ref[...]
    ms = jnp.mean(x * x, axis=-1, keepdims=True)
    hn = (x * lax.rsqrt(ms + NORM_EPS) * g_ref[...]).astype(jnp.bfloat16)
    parts = [jnp.dot(hn, wm_ref[:, g * PROJ_GROUP:(g + 1) * PROJ_GROUP], preferred_element_type=jnp.float32)
             for g in range(wm_ref.shape[1] // PROJ_GROUP)]
    tail = jnp.dot(hn, wt_ref[...], preferred_element_type=jnp.float32)

    def proj_cols(lo, width):
        if width < PROJ_GROUP:
            off = lo % PROJ_GROUP
            return parts[lo // PROJ_GROUP][:, off:off + width]
        return jnp.concatenate([parts[g] for g in range(lo // PROJ_GROUP, (lo + width) // PROJ_GROUP)], axis=1)

    u = proj_cols(0, C_CONV) * jax.nn.sigmoid(proj_cols(C_CONV, C_CONV))
    ubuf_ref[CONV_HALO:CONV_HALO + ts, :] = u
    span = ts + CONV_HALO - SUBLANES
    for b in range(1, SUBLANES):
        ushift_ref[b - 1] = ubuf_ref[b:b + span, :]
    acc = jnp.zeros((ts, C_CONV), jnp.float32) + cb_ref[...]
    for kk in range(CONV_WIDTH):
        off = CONV_HALO - (CONV_WIDTH - 1) + kk
        a, b = off // SUBLANES * SUBLANES, off % SUBLANES
        tap = ubuf_ref[a:a + ts, :] if b == 0 else ushift_ref[b - 1, a:a + ts, :]
        acc = acc + tap * cw_ref[kk:kk + 1, :]
    ubuf_ref[0:CONV_HALO, :] = ubuf_ref[ts:ts + CONV_HALO, :]
    mu = jnp.mean(acc, axis=-1, keepdims=True)
    d = acc - mu
    var = jnp.mean(d * d, axis=-1, keepdims=True)
    yn = d * lax.rsqrt(var + NORM_EPS) * cng_ref[...] + cnb_ref[...]
    u_ref[...] = (yn * jax.nn.sigmoid(yn)).astype(jnp.bfloat16)

    cos_t, sin_lo, sin_hi = cos_ref[...], slo_ref[...], shi_ref[...]
    base = 2 * C_CONV
    for p in range(D_ATTN // LANES):
        lo = p * LANES
        qp = proj_cols(base + lo, LANES)
        q_ref[:, lo:lo + LANES] = (_rope128(qp, cos_t, sin_lo, sin_hi) * Q_SCALE).astype(jnp.bfloat16)
        kp = proj_cols(base + D_ATTN + lo, LANES)
        k_ref[:, lo:lo + LANES] = _rope128(kp, cos_t, sin_lo, sin_hi).astype(jnp.bfloat16)
        qip = proj_cols(base + 3 * D_ATTN + lo, LANES)
        qi_ref[:, lo:lo + LANES] = _rope128(qip, cos_t, sin_lo, sin_hi).astype(jnp.bfloat16)
    v_ref[...] = proj_cols(base + 2 * D_ATTN, D_ATTN).astype(jnp.bfloat16)

    lane = lax.broadcasted_iota(jnp.int32, tail.shape, 1)
    is_k = lane < IDX_DIM
    kmu = jnp.sum(jnp.where(is_k, tail, 0.0), axis=-1, keepdims=True) * (1.0 / IDX_DIM)
    kd = jnp.where(is_k, tail - kmu, 0.0)
    kvar = jnp.sum(kd * kd, axis=-1, keepdims=True) * (1.0 / IDX_DIM)
    kn = kd * lax.rsqrt(kvar + NORM_EPS) * kng_ref[...] + knb_ref[...]
    kr = _rope128(kn, cos_t, sin_lo, sin_hi)
    wi = tail * (IDX_HEADS ** -0.5 * IDX_DIM ** -0.5)
    tail_ref[...] = jnp.where(is_k, kr, jnp.where(lane < IDX_DIM + IDX_HEADS, wi, 0.0))


def _inproj(x, g, w_main, w_tail, cos_t, sin_lo, sin_hi, cw, cb, cng, cnb, kng, knb):
    s = x.shape[0]
    ts = min(ROW_TILE, s)
    row = lambda w: pl.BlockSpec((ts, w), lambda i: (i, 0))
    full = lambda a: pl.BlockSpec(a.shape, lambda i: (0,) * a.ndim)
    return pl.pallas_call(
        _inproj_kernel,
        grid=(s // ts,),
        in_specs=[row(D_MODEL), full(g), full(w_main), full(w_tail), row(LANES), row(LANES), row(LANES),
                  full(cw), full(cb), full(cng), full(cnb), full(kng), full(knb)],
        out_specs=[row(C_CONV), row(D_ATTN), row(D_ATTN), row(D_ATTN), row(D_ATTN), row(LANES)],
        out_shape=[jax.ShapeDtypeStruct((s, C_CONV), jnp.bfloat16),
                   jax.ShapeDtypeStruct((s, D_ATTN), jnp.bfloat16),
                   jax.ShapeDtypeStruct((s, D_ATTN), jnp.bfloat16),
                   jax.ShapeDtypeStruct((s, D_ATTN), jnp.bfloat16),
                   jax.ShapeDtypeStruct((s, D_ATTN), jnp.bfloat16),
                   jax.ShapeDtypeStruct((s, LANES), jnp.float32)],
        scratch_shapes=[pltpu.VMEM((ts + CONV_HALO, C_CONV), jnp.float32),
                        pltpu.VMEM((SUBLANES - 1, ts + CONV_HALO - SUBLANES, C_CONV), jnp.float32)],
        compiler_params=_cparams(("arbitrary",)),
        name="inproj",
    )(x, g, w_main, w_tail, cos_t, sin_lo, sin_hi, cw, cb, cng, cnb, kng, knb)


def _float_to_key(x):
    b = lax.bitcast_convert_type(x, jnp.int32)
    return jnp.where(b < 0, b ^ jnp.int32(0x7FFFFFFF), b)


def _key_to_float(k):
    b = jnp.where(k < 0, k ^ jnp.int32(0x7FFFFFFF), k)
    return lax.bitcast_convert_type(b, jnp.float32)


def _avg_floor(a, b):
    return (a >> 1) + (b >> 1) + (a & b & 1)


def _probit_upper(q):
    p = jnp.minimum(q, 1.0 - q)
    t = jnp.sqrt(-2.0 * jnp.log(p))
    z = t - (2.515517 + t * (0.802853 + t * 0.010328)) / (1.0 + t * (1.432788 + t * (0.189269 + t * 0.001308)))
    return jnp.where(q <= 0.5, z, -z)


def _indexer_kernel(nkb_ref, qi_ref, wi_ref, cq_ref, ki_ref, ck_ref, bias_ref, sc_ref, *, k_sel):
    qb = qi_ref.shape[0]
    s = ki_ref.shape[0]
    kc = min(KEY_CHUNK, s)
    grp = kc // SUBLANES
    n_total = s // kc
    nch = (nkb_ref[pl.program_id(0)] * Q_BLOCK + kc - 1) // kc
    f32, i32 = jnp.float32, jnp.int32
    neg_inf, pos_inf = f32(-jnp.inf), f32(jnp.inf)

    def as3(a):
        return a.reshape(grp, SUBLANES, qb)

    def rows8(v):
        return jnp.broadcast_to(v, (SUBLANES, qb))

    qi = qi_ref[...]
    q_pairs = [jnp.concatenate([qi[:, (2 * p) * IDX_DIM:(2 * p + 1) * IDX_DIM],
                                qi[:, (2 * p + 1) * IDX_DIM:(2 * p + 2) * IDX_DIM]], axis=0)
               for p in range(IDX_HEADS // 2)]
    w8 = [rows8(wi_ref[h:h + 1, :]) for h in range(IDX_HEADS)]
    cq8 = rows8(cq_ref[...])

    def score_chunks(c, carry):
        rmax, rmin, nadm, nge0, ngt0 = carry
        starts = [pl.multiple_of((c * SCORE_UNROLL + r) * kc, kc) for r in range(SCORE_UNROLL)]
        sps = [[lax.dot_general(ki_ref[pl.ds(k0, kc), :], q_pairs[p], _NT, preferred_element_type=f32)
                for p in range(IDX_HEADS // 2)] for k0 in starts]
        for k0, sp in zip(starts, sps):
            acc = jnp.zeros((grp, SUBLANES, qb), f32)
            for p in range(IDX_HEADS // 2):
                acc = acc + w8[2 * p][None] * jnp.maximum(as3(sp[p][:, :qb]), 0.0)
                acc = acc + w8[2 * p + 1][None] * jnp.maximum(as3(sp[p][:, qb:]), 0.0)
            adm = as3(ck_ref[pl.ds(k0, kc), :]) <= cq8[None]
            val = jnp.where(adm, acc, neg_inf)
            sc_ref[pl.ds(k0, kc), :] = val.reshape(kc, qb)
            rmax = jnp.maximum(rmax, jnp.max(val, axis=0))
            rmin = jnp.minimum(rmin, jnp.min(acc, axis=0))
            nadm = nadm + jnp.sum(jnp.where(adm, 1, 0), axis=0)
            nge0 = nge0 + jnp.sum(jnp.where(val >= 0.0, 1, 0), axis=0)
            ngt0 = ngt0 + jnp.sum(jnp.where(val > 0.0, 1, 0), axis=0)
        return rmax, rmin, nadm, nge0, ngt0

    assert n_total % SCORE_UNROLL == 0
    n_steps = (nch + SCORE_UNROLL - 1) // SCORE_UNROLL
    zero8 = jnp.zeros((SUBLANES, qb), i32)
    rmax8, rmin8, nadm8, nge8, ngt8 = lax.fori_loop(
        0, n_steps, score_chunks,
        (jnp.full((SUBLANES, qb), neg_inf, f32), jnp.full((SUBLANES, qb), pos_inf, f32), zero8, zero8, zero8))
    n_adm = jnp.sum(nadm8, axis=0, keepdims=True)
    row_max = jnp.max(rmax8, axis=0, keepdims=True)
    row_min = jnp.min(rmin8, axis=0, keepdims=True)
    c_ge0 = jnp.sum(nge8, axis=0, keepdims=True)
    c_gt0 = jnp.sum(ngt8, axis=0, keepdims=True)

    def count_rows(pred):
        def body(c, acc):
            parts = [acc, jnp.zeros_like(acc), jnp.zeros_like(acc), jnp.zeros_like(acc)]
            for r in range(SCORE_UNROLL):
                k0 = pl.multiple_of((c * SCORE_UNROLL + r) * kc, kc)
                m = pred(as3(sc_ref[pl.ds(k0, kc), :]), k0)
                for g in range(grp):
                    parts[g % 4] = jnp.where(m[g], parts[g % 4] + 1, parts[g % 4])
            return (parts[0] + parts[1]) + (parts[2] + parts[3])
        acc = lax.fori_loop(0, n_steps, body, jnp.zeros((SUBLANES, qb), i32))
        return jnp.sum(acc, axis=0, keepdims=True)

    def count_ge(cand):
        c8 = rows8(cand)[None]
        return count_rows(lambda x3, k0: x3 >= c8)

    k_eff = jnp.minimum(n_adm, k_sel)
    inv_n = 1.0 / (n_adm.astype(f32) + 1.0)

    def zscore(cnt):
        return _probit_upper((cnt.astype(f32) + 0.5) * inv_n)

    z_target = zscore(k_eff)

    above = c_gt0 >= k_eff
    below = c_ge0 < k_eff
    zero_key = jnp.zeros((1, qb), i32)
    lo0 = jnp.where(below, _float_to_key(row_min), zero_key)
    clo0 = jnp.where(below, n_adm, c_ge0)
    hi0 = jnp.where(above, _float_to_key(row_max) + 1, jnp.where(below, zero_key - 1, zero_key + 1))
    chi0 = jnp.where(above, 0, jnp.where(below, c_ge0, c_gt0))

    def settled(lo, hi, clo):
        return jnp.logical_or(clo == k_eff, _avg_floor(lo, hi) == lo)

    def cond(st):
        return st[7] > 0

    def step(st):
        lo, hi, clo, chi, z_lo, z_hi, done, _, phase = st
        lo_f, hi_f = _key_to_float(lo), _key_to_float(hi)
        frac = jnp.clip((z_target - z_lo) / (z_hi - z_lo), SEARCH_FRAC_CLIP, 1.0 - SEARCH_FRAC_CLIP)
        guess = lo_f + (hi_f - lo_f) * frac
        guess_ok = jnp.logical_and(guess == guess, jnp.abs(guess) < pos_inf)
        mid = jnp.maximum(_avg_floor(lo, hi), lo + 1)
        cand = jnp.clip(_float_to_key(jnp.where(guess_ok, guess, lo_f)), lo + 1, hi - 1)
        last_phase = SEARCH_BISECT_EVERY - 1
        cand = jnp.where(jnp.logical_or(phase == last_phase, jnp.logical_not(guess_ok)), mid, cand)
        cnt = count_ge(_key_to_float(cand))
        z_c = zscore(cnt)
        ge = cnt >= k_eff
        up = jnp.logical_and(done == 0, ge)
        dn = jnp.logical_and(done == 0, jnp.logical_not(ge))
        lo = jnp.where(up, cand, lo)
        clo = jnp.where(up, cnt, clo)
        z_lo = jnp.where(up, z_c, z_lo)
        hi = jnp.where(dn, cand, hi)
        chi = jnp.where(dn, cnt, chi)
        z_hi = jnp.where(dn, z_c, z_hi)
        done = jnp.where(settled(lo, hi, clo), 1, done)
        return (lo, hi, clo, chi, z_lo, z_hi, done, jnp.sum(1 - done),
                jnp.where(phase == last_phase, 0, phase + 1))

    done0 = jnp.where(settled(lo0, hi0, clo0), 1, 0)
    st0 = (lo0, hi0, clo0, chi0, zscore(clo0), zscore(chi0), done0, jnp.sum(1 - done0), i32(0))
    lo, hi, clo, chi = lax.while_loop(cond, step, st0)[:4]
    thr = _key_to_float(lo)
    thr8 = rows8(thr)[None]

    tie = clo > k_eff
    need = k_eff - chi
    n_tie = jnp.sum(jnp.where(tie, 1, 0))
    key_iota = (lax.broadcasted_iota(i32, (grp, SUBLANES, qb), 0) * SUBLANES
                + lax.broadcasted_iota(i32, (grp, SUBLANES, qb), 1))

    def store_bias(k0, sel):
        bias_ref[0, pl.ds(k0, kc), :] = jnp.where(sel, 0.0, neg_inf).reshape(kc, qb).astype(jnp.bfloat16)

    @pl.when(n_tie == 0)
    def _():
        def emit_chunk(c, carry):
            k0 = pl.multiple_of(c * kc, kc)
            store_bias(k0, as3(sc_ref[pl.ds(k0, kc), :]) >= thr8)
            return carry
        lax.fori_loop(0, nch, emit_chunk, 0)

    @pl.when(n_tie > 0)
    def _():
        def cnt_upto(m):
            m8 = rows8(m)[None]
            return count_rows(lambda x3, k0: jnp.logical_and(x3 == thr8, key_iota + k0 <= m8))

        def body(_, jj):
            jl, jh = jj
            m = (jl + jh) >> 1
            ok = cnt_upto(m) >= need
            return jnp.where(ok, jl, m), jnp.where(ok, m, jh)

        n_it = max(1, (s - 1).bit_length()) + 1
        _, jh = lax.fori_loop(0, n_it, body, (jnp.full((1, qb), -1, i32), jnp.full((1, qb), s - 1, i32)))
        jlim8 = rows8(jnp.where(tie, jh, s))[None]

        def emit_chunk(c, carry):
            k0 = pl.multiple_of(c * kc, kc)
            x3 = as3(sc_ref[pl.ds(k0, kc), :])
            store_bias(k0, jnp.logical_or(x3 > thr8, jnp.logical_and(x3 == thr8, key_iota + k0 <= jlim8)))
            return carry
        lax.fori_loop(0, nch, emit_chunk, 0)

    def fill_chunk(c, carry):
        k0 = pl.multiple_of(c * kc, kc)
        bias_ref[0, pl.ds(k0, kc), :] = jnp.full((kc, qb), neg_inf, jnp.bfloat16)
        return carry

    lax.fori_loop(nch, n_total, fill_chunk, 0)


def _indexer(nkb, qi, wi_t, cq_row, ki, ck_lanes, k_sel):
    s = qi.shape[0]
    qb = min(Q_BLOCK, s)
    grid_spec = pltpu.PrefetchScalarGridSpec(
        num_scalar_prefetch=1,
        grid=(s // qb,),
        in_specs=[pl.BlockSpec((qb, D_ATTN), lambda i, n: (i, 0)),
                  pl.BlockSpec((IDX_HEADS, qb), lambda i, n: (0, i)),
                  pl.BlockSpec((1, qb), lambda i, n: (0, i)),
                  pl.BlockSpec((s, IDX_DIM), lambda i, n: (0, 0)),
                  pl.BlockSpec((s, qb), lambda i, n: (0, 0))],
        out_specs=pl.BlockSpec((1, s, qb), lambda i, n: (i, 0, 0)),
        scratch_shapes=[pltpu.VMEM((s, qb), jnp.float32)],
    )
    return pl.pallas_call(
        functools.partial(_indexer_kernel, k_sel=k_sel),
        grid_spec=grid_spec,
        out_shape=jax.ShapeDtypeStruct((s // qb, s, qb), jnp.bfloat16),
        compiler_params=_cparams(("arbitrary",)),
        name="indexer",
    )(nkb, qi, wi_t, cq_row, ki, ck_lanes)


def _attention_kernel(nkt_ref, q_ref, k_ref, vt_ref, b_ref, o_ref, qm_ref, m_ref, acc_ref):
    i, j = pl.program_id(0), pl.program_id(1)
    tq = q_ref.shape[0]
    n_pairs = D_ATTN // LANES
    bf16 = jnp.bfloat16

    @pl.when(j == 0)
    def _():
        q = q_ref[...]
        lane = lax.broadcasted_iota(jnp.int32, (tq, LANES), 1)
        zero = jnp.zeros((tq, LANES), q.dtype)
        for p in range(n_pairs):
            qp = q[:, p * LANES:(p + 1) * LANES]
            qm_ref[2 * p] = jnp.where(lane < HEAD_DIM, qp, zero)
            qm_ref[2 * p + 1] = jnp.where(lane < HEAD_DIM, zero, qp)
        m_ref[...] = jnp.full(m_ref.shape, -1e30, jnp.float32)
        acc_ref[...] = jnp.zeros(acc_ref.shape, jnp.float32)

    @pl.when(j < nkt_ref[i])
    def _():
        bias = jnp.concatenate([b_ref[r] for r in range(b_ref.shape[0])], axis=1)
        scores = []
        for h in range(N_HEADS):
            kp = k_ref[:, (h // 2) * LANES:(h // 2 + 1) * LANES]
            st = lax.dot_general(kp, qm_ref[h], _NT, preferred_element_type=jnp.float32)
            scores.append(st.astype(bf16) + bias)
        for h in range(N_HEADS):
            p = h // 2
            vtp = vt_ref[p * ATT_V_ROWS:(p + 1) * ATT_V_ROWS, :]
            st = scores[h]
            m_prev = m_ref[h]
            m_new = jnp.maximum(m_prev, jnp.max(st, axis=0, keepdims=True).astype(jnp.float32))
            alpha = jnp.exp2(m_prev - m_new)
            pt = jnp.exp2(st - m_new.astype(bf16))
            acc_ref[h] = alpha * acc_ref[h] + jnp.dot(vtp, pt, preferred_element_type=jnp.float32)
            m_ref[h] = m_new

    @pl.when(j == nkt_ref[i] - 1)
    def _():
        row = lax.broadcasted_iota(jnp.int32, (LANES, tq), 0)
        for p in range(n_pairs):
            a0, a1 = acc_ref[2 * p], acc_ref[2 * p + 1]
            o0 = a0[:LANES] / a0[LANES:LANES + 1]
            o1 = a1[:LANES] / a1[LANES:LANES + 1]
            o_ref[:, p * LANES:(p + 1) * LANES] = jnp.where(row < HEAD_DIM, o0, o1).T.astype(o_ref.dtype)


def _attention(nkt, q, k, vt_aug, bias3):
    s = q.shape[0]
    tq, tk = min(ATT_TQ, s), min(ATT_TK, s)
    qb = bias3.shape[2]
    last = lambda i, j, n: jnp.minimum(j, n[i] - 1)
    grid_spec = pltpu.PrefetchScalarGridSpec(
        num_scalar_prefetch=1,
        grid=(s // tq, s // tk),
        in_specs=[pl.BlockSpec((tq, D_ATTN), lambda i, j, n: (i, 0)),
                  pl.BlockSpec((tk, D_ATTN), lambda i, j, n: (last(i, j, n), 0)),
                  pl.BlockSpec((vt_aug.shape[0], tk), lambda i, j, n: (0, last(i, j, n))),
                  pl.BlockSpec((tq // qb, tk, qb), lambda i, j, n: (i, last(i, j, n), 0))],
        out_specs=pl.BlockSpec((tq, D_ATTN), lambda i, j, n: (i, 0)),
        scratch_shapes=[pltpu.VMEM((N_HEADS, tq, LANES), jnp.bfloat16),
                        pltpu.VMEM((N_HEADS, 1, tq), jnp.float32),
                        pltpu.VMEM((N_HEADS, ATT_V_ROWS, tq), jnp.float32)],
    )
    return pl.pallas_call(
        _attention_kernel,
        grid_spec=grid_spec,
        out_shape=jax.ShapeDtypeStruct((s, D_ATTN), jnp.bfloat16),
        compiler_params=_cparams(("arbitrary", "arbitrary")),
        name="attention",
    )(nkt, q, k, vt_aug, bias3)


def _outproj_kernel(x_ref, u_ref, a_ref, wu_ref, wa_ref, g_ref, wr_ref, br_ref,
                    h_ref, xn_ref, eidx_ref, gate_ref, rank_ref, cnt_ref, carry_ref):
    ts = x_ref.shape[0]

    @pl.when(pl.program_id(0) == 0)
    def _():
        carry_ref[...] = jnp.zeros(carry_ref.shape, jnp.float32)

    h = (x_ref[...]
         + jnp.dot(u_ref[...], wu_ref[...], preferred_element_type=jnp.float32)
         + jnp.dot(a_ref[...], wa_ref[...], preferred_element_type=jnp.float32))
    h_ref[...] = h
    ms = jnp.mean(h * h, axis=-1, keepdims=True)
    xn = h * lax.rsqrt(ms + NORM_EPS) * g_ref[...]
    xn_ref[...] = xn
    logits_t = lax.dot_general(wr_ref[...], xn, _NT, preferred_element_type=jnp.float32,
                               precision=lax.Precision.HIGHEST) + br_ref[...]
    erow = lax.broadcasted_iota(jnp.int32, (N_EXPERTS, ts), 0)
    work = logits_t
    vals, idxs = [], []
    multi = jnp.zeros((N_EXPERTS, ts), jnp.float32)
    for _ in range(TOP_K_EXPERTS):
        mx = jnp.max(work, axis=0, keepdims=True)
        ix = jnp.min(jnp.where(work == mx, erow, N_EXPERTS), axis=0, keepdims=True)
        hit = erow == ix
        multi = jnp.where(hit, 1.0, multi)
        work = jnp.where(hit, -jnp.inf, work)
        vals.append(mx)
        idxs.append(ix)
    ex = [jnp.exp(v - vals[0]) for v in vals]
    den = ex[0] + ex[1] + ex[2] + ex[3]

    r = lax.broadcasted_iota(jnp.int32, (ts, ts), 0)
    c = lax.broadcasted_iota(jnp.int32, (ts, ts), 1)
    earlier = jnp.where(r < c, 1.0, 0.0).astype(jnp.bfloat16)
    prior = jnp.dot(multi.astype(jnp.bfloat16), earlier, preferred_element_type=jnp.float32) + carry_ref[...]
    row8 = lax.broadcasted_iota(jnp.int32, (SUBLANES, ts), 0)
    eidx = jnp.zeros((SUBLANES, ts), jnp.int32)
    gate = jnp.zeros((SUBLANES, ts), jnp.float32)
    rank = jnp.zeros((SUBLANES, ts), jnp.int32)
    for kk in range(TOP_K_EXPERTS):
        rk = jnp.sum(jnp.where(erow == idxs[kk], prior, 0.0), axis=0, keepdims=True)
        eidx = jnp.where(row8 == kk, idxs[kk], eidx)
        gate = jnp.where(row8 == kk, ex[kk] / den, gate)
        rank = jnp.where(row8 == kk, rk.astype(jnp.int32), rank)
    eidx_ref[...] = eidx
    gate_ref[...] = gate
    rank_ref[...] = rank
    carry_ref[...] = carry_ref[...] + jnp.sum(multi, axis=1, keepdims=True)
    cnt_ref[...] = jnp.broadcast_to(carry_ref[...], cnt_ref.shape)


def _outproj(x, u, a, w_u, w_a, g, w_r, b_r):
    s = x.shape[0]
    ts = min(OUT_ROW_TILE, s)
    row = lambda w: pl.BlockSpec((ts, w), lambda i: (i, 0))
    col = lambda: pl.BlockSpec((SUBLANES, ts), lambda i: (0, i))
    full = lambda arr: pl.BlockSpec(arr.shape, lambda i: (0,) * arr.ndim)
    return pl.pallas_call(
        _outproj_kernel,
        grid=(s // ts,),
        in_specs=[row(D_MODEL), row(C_CONV), row(D_ATTN), full(w_u), full(w_a), full(g), full(w_r), full(b_r)],
        out_specs=[row(D_MODEL), row(D_MODEL), col(), col(), col(),
                   pl.BlockSpec((N_EXPERTS, LANES), lambda i: (0, 0))],
        out_shape=[jax.ShapeDtypeStruct((s, D_MODEL), jnp.float32),
                   jax.ShapeDtypeStruct((s, D_MODEL), jnp.float32),
                   jax.ShapeDtypeStruct((SUBLANES, s), jnp.int32),
                   jax.ShapeDtypeStruct((SUBLANES, s), jnp.float32),
                   jax.ShapeDtypeStruct((SUBLANES, s), jnp.int32),
                   jax.ShapeDtypeStruct((N_EXPERTS, LANES), jnp.float32)],
        scratch_shapes=[pltpu.VMEM((N_EXPERTS, 1), jnp.float32)],
        compiler_params=_cparams(("arbitrary",)),
        name="outproj_router",
    )(x, u, a, w_u, w_a, g, w_r, b_r)


def _row_copy(src_ref, dst_ref, sem, src_row, dst_row):
    return pltpu.make_async_copy(src_ref.at[pl.ds(src_row, 1), :], dst_ref.at[pl.ds(dst_row, 1), :], sem)


def _pad_copies(pad_start_ref, pad_len_ref, zero_ref, xs_ref, zsem, e):
    pos, n = pad_start_ref[e], pad_len_ref[e]
    head = jnp.minimum((-pos) % SUBLANES, n)
    out = []
    for r in range(SUBLANES - 1):
        cp = pltpu.make_async_copy(zero_ref.at[pl.ds(0, 1), :], xs_ref.at[pl.ds(pos + r, 1), :], zsem)
        out.append((r < head, cp))
    pos, n = pos + head, n - head
    for bit in [1 << i for i in reversed(range(SUBLANES.bit_length() - 1, EXPERT_BLOCK.bit_length() - 1))]:
        dst = xs_ref.at[pl.ds(pl.multiple_of(pos, SUBLANES), bit), :]
        out.append(((n & bit) != 0, pltpu.make_async_copy(zero_ref.at[pl.ds(0, bit), :], dst, zsem)))
        pos = pos + (n & bit)
    return out


def _dispatch_kernel(pad_start_ref, pad_len_ref, tail_ref, dest_ref, xn_ref, xs_ref, zero_ref, sem, zsem):
    tt = xn_ref.shape[0]

    @pl.when(pl.program_id(0) == 0)
    def _():
        zero_ref[...] = jnp.zeros(zero_ref.shape, zero_ref.dtype)
        pads = functools.partial(_pad_copies, pad_start_ref, pad_len_ref, zero_ref, xs_ref, zsem)

        def tail_copy(t):
            row0 = pl.multiple_of(tail_ref[0] + t * EXPERT_BLOCK, EXPERT_BLOCK)
            return pltpu.make_async_copy(zero_ref, xs_ref.at[pl.ds(row0, EXPERT_BLOCK), :], zsem)

        def pad_start(e, carry):
            for cond, cp in pads(e):
                pl.when(cond)(cp.start)
            return carry

        def pad_wait(e, carry):
            for cond, cp in pads(e):
                pl.when(cond)(cp.wait)
            return carry

        def tail_start(t, carry):
            tail_copy(t).start()
            return carry

        def tail_wait(t, carry):
            tail_copy(t).wait()
            return carry

        lax.fori_loop(0, N_EXPERTS, pad_start, 0)
        lax.fori_loop(0, tail_ref[1], tail_start, 0)
        lax.fori_loop(0, N_EXPERTS, pad_wait, 0)
        lax.fori_loop(0, tail_ref[1], tail_wait, 0)

    def start(r, carry):
        for kk in range(TOP_K_EXPERTS):
            _row_copy(xn_ref, xs_ref, sem, r, dest_ref[0, 0, r * TOP_K_EXPERTS + kk]).start(priority=kk % 2)
        return carry

    lax.fori_loop(0, tt, start, 0, unroll=ROUTE_UNROLL)
    for _ in range(TOP_K_EXPERTS):
        pltpu.make_async_copy(xn_ref, xs_ref.at[pl.ds(0, tt), :], sem).wait()


def _dispatch(pad_start, pad_len, tail, dest3, xn, n_rows):
    s = xn.shape[0]
    tt = min(ROUTE_TILE, s)
    grid_spec = pltpu.PrefetchScalarGridSpec(
        num_scalar_prefetch=3,
        grid=(s // tt,),
        in_specs=[pl.BlockSpec((1, 1, tt * TOP_K_EXPERTS), lambda i, *_: (i, 0, 0), memory_space=pltpu.SMEM),
                  pl.BlockSpec((tt, D_MODEL), lambda i, *_: (i, 0))],
        out_specs=pl.BlockSpec(memory_space=pl.ANY),
        scratch_shapes=[pltpu.VMEM((EXPERT_BLOCK, D_MODEL), xn.dtype),
                        pltpu.SemaphoreType.DMA(()),
                        pltpu.SemaphoreType.DMA(())],
    )
    return pl.pallas_call(
        _dispatch_kernel,
        grid_spec=grid_spec,
        out_shape=jax.ShapeDtypeStruct((n_rows, D_MODEL), xn.dtype),
        compiler_params=_cparams(("arbitrary",)),
        name="dispatch",
    )(pad_start, pad_len, tail, dest3, xn)


def _expert_weight_copies(wgu_hbm, wdn_hbm, wgu_buf, wdn_buf, sem, expert, slot):
    return (pltpu.make_async_copy(wgu_hbm.at[expert], wgu_buf.at[slot], sem.at[0, slot]),
            pltpu.make_async_copy(wdn_hbm.at[expert], wdn_buf.at[slot], sem.at[1, slot]))


def _experts_kernel(be_ref, nb_ref, run_ref, nxt_ref, xs_ref, wgu_hbm, bgu_ref, wdn_hbm, bdn_ref, ys_ref,
                    wgu_buf, wdn_buf, wgu_bf, wdn_bf, sem):
    b = pl.program_id(0)
    active = b < nb_ref[0]
    fresh = jnp.logical_or(b == 0, be_ref[b] != be_ref[jnp.maximum(b - 1, 0)])
    slot = run_ref[b] % 2
    copies = functools.partial(_expert_weight_copies, wgu_hbm, wdn_hbm, wgu_buf, wdn_buf, sem)

    @pl.when(jnp.logical_and(active, b == 0))
    def _():
        for cp in copies(be_ref[b], slot):
            cp.start()

    @pl.when(jnp.logical_and(active, fresh))
    def _():
        for cp in copies(be_ref[b], slot):
            cp.wait()

        @pl.when(nxt_ref[b] >= 0)
        def _():
            for cp in copies(nxt_ref[b], 1 - slot):
                cp.start()

        wgu_bf[...] = wgu_buf[slot].astype(jnp.bfloat16)
        wdn_bf[...] = wdn_buf[slot].astype(jnp.bfloat16)

    @pl.when(active)
    def _():
        half = xs_ref.shape[0] // 2
        gus = [jnp.dot(xs_ref[r * half:(r + 1) * half, :].astype(jnp.bfloat16), wgu_bf[...],
                       preferred_element_type=jnp.float32) + bgu_ref[0] for r in range(2)]
        for r in range(2):
            g = jnp.minimum(gus[r][:, :D_EXPERT], SWIGLU_LIMIT)
            u = jnp.clip(gus[r][:, D_EXPERT:], -SWIGLU_LIMIT, SWIGLU_LIMIT)
            hdn = g * jax.nn.sigmoid(SWIGLU_ALPHA * g) * (u + 1.0)
            ys_ref[r * half:(r + 1) * half, :] = jnp.dot(hdn.astype(jnp.bfloat16), wdn_bf[...],
                                                        preferred_element_type=jnp.float32) + bdn_ref[0]

    @pl.when(jnp.logical_not(active))
    def _():
        ys_ref[...] = jnp.zeros(ys_ref.shape, ys_ref.dtype)


def _experts(block_expert, n_used, run_id, next_expert, xs, w_gu, b_gu, w_dn, b_dn):
    n_rows = xs.shape[0]
    blk = EXPERT_BLOCK
    row_map = lambda b, be, nb, run, nxt: (jnp.maximum(jnp.minimum(b, nb[0] - 1), 0), 0)
    exp_map = lambda b, be, nb, run, nxt: (be[b], 0, 0)
    grid_spec = pltpu.PrefetchScalarGridSpec(
        num_scalar_prefetch=4,
        grid=(n_rows // blk,),
        in_specs=[pl.BlockSpec((blk, D_MODEL), row_map),
                  pl.BlockSpec(memory_space=pl.ANY),
                  pl.BlockSpec((1, 1, 2 * D_EXPERT), exp_map),
                  pl.BlockSpec(memory_space=pl.ANY),
                  pl.BlockSpec((1, 1, D_MODEL), exp_map)],
        out_specs=pl.BlockSpec((blk, D_MODEL), lambda b, be, nb, run, nxt: (b, 0)),
        scratch_shapes=[pltpu.VMEM((2, D_MODEL, 2 * D_EXPERT), jnp.float32),
                        pltpu.VMEM((2, D_EXPERT, D_MODEL), jnp.float32),
                        pltpu.VMEM((D_MODEL, 2 * D_EXPERT), jnp.bfloat16),
                        pltpu.VMEM((D_EXPERT, D_MODEL), jnp.bfloat16),
                        pltpu.SemaphoreType.DMA((2, 2))],
    )
    return pl.pallas_call(
        _experts_kernel,
        grid_spec=grid_spec,
        out_shape=jax.ShapeDtypeStruct((n_rows, D_MODEL), jnp.float32),
        compiler_params=_cparams(("arbitrary",)),
        name="experts",
    )(block_expert, n_used, run_id, next_expert, xs, w_gu, b_gu, w_dn, b_dn)


def _combine_kernel(dest_ref, h_ref, gate_ref, g_ref, ys_ref, o_ref, buf_ref, sem):
    tt = h_ref.shape[0]

    def start(r, carry):
        for kk in range(TOP_K_EXPERTS):
            _row_copy(ys_ref, buf_ref.at[kk], sem, dest_ref[0, 0, r * TOP_K_EXPERTS + kk], r).start(priority=kk % 2)
        return carry

    lax.fori_loop(0, tt, start, 0, unroll=ROUTE_UNROLL)
    for kk in range(TOP_K_EXPERTS):
        pltpu.make_async_copy(ys_ref.at[pl.ds(0, tt), :], buf_ref.at[kk], sem).wait()
    gate = gate_ref[...]
    h = h_ref[...]
    for kk in range(TOP_K_EXPERTS):
        h = h + gate[:, kk:kk + 1] * buf_ref[kk]
    ms = jnp.mean(h * h, axis=-1, keepdims=True)
    o_ref[...] = h * lax.rsqrt(ms + NORM_EPS) * g_ref[...]


def _combine(dest3, h, gate, g_final, ys):
    s = h.shape[0]
    tt = min(ROUTE_TILE, s)
    return pl.pallas_call(
        _combine_kernel,
        grid=(s // tt,),
        in_specs=[pl.BlockSpec((1, 1, tt * TOP_K_EXPERTS), lambda i: (i, 0, 0), memory_space=pltpu.SMEM),
                  pl.BlockSpec((tt, D_MODEL), lambda i: (i, 0)),
                  pl.BlockSpec((tt, TOP_K_EXPERTS), lambda i: (i, 0)),
                  pl.BlockSpec((1, D_MODEL), lambda i: (0, 0)),
                  pl.BlockSpec(memory_space=pl.ANY)],
        out_specs=pl.BlockSpec((tt, D_MODEL), lambda i: (i, 0)),
        out_shape=jax.ShapeDtypeStruct((s, D_MODEL), jnp.float32),
        scratch_shapes=[pltpu.VMEM((TOP_K_EXPERTS, tt, D_MODEL), jnp.float32),
                        pltpu.SemaphoreType.DMA(())],
        compiler_params=_cparams(("arbitrary",)),
        name="combine",
    )(dest3, h, gate, g_final, ys)


def _rope_tables(pos):
    half = HEAD_DIM // 2
    inv = ROPE_THETA ** (-jnp.arange(half, dtype=jnp.float32) / half)
    ang = pos.astype(jnp.float32)[:, None] * inv
    cos, sin = jnp.cos(ang), jnp.sin(ang)
    zero = jnp.zeros_like(sin)
    reps = LANES // HEAD_DIM
    cos_t = jnp.tile(jnp.concatenate([cos, cos], axis=-1), (1, reps))
    sin_lo = jnp.tile(jnp.concatenate([-sin, zero], axis=-1), (1, reps))
    sin_hi = jnp.tile(jnp.concatenate([zero, sin], axis=-1), (1, reps))
    return cos_t, sin_lo, sin_hi


def _block_bounds(chunk, q_rows, k_rows):
    cq_max = jnp.max(chunk.reshape(-1, q_rows), axis=1)
    ck_min = jnp.min(chunk.reshape(-1, k_rows), axis=1)
    need = ck_min[None, :] <= cq_max[:, None]
    last = jnp.max(jnp.where(need, jnp.arange(ck_min.shape[0], dtype=jnp.int32)[None, :] + 1, 1), axis=1)
    return last.astype(jnp.int32)


def _layer(h, pos, norm_mix_g, w_in, idx_k_norm_g, idx_k_norm_b, conv_w, conv_b, conv_norm_g, conv_norm_b,
           w_out, norm_ffn_g, w_router, b_router, w_gate_up, b_gate_up, w_down, b_down, out_gain):
    s = h.shape[0]
    f32, bf16 = jnp.float32, jnp.bfloat16
    n_main = 2 * C_CONV + 3 * D_ATTN + IDX_HEADS * IDX_DIM
    w_main = w_in[:, :n_main].astype(bf16)
    w_tail = jnp.pad(w_in[:, n_main:], ((0, 0), (0, LANES - (IDX_DIM + IDX_HEADS)))).astype(bf16)
    cos_t, sin_lo, sin_hi = _rope_tables(pos)
    cw = jnp.pad(conv_w, ((0, CONV_HALO - CONV_WIDTH), (0, 0)))
    kng = jnp.pad(idx_k_norm_g, (0, LANES - IDX_DIM))[None, :]
    knb = jnp.pad(idx_k_norm_b, (0, LANES - IDX_DIM))[None, :]
    u, q, k, v, qi, tail = _inproj(h, norm_mix_g[None, :], w_main, w_tail, cos_t, sin_lo, sin_hi, cw,
                                   conv_b[None, :], conv_norm_g[None, :], conv_norm_b[None, :], kng, knb)

    chunk = pos // CHUNK
    k_sel = min(TOPK_KEYS_MAX, s // 4)
    qb = min(Q_BLOCK, s)
    ki = tail[:, :IDX_DIM].astype(bf16)
    wi_t = tail[:, IDX_DIM:IDX_DIM + IDX_HEADS].T
    bias3 = _indexer(_block_bounds(chunk, qb, qb), qi, wi_t, chunk[None, :], ki,
                     jnp.broadcast_to(chunk[:, None], (s, qb)), k_sel)
    vt_aug = jnp.concatenate(
        [v.T.reshape(D_ATTN // LANES, LANES, s), jnp.ones((D_ATTN // LANES, ATT_V_ROWS - LANES, s), bf16)],
        axis=1).reshape(-1, s)
    a = _attention(_block_bounds(chunk, min(ATT_TQ, s), min(ATT_TK, s)), q, k, vt_aug, bias3)

    h1, xn, eidx_t, gate_t, rank_t, counts = _outproj(
        h, u, a, w_out[:C_CONV].astype(bf16), w_out[C_CONV:].astype(bf16), norm_ffn_g[None, :],
        w_router.T, b_router[:, None])
    eidx, rank = eidx_t[:TOP_K_EXPERTS].T, rank_t[:TOP_K_EXPERTS].T
    gate = gate_t[:TOP_K_EXPERTS].T

    blk = EXPERT_BLOCK
    cnt = counts[:, 0].astype(jnp.int32)
    padded = (cnt + blk - 1) // blk * blk
    end = jnp.cumsum(padded)
    start = end - padded
    n_blocks = s * TOP_K_EXPERTS // blk + N_EXPERTS
    dest = start[eidx] + rank
    tt = min(ROUTE_TILE, s)
    dest3 = dest.reshape(s // tt, 1, tt * TOP_K_EXPERTS)
    block_row = jnp.arange(n_blocks, dtype=jnp.int32) * blk
    block_expert = jnp.minimum(jnp.sum(end[None, :] <= block_row[:, None], axis=1), N_EXPERTS - 1).astype(jnp.int32)
    n_used = (end[-1:] // blk).astype(jnp.int32)

    tail = jnp.stack([end[-1], n_blocks - n_used[0]]).astype(jnp.int32)
    xs = _dispatch(start + cnt, padded - cnt, tail, dest3, xn, n_blocks * blk)
    expert_ids = jnp.arange(N_EXPERTS, dtype=jnp.int32)
    has_rows = cnt > 0
    run_of_expert = jnp.cumsum(has_rows.astype(jnp.int32)) - 1
    later = jnp.logical_and(expert_ids[None, :] > expert_ids[:, None], has_rows[None, :])
    next_of_expert = jnp.where(jnp.any(later, axis=1), jnp.argmax(later, axis=1), -1).astype(jnp.int32)
    block_is = block_expert[:, None] == expert_ids[None, :]
    run_id = jnp.sum(jnp.where(block_is, run_of_expert[None, :], 0), axis=1)
    next_expert = jnp.sum(jnp.where(block_is, next_of_expert[None, :], 0), axis=1)
    ys = _experts(block_expert, n_used, run_id, next_expert, xs,
                  w_gate_up, b_gate_up[:, None, :], w_down, b_down[:, None, :])
    return _combine(dest3, h1, gate, out_gain[None, :], ys)


def kernel(x, positions, norm_mix_g, w_in, idx_k_norm_g, idx_k_norm_b, conv_w, conv_b, conv_norm_g, conv_norm_b,
           w_out, norm_ffn_g, w_router, b_router, w_gate_up, b_gate_up, w_down, b_down, norm_final_g):
    assert x.shape[0] == 1 and norm_mix_g.shape[0] == 1, "single sequence, single layer"
    y = _layer(x[0], positions[0], norm_mix_g[0], w_in[0], idx_k_norm_g[0], idx_k_norm_b[0], conv_w[0],
               conv_b[0], conv_norm_g[0], conv_norm_b[0], w_out[0], norm_ffn_g[0], w_router[0], b_router[0],
               w_gate_up[0], b_gate_up[0], w_down[0], b_down[0], norm_final_g)
    return y[None]
```
